```python
import math
import jax, jax.numpy as jnp
from jax import lax
import numpy as np

D_MODEL = 1024
BATCH = 32
SEQ = 256
DEPTH = 2
DEC_BATCH = 2
DEC_SEQ = 4096
PAST_LEN = 512

GRID_W = 64
Q_BLOCK = 128
RMS_EPS = 1e-6
ROPE_THETA = 10000.0
MLA_HEADS = 8
MLA_NOPE = 64
MLA_ROPE = 32
MLA_V = 64
Q_LORA = 256
KV_LORA = 128
GQA_HEADS = 8
GQA_KV_HEADS = 2
GQA_HEAD_DIM = 64
POOL_WINDOWS = (2, 4, 8, 16)
POOL_GROUP = 128
POOL_WIDTH = POOL_GROUP * len(POOL_WINDOWS)
D_FF = 2816
N_BRANCH = 3
MLA_OUT = MLA_HEADS * MLA_V
GQA_OUT = GQA_HEADS * GQA_HEAD_DIM
IN_SPLITS = (Q_LORA, KV_LORA + MLA_ROPE, GQA_HEADS * GQA_HEAD_DIM, GQA_KV_HEADS * GQA_HEAD_DIM,
             GQA_KV_HEADS * GQA_HEAD_DIM, POOL_WIDTH, N_BRANCH * D_MODEL)
IN_COLS = Q_LORA + KV_LORA + MLA_ROPE + (GQA_HEADS + 2 * GQA_KV_HEADS) * GQA_HEAD_DIM + POOL_WIDTH + N_BRANCH * D_MODEL

kernel_name = "hybrid_diffusion_mla_gqa_pool_convffn_step"


def rms_norm(x, g):
    xf = x.astype(jnp.float32)
    xf = xf * lax.rsqrt(jnp.mean(xf * xf, axis=-1, keepdims=True) + RMS_EPS)
    return xf.astype(x.dtype) * g


def split_cols(z):
    outs = []
    start = 0
    for w in IN_SPLITS:
        outs.append(z[..., start:start + w])
        start += w
    return outs


def axial_angles(n_tokens, dim_axis):
    n_rows = n_tokens // GRID_W
    row = jnp.repeat(jnp.arange(n_rows), GRID_W).astype(jnp.float32)
    col = jnp.tile(jnp.arange(GRID_W), n_rows).astype(jnp.float32)
    freqs = ROPE_THETA ** (-jnp.arange(0, dim_axis, 2, dtype=jnp.float32) / dim_axis)
    return row[:, None] * freqs[None, :], col[:, None] * freqs[None, :]


def rope_rotate(x, ang):
    half = x.shape[-1] // 2
    x1, x2 = x[..., :half], x[..., half:]
    cos = jnp.cos(ang)[None, :, None, :].astype(x.dtype)
    sin = jnp.sin(ang)[None, :, None, :].astype(x.dtype)
    return jnp.concatenate([x1 * cos - x2 * sin, x1 * sin + x2 * cos], axis=-1)


def axial_rope(x):
    T, d = x.shape[1], x.shape[-1]
    ang_row, ang_col = axial_angles(T, d // 2)
    h = d // 2
    return jnp.concatenate([rope_rotate(x[..., :h], ang_row), rope_rotate(x[..., h:], ang_col)], axis=-1)


def block_attention(q, k, v):
    B, T, H, D = q.shape
    G = k.shape[2]
    R = H // G
    Dv = v.shape[-1]
    scale = D ** -0.5
    qb = q.reshape(B, T // Q_BLOCK, Q_BLOCK, G, R, D).transpose(1, 0, 2, 3, 4, 5)

    def one_block(qi):
        s = jnp.einsum('bqgrd,bsgd->bgrqs', qi, k).astype(jnp.float32) * scale
        pr = jax.nn.softmax(s, axis=-1).astype(v.dtype)
        return jnp.einsum('bgrqs,bsgv->bqgrv', pr, v)

    o = lax.map(one_block, qb)
    return o.transpose(1, 0, 2, 3, 4, 5).reshape(B, T, H * Dv)


def pool_mixer(u, w_pool, pool_scale):
    B, T, _ = u.shape
    uf = u.astype(jnp.float32)
    cs = jnp.concatenate([jnp.zeros((B, 1, POOL_WIDTH), jnp.float32), jnp.cumsum(uf, axis=1)], axis=1)
    t = jnp.arange(T)
    pooled = []
    for g, w in enumerate(POOL_WINDOWS):
        lo = jnp.clip(t - w // 2, 0, T)
        hi = jnp.clip(t - w // 2 + w, 0, T)
        seg = cs[:, :, g * POOL_GROUP:(g + 1) * POOL_GROUP]
        cnt = (hi - lo).astype(jnp.float32)[None, :, None]
        pooled.append((seg[:, hi] - seg[:, lo]) / cnt)
    pooled = jnp.concatenate(pooled, axis=-1) - uf
    pooled = pooled.astype(u.dtype).reshape(B, T, len(POOL_WINDOWS), POOL_GROUP)
    out = jnp.einsum('btgc,gcd->btgd', pooled, w_pool).reshape(B, T, POOL_WIDTH)
    return out * pool_scale


def conv_ffn(h, w_up, conv_w, conv_b, w_down):
    u = h @ w_up
    g, val = u[..., :D_FF], u[..., D_FF:]
    gp = jnp.pad(g, ((0, 0), (1, 1), (0, 0)))
    g = gp[:, :-2] * conv_w[0] + gp[:, 1:-1] * conv_w[1] + gp[:, 2:] * conv_w[2] + conv_b
    return (jax.nn.silu(g) * val) @ w_down


def trunk_layer(x, mod, p, cache=None):
    B, T, _ = x.shape
    sh1, sc1, g1, sh2, sc2, g2 = jnp.split(mod, 6, axis=-1)
    h = rms_norm(x, p['g_norm_mix']) * (1 + sc1) + sh1
    q_a, kv_a, gq, gk, gv, pool_in, gate_logits = split_cols(h @ p['w_in'])
    q_mla = (rms_norm(q_a, p['g_q_a']) @ p['w_q_b']).reshape(B, T, MLA_HEADS, MLA_NOPE + MLA_ROPE)
    ckv = rms_norm(kv_a[..., :KV_LORA], p['g_kv_a'])
    krope = kv_a[..., KV_LORA:]
    q_gqa = rms_norm(gq.reshape(B, T, GQA_HEADS, GQA_HEAD_DIM), p['g_q_gqa'])
    k_gqa = rms_norm(gk.reshape(B, T, GQA_KV_HEADS, GQA_HEAD_DIM), p['g_k_gqa'])
    v_gqa = gv.reshape(B, T, GQA_KV_HEADS, GQA_HEAD_DIM)
    if cache is None:
        new_state = (ckv, krope, k_gqa, v_gqa)
        ckv_all, krope_all, k_all, v_all = ckv, krope, k_gqa, v_gqa
    else:
        c_ckv, c_krope, c_k, c_v = cache
        q_mla = jnp.concatenate([q_mla[..., :MLA_NOPE], axial_rope(q_mla[..., MLA_NOPE:])], axis=-1)
        krope = axial_rope(krope[:, :, None, :])[:, :, 0, :]
        q_gqa = axial_rope(q_gqa)
        k_gqa = axial_rope(k_gqa)
        ckv_all = jnp.concatenate([ckv, c_ckv], axis=1)
        krope_all = jnp.concatenate([krope, c_krope], axis=1)
        k_all = jnp.concatenate([k_gqa, c_k], axis=1)
        v_all = jnp.concatenate([v_gqa, c_v], axis=1)
        new_state = None
    S = ckv_all.shape[1]
    kv = (ckv_all @ p['w_kv_b']).reshape(B, S, MLA_HEADS, MLA_NOPE + MLA_V)
    k_mla = jnp.concatenate([kv[..., :MLA_NOPE],
                             jnp.broadcast_to(krope_all[:, :, None, :], (B, S, MLA_HEADS, MLA_ROPE))], axis=-1)
    a_out = block_attention(q_mla, k_mla, kv[..., MLA_NOPE:])
    b_out = block_attention(q_gqa, k_all, v_all)
    c_out = pool_mixer(pool_in, p['w_pool'], p['pool_scale'])
    gates = jax.nn.sigmoid(gate_logits)
    ga, gb, gc = jnp.split(gates, N_BRANCH, axis=-1)
    merged = ga * (a_out @ p['w_br_a']) + gb * (b_out @ p['w_br_b']) + gc * (c_out @ p['w_br_c'])
    x = x + g1 * (merged @ p['w_out'])
    h2 = rms_norm(x, p['g_norm_ffn']) * (1 + sc2) + sh2
    x = x + g2 * conv_ffn(h2, p['w_up'], p['conv_w'], p['conv_b'], p['w_down'])
    return x, new_state


def setup_inputs(seed: int = 0) -> dict:
    key = jax.random.key(seed)
    ks = jax.random.split(key, 40)
    f32 = jnp.float32

    def nrm(k, shape, scale=1.0):
        return jax.random.normal(k, shape, f32) * scale

    def gain(k, shape):
        return 1.0 + 0.1 * jax.random.normal(k, shape, f32)

    L = DEPTH
    return {
        'x_prompt': nrm(ks[0], (BATCH, SEQ, D_MODEL)),
        'x_sample': nrm(ks[1], (DEC_BATCH, DEC_SEQ, D_MODEL)),
        'c': nrm(ks[2], (DEC_BATCH, D_MODEL)),
        'cache_mla_ckv': nrm(ks[3], (DEC_BATCH, L, PAST_LEN, KV_LORA)),
        'cache_mla_krope': nrm(ks[4], (DEC_BATCH, L, PAST_LEN, MLA_ROPE)),
        'cache_gqa_k': nrm(ks[5], (DEC_BATCH, L, PAST_LEN, GQA_KV_HEADS, GQA_HEAD_DIM)),
        'cache_gqa_v': nrm(ks[6], (DEC_BATCH, L, PAST_LEN, GQA_KV_HEADS, GQA_HEAD_DIM)),
        'c_ctx': nrm(ks[7], (D_MODEL,)),
        'w_ada': nrm(ks[8], (L, D_MODEL, 6 * D_MODEL), 0.5 * D_MODEL ** -0.5),
        'b_ada': nrm(ks[9], (L, 6 * D_MODEL), 0.01),
        'g_norm_mix': gain(ks[10], (L, D_MODEL)),
        'w_in': nrm(ks[11], (L, D_MODEL, IN_COLS), D_MODEL ** -0.5),
        'g_q_a': gain(ks[12], (L, Q_LORA)),
        'w_q_b': nrm(ks[13], (L, Q_LORA, MLA_HEADS * (MLA_NOPE + MLA_ROPE)), Q_LORA ** -0.5),
        'g_kv_a': gain(ks[14], (L, KV_LORA)),
        'w_kv_b': nrm(ks[15], (L, KV_LORA, MLA_HEADS * (MLA_NOPE + MLA_V)), KV_LORA ** -0.5),
        'g_q_gqa': gain(ks[16], (L, GQA_HEAD_DIM)),
        'g_k_gqa': gain(ks[17], (L, GQA_HEAD_DIM)),
        'w_pool': nrm(ks[18], (L, len(POOL_WINDOWS), POOL_GROUP, POOL_GROUP), POOL_GROUP ** -0.5),
        'pool_scale': gain(ks[19], (L, POOL_WIDTH)),
        'w_br_a': nrm(ks[20], (L, MLA_OUT, D_MODEL), MLA_OUT ** -0.5),
        'w_br_b': nrm(ks[21], (L, GQA_OUT, D_MODEL), GQA_OUT ** -0.5),
        'w_br_c': nrm(ks[22], (L, POOL_WIDTH, D_MODEL), POOL_WIDTH ** -0.5),
        'w_out': nrm(ks[23], (L, D_MODEL, D_MODEL), D_MODEL ** -0.5),
        'g_norm_ffn': gain(ks[24], (L, D_MODEL)),
        'w_up': nrm(ks[25], (L, D_MODEL, 2 * D_FF), D_MODEL ** -0.5),
        'conv_w': nrm(ks[26], (L, 3, D_FF), 3 ** -0.5),
        'conv_b': nrm(ks[27], (L, D_FF), 0.01),
        'w_down': nrm(ks[28], (L, D_FF, D_MODEL), D_FF ** -0.5),
        'g_final': gain(ks[29], (D_MODEL,)),
    }


def reference(x_prompt, x_sample, c, cache_mla_ckv, cache_mla_krope, cache_gqa_k, cache_gqa_v,
              c_ctx, w_ada, b_ada, g_norm_mix, w_in, g_q_a, w_q_b, g_kv_a, w_kv_b, g_q_gqa, g_k_gqa,
              w_pool, pool_scale, w_br_a, w_br_b, w_br_c, w_out, g_norm_ffn, w_up, conv_w, conv_b,
              w_down, g_final):
    xp = x_prompt
    xs = x_sample
    st_ckv, st_krope, st_k, st_v = [], [], [], []
    for l in range(DEPTH):
        p = {
            'g_norm_mix': g_norm_mix[l], 'w_in': w_in[l], 'g_q_a': g_q_a[l], 'w_q_b': w_q_b[l],
            'g_kv_a': g_kv_a[l], 'w_kv_b': w_kv_b[l], 'g_q_gqa': g_q_gqa[l], 'g_k_gqa': g_k_gqa[l],
            'w_pool': w_pool[l], 'pool_scale': pool_scale[l], 'w_br_a': w_br_a[l], 'w_br_b': w_br_b[l],
            'w_br_c': w_br_c[l], 'w_out': w_out[l], 'g_norm_ffn': g_norm_ffn[l], 'w_up': w_up[l],
            'conv_w': conv_w[l], 'conv_b': conv_b[l], 'w_down': w_down[l],
        }
        mod_ctx = (jax.nn.silu(c_ctx) @ w_ada[l] + b_ada[l])[None, None, :]
        xp, st = trunk_layer(xp, mod_ctx, p)
        st_ckv.append(st[0])
        st_krope.append(st[1])
        st_k.append(st[2])
        st_v.append(st[3])
        mod_lat = (jax.nn.silu(c) @ w_ada[l] + b_ada[l])[:, None, :]
        xs, _ = trunk_layer(xs, mod_lat, p,
                            cache=(cache_mla_ckv[:, l], cache_mla_krope[:, l], cache_gqa_k[:, l], cache_gqa_v[:, l]))
    y_prompt = rms_norm(xp, g_final)
    y_sample = rms_norm(xs, g_final)
    new_mla_ckv = jnp.stack(st_ckv, axis=1)
    new_mla_krope = jnp.stack(st_krope, axis=1)
    new_gqa_k = jnp.stack(st_k, axis=1)
    new_gqa_v = jnp.stack(st_v, axis=1)
    return (y_prompt, y_sample, new_mla_ckv, new_mla_krope, new_gqa_k, new_gqa_v)
```

```python
import functools
import math

import numpy as np
import jax
import jax.numpy as jnp
from jax import lax
from jax.experimental import pallas as pl
from jax.experimental.pallas import tpu as pltpu

D_MODEL = 1024
GRID_W = 64
RMS_EPS = 1e-6
ROPE_THETA = 10000.0
MLA_HEADS = 8
MLA_NOPE = 64
MLA_ROPE = 32
MLA_V = 64
Q_LORA = 256
KV_LORA = 128
GQA_HEADS = 8
GQA_KV_HEADS = 2
GQA_HEAD_DIM = 64
POOL_WINDOWS = (2, 4, 8, 16)
POOL_GROUP = 128
POOL_WIDTH = POOL_GROUP * len(POOL_WINDOWS)
D_FF = 2816
N_BRANCH = 3

LANE = 128
HALO = 8
FF_CHUNK = 256
N_PAIRS = MLA_HEADS // 2
VMEM_LIMIT = 56 * 1024 * 1024

BF16 = jnp.bfloat16
F32 = jnp.float32

_OFF_QA = 0
_OFF_CKV = _OFF_QA + Q_LORA
_OFF_KR = _OFF_CKV + KV_LORA
_OFF_GQ = _OFF_KR + MLA_ROPE
_OFF_GK = _OFF_GQ + GQA_HEADS * GQA_HEAD_DIM
_OFF_GV = _OFF_GK + GQA_KV_HEADS * GQA_HEAD_DIM
_OFF_POOL = _OFF_GV + GQA_KV_HEADS * GQA_HEAD_DIM
_OFF_GATE = _OFF_POOL + POOL_WIDTH

_W1_QA = 0
_W1_CKV = 256
_W1_GQ = 384
_W1_GK = 896
_W1_GV = 1024
_W1_POOL = 1152
_W1_KR = 1664
_W1_CTX_COLS = 1792
_W1_GQ_SW = 1792
_W1_GK_SW = 2304
_W1_KR_SW = 2432
_W1_LAT_COLS = 2560


def _const_spec(shape):
    nd = len(shape)
    return pl.BlockSpec(shape, lambda *_: (0,) * nd, pipeline_mode=pl.Buffered(1))


def _rms(x, g):
    ms = jnp.mean(x * x, axis=-1, keepdims=True)
    return x * lax.rsqrt(ms + RMS_EPS) * g


def _dot(a, b):
    return jnp.dot(a, b, preferred_element_type=F32)


def _swap_index(d):
    h = d // 2
    q = h // 2
    idx = []
    for base in (0, h):
        idx += list(range(base + q, base + h)) + list(range(base, base + q))
    return np.asarray(idx, np.int32)


def _rope_head_tables(n_tokens, d):
    dim_axis = d // 2
    t = np.arange(n_tokens)
    row = (t // GRID_W).astype(np.float64)
    col = (t % GRID_W).astype(np.float64)
    freqs = ROPE_THETA ** (-np.arange(0, dim_axis, 2, dtype=np.float64) / dim_axis)
    ar = row[:, None] * freqs[None, :]
    ac = col[:, None] * freqs[None, :]
    cos = np.concatenate([np.cos(ar), np.cos(ar), np.cos(ac), np.cos(ac)], axis=1)
    sin = np.concatenate([-np.sin(ar), np.sin(ar), -np.sin(ac), np.sin(ac)], axis=1)
    return cos, sin


def _rope_tables(n_tokens):
    cg, sg = _rope_head_tables(n_tokens, GQA_HEAD_DIM)
    cg = np.concatenate([cg, cg], axis=1)
    sg = np.concatenate([sg, sg], axis=1)
    cm32, sm32 = _rope_head_tables(n_tokens, MLA_ROPE)
    pad = LANE - MLA_NOPE - MLA_ROPE
    cm = np.concatenate([np.ones((n_tokens, MLA_NOPE)), cm32, np.zeros((n_tokens, pad))], axis=1)
    sm = np.concatenate([np.zeros((n_tokens, MLA_NOPE)), sm32, np.zeros((n_tokens, pad))], axis=1)
    return tuple(jnp.asarray(a, F32) for a in (cg, sg, cm, sm))


def _ada_body(c_ref, w_ref, b_ref, o_ref):
    c = c_ref[...]
    s = (c * jax.nn.sigmoid(c)).astype(BF16)
    o_ref[0] = _dot(s, w_ref[0].astype(BF16)) + b_ref[0]


def _ada_call(cin, w_ada, b_ada):
    depth, d, n = w_ada.shape
    cols = 2048
    rows = cin.shape[0]
    return pl.pallas_call(
        _ada_body,
        grid=(depth, n // cols),
        in_specs=[
            pl.BlockSpec((rows, d), lambda l, j: (0, 0)),
            pl.BlockSpec((1, d, cols), lambda l, j: (l, 0, j)),
            pl.BlockSpec((1, 1, cols), lambda l, j: (l, 0, j)),
        ],
        out_specs=pl.BlockSpec((1, rows, cols), lambda l, j: (l, 0, j)),
        out_shape=jax.ShapeDtypeStruct((depth, rows, n), F32),
        compiler_params=pltpu.CompilerParams(
            dimension_semantics=("arbitrary", "arbitrary"), vmem_limit_bytes=VMEM_LIMIT),
        name="ada_mod",
    )(cin, w_ada, b_ada.reshape(depth, 1, n))


def _two_head_rsqrt(xb, lo):
    x2 = xb * xb
    s_lo = jnp.sum(jnp.where(lo, x2, 0.0), axis=-1, keepdims=True)
    s_hi = jnp.sum(jnp.where(lo, 0.0, x2), axis=-1, keepdims=True)
    ms = jnp.where(lo, s_lo, s_hi) * (1.0 / GQA_HEAD_DIM)
    return lax.rsqrt(ms + RMS_EPS)


def _inproj_body(latent, names, *refs):
    r = dict(zip(names, refs))
    x = r["x"][...]
    tm = x.shape[0]
    mod = r["mod"][0]
    sh1 = mod[:, 0:D_MODEL]
    sc1 = mod[:, D_MODEL:2 * D_MODEL]
    h = (_rms(x, r["gmix"][...]) * (1.0 + sc1) + sh1).astype(BF16)
    z = _dot(h, r["w1"][...])

    lane = lax.broadcasted_iota(jnp.int32, (tm, LANE), 1)
    lo = lane < GQA_HEAD_DIM

    if latent:
        cg = r["cg"][...]
        sg = r["sg"][...]
        cm = r["cm"][...]
        sm = r["sm"][...]

    qn = _rms(z[:, _W1_QA:_W1_QA + Q_LORA], r["gqa"][...]).astype(BF16)
    qm = _dot(qn, r["wqb"][...])
    if latent:
        qms = _dot(qn, r["wqbsw"][...])
    for hh in range(MLA_HEADS):
        blk = qm[:, hh * LANE:(hh + 1) * LANE]
        if latent:
            blk = blk * cm + qms[:, hh * LANE:(hh + 1) * LANE] * sm
        r["q_mla"][hh] = blk.astype(BF16)

    ckv = _rms(z[:, _W1_CKV:_W1_CKV + KV_LORA], r["gkva"][...])
    kr = z[:, _W1_KR:_W1_KR + LANE]
    if latent:
        kr = kr * cm + z[:, _W1_KR_SW:_W1_KR_SW + LANE] * sm
    else:
        r["ckv_out"][...] = ckv
        r["kr_out"][...] = kr[:, 0:MLA_ROPE]
    ckv_b = ckv.astype(BF16)
    kfull = _dot(ckv_b, r["wk"][...]) + _dot(kr.astype(BF16), r["pk"][...])
    for hh in range(MLA_HEADS):
        r["k_mla"][hh] = kfull[:, hh * LANE:(hh + 1) * LANE].astype(BF16)
    vfull = _dot(ckv_b, r["wv"][...])
    for pp in range(N_PAIRS):
        r["v_mla"][pp] = vfull[:, pp * LANE:(pp + 1) * LANE].astype(BF16)

    gq2 = r["gq2"][...]
    q_scale = GQA_HEAD_DIM ** -0.5
    for j in range(N_PAIRS):
        xb = z[:, _W1_GQ + j * LANE:_W1_GQ + (j + 1) * LANE]
        rs = _two_head_rsqrt(xb, lo)
        y = xb * rs * gq2
        if latent:
            xs = z[:, _W1_GQ_SW + j * LANE:_W1_GQ_SW + (j + 1) * LANE]
            y = y * cg + (xs * rs * r["gq2sw"][...]) * sg
        y = y * q_scale
        r["q_gqa"][2 * j] = jnp.where(lo, y, 0.0).astype(BF16)
        r["q_gqa"][2 * j + 1] = jnp.where(lo, 0.0, y).astype(BF16)
    kb = z[:, _W1_GK:_W1_GK + LANE]
    rs = _two_head_rsqrt(kb, lo)
    kg = kb * rs * r["gk2"][...]
    if latent:
        ks = z[:, _W1_GK_SW:_W1_GK_SW + LANE]
        kg = kg * cg + (ks * rs * r["gk2sw"][...]) * sg
    vg = z[:, _W1_GV:_W1_GV + LANE]
    if not latent:
        r["kg_out"][...] = kg
        r["vg_out"][...] = vg
    r["k_gqa"][...] = kg.astype(BF16)
    r["v_gqa"][...] = vg.astype(BF16)

    r["pool"][...] = z[:, _W1_POOL:_W1_POOL + POOL_WIDTH]


def _inproj_call(x, mod3, mod_base, seq_len, tm, latent, wl, tabs):
    n = x.shape[0]
    nt = n // tm
    per_seq = seq_len // tm if latent else 1

    def tile(i):
        return (i, 0)

    def mod_idx(i):
        if latent:
            return (mod_base + i // per_seq, 0, 0)
        return (mod_base, 0, 0)

    names = ["x", "mod", "gmix", "w1", "gqa", "wqb", "gkva", "wk", "pk", "wv", "gq2", "gk2"]
    args = [x, mod3, wl["gmix"], wl["w1_lat"] if latent else wl["w1_ctx"], wl["gqa"], wl["wqb"],
            wl["gkva"], wl["wk"], wl["pk_lat"] if latent else wl["pk_ctx"], wl["wv"], wl["gq2"], wl["gk2"]]
    specs = [pl.BlockSpec((tm, D_MODEL), tile), pl.BlockSpec((1, 1, 6 * D_MODEL), mod_idx)]
    specs += [_const_spec(a.shape) for a in args[2:]]
    if latent:
        extra = [("wqbsw", wl["wqbsw"]), ("gq2sw", wl["gq2sw"]), ("gk2sw", wl["gk2sw"])]
        for nm, a in extra:
            names.append(nm)
            args.append(a)
            specs.append(_const_spec(a.shape))
        for nm, a in zip(("cg", "sg", "cm", "sm"), tabs):
            names.append(nm)
            args.append(a)
            specs.append(pl.BlockSpec((tm, LANE), lambda i: (i % per_seq, 0)))

    head_spec = pl.BlockSpec((MLA_HEADS, tm, LANE), lambda i: (0, i, 0))
    pair_spec = pl.BlockSpec((N_PAIRS, tm, LANE), lambda i: (0, i, 0))
    out_names = ["q_mla", "k_mla", "v_mla", "q_gqa", "k_gqa", "v_gqa", "pool"]
    out_shapes = [
        jax.ShapeDtypeStruct((MLA_HEADS, n, LANE), BF16),
        jax.ShapeDtypeStruct((MLA_HEADS, n, LANE), BF16),
        jax.ShapeDtypeStruct((N_PAIRS, n, LANE), BF16),
        jax.ShapeDtypeStruct((GQA_HEADS, n, LANE), BF16),
        jax.ShapeDtypeStruct((n, LANE), BF16),
        jax.ShapeDtypeStruct((n, LANE), BF16),
        jax.ShapeDtypeStruct((n, POOL_WIDTH), F32),
    ]
    out_specs = [head_spec, head_spec, pair_spec, head_spec,
                 pl.BlockSpec((tm, LANE), tile), pl.BlockSpec((tm, LANE), tile),
                 pl.BlockSpec((tm, POOL_WIDTH), tile)]
    if not latent:
        out_names += ["ckv_out", "kr_out", "kg_out", "vg_out"]
        out_shapes += [jax.ShapeDtypeStruct((n, KV_LORA), F32), jax.ShapeDtypeStruct((n, MLA_ROPE), F32),
                       jax.ShapeDtypeStruct((n, LANE), F32), jax.ShapeDtypeStruct((n, LANE), F32)]
        out_specs += [pl.BlockSpec((tm, KV_LORA), tile), pl.BlockSpec((tm, MLA_ROPE), tile),
                      pl.BlockSpec((tm, LANE), tile), pl.BlockSpec((tm, LANE), tile)]

    outs = pl.pallas_call(
        functools.partial(_inproj_body, latent, names + out_names),
        grid=(nt,),
        in_specs=specs,
        out_specs=out_specs,
        out_shape=out_shapes,
        compiler_params=pltpu.CompilerParams(
            dimension_semantics=("arbitrary",), vmem_limit_bytes=VMEM_LIMIT),
        name="inproj_lat" if latent else "inproj_ctx",
    )(*args)
    return dict(zip(out_names, outs))


def _cachekv_body(ckv_ref, kr_ref, wk_ref, pk_ref, wv_ref, k_ref, v_ref):
    ckv_b = ckv_ref[...]
    kfull = _dot(ckv_b, wk_ref[...]) + _dot(kr_ref[...], pk_ref[...])
    for hh in range(MLA_HEADS):
        k_ref[hh] = kfull[:, hh * LANE:(hh + 1) * LANE].astype(BF16)
    vfull = _dot(ckv_b, wv_ref[...])
    for pp in range(N_PAIRS):
        v_ref[pp] = vfull[:, pp * LANE:(pp + 1) * LANE].astype(BF16)


def _cachekv_call(ckv_b, kr_pad_b, wl):
    n = ckv_b.shape[0]
    args = [ckv_b, kr_pad_b, wl["wk"], wl["pk_lat"], wl["wv"]]
    return pl.pallas_call(
        _cachekv_body,
        grid=(1,),
        in_specs=[_const_spec(a.shape) for a in args],
        out_specs=[_const_spec((MLA_HEADS, n, LANE)), _const_spec((N_PAIRS, n, LANE))],
        out_shape=[jax.ShapeDtypeStruct((MLA_HEADS, n, LANE), BF16),
                   jax.ShapeDtypeStruct((N_PAIRS, n, LANE), BF16)],
        compiler_params=pltpu.CompilerParams(
            dimension_semantics=("arbitrary",), vmem_limit_bytes=VMEM_LIMIT),
        name="cache_kv",
    )(*args)


def _attn_body(pp, shared_kv, has_cache, exp_scale, *refs):
    if has_cache:
        q_ref, k_ref, v_ref, kc_ref, vc_ref, o_ref = refs
    else:
        q_ref, k_ref, v_ref, o_ref = refs
    tq = q_ref.shape[1]
    lane = lax.broadcasted_iota(jnp.int32, (tq, LANE), 1)
    lo = lane < MLA_V
    nt_dims = (((1,), (1,)), ((), ()))
    for pr in range(pp):
        outs = []
        for e in range(2):
            q = q_ref[2 * pr + e]
            k = k_ref[0] if shared_kv else k_ref[2 * pr + e]
            v = v_ref[0] if shared_kv else v_ref[pr]
            s = lax.dot_general(q, k, nt_dims, preferred_element_type=F32)
            m = jnp.max(s, axis=-1, keepdims=True)
            if has_cache:
                kc = kc_ref[0] if shared_kv else kc_ref[2 * pr + e]
                vc = vc_ref[0] if shared_kv else vc_ref[pr]
                s2 = lax.dot_general(q, kc, nt_dims, preferred_element_type=F32)
                m = jnp.maximum(m, jnp.max(s2, axis=-1, keepdims=True))
            p = jnp.exp2((s - m) * exp_scale)
            den = jnp.sum(p, axis=-1, keepdims=True)
            o = _dot(p.astype(BF16), v)
            if has_cache:
                p2 = jnp.exp2((s2 - m) * exp_scale)
                den = den + jnp.sum(p2, axis=-1, keepdims=True)
                o = o + _dot(p2.astype(BF16), vc)
            outs.append(o / den)
        o_ref[:, pr * LANE:(pr + 1) * LANE] = jnp.where(lo, outs[0], outs[1]).astype(BF16)


def _attn_call(q, k, v, kc, vc, batch, seq_len, tq, pp, scale, shared_kv, name):
    n = q.shape[1]
    nq = seq_len // tq
    has_cache = kc is not None
    grid = (batch, N_PAIRS // pp, nq)
    exp_scale = scale * math.log2(math.e)

    def kv_spec(heads, rows):
        if shared_kv:
            return pl.BlockSpec((1, rows, LANE), lambda b, p, i: (0, b, 0))
        return pl.BlockSpec((heads, rows, LANE), lambda b, p, i: (p, b, 0))

    in_specs = [pl.BlockSpec((2 * pp, tq, LANE), lambda b, p, i: (p, b * nq + i, 0)),
                kv_spec(2 * pp, seq_len), kv_spec(pp, seq_len)]
    args = [q, k, v]
    if has_cache:
        past = kc.shape[1] // batch
        in_specs += [kv_spec(2 * pp, past), kv_spec(pp, past)]
        args += [kc, vc]
    return pl.pallas_call(
        functools.partial(_attn_body, pp, shared_kv, has_cache, exp_scale),
        grid=grid,
        in_specs=in_specs,
        out_specs=pl.BlockSpec((tq, pp * LANE), lambda b, p, i: (b * nq + i, p)),
        out_shape=jax.ShapeDtypeStruct((n, N_PAIRS * LANE), BF16),
        compiler_params=pltpu.CompilerParams(
            dimension_semantics=("arbitrary", "arbitrary", "arbitrary"), vmem_limit_bytes=VMEM_LIMIT),
        name=name,
    )(*args)


def _seq_pos(tm, seq_len):
    i = pl.program_id(0)
    row = lax.broadcasted_iota(jnp.int32, (tm, 1), 0) + i * tm
    return jnp.bitwise_and(row, seq_len - 1)


def _merge_body(seq_len, x_ref, mod_ref, gmix_ref, wg_ref, a_ref, b_ref, pc_ref, pl_ref, pr_ref,
                wpool_ref, pscale_ref, wa_ref, wb_ref, wc_ref, wo_ref, o_ref):
    x = x_ref[...]
    tm = x.shape[0]
    mod = mod_ref[0]
    sh1 = mod[:, 0:D_MODEL]
    sc1 = mod[:, D_MODEL:2 * D_MODEL]
    g1 = mod[:, 2 * D_MODEL:3 * D_MODEL]
    h = (_rms(x, gmix_ref[...]) * (1.0 + sc1) + sh1).astype(BF16)
    gates = jax.nn.sigmoid(_dot(h, wg_ref[...]))

    pos = _seq_pos(tm, seq_len)
    ext = jnp.concatenate([pl_ref[...], pc_ref[...], pr_ref[...]], axis=0)
    pscale = pscale_ref[...]
    c_parts = []
    for g, w in enumerate(POOL_WINDOWS):
        eg = ext[:, g * POOL_GROUP:(g + 1) * POOL_GROUP]
        wsum = jnp.zeros((tm, POOL_GROUP), F32)
        cnt = jnp.zeros((tm, 1), F32)
        for dlt in range(-(w // 2), w - w // 2):
            valid = jnp.logical_and(pos + dlt >= 0, pos + dlt < seq_len)
            wsum = wsum + jnp.where(valid, eg[HALO + dlt:HALO + dlt + tm], 0.0)
            cnt = cnt + valid.astype(F32)
        pooled = (wsum / cnt - eg[HALO:HALO + tm]).astype(BF16)
        c_parts.append(_dot(pooled, wpool_ref[g]) * pscale[:, g * POOL_GROUP:(g + 1) * POOL_GROUP])
    c_out = jnp.concatenate(c_parts, axis=-1).astype(BF16)

    merged = (gates[:, 0:D_MODEL] * _dot(a_ref[...], wa_ref[...])
              + gates[:, D_MODEL:2 * D_MODEL] * _dot(b_ref[...], wb_ref[...])
              + gates[:, 2 * D_MODEL:3 * D_MODEL] * _dot(c_out, wc_ref[...]))
    o_ref[...] = x + g1 * _dot(merged.astype(BF16), wo_ref[...])


def _halo_specs(tm, width, n):
    blocks = tm // HALO
    last = n // HALO - 1
    left = pl.BlockSpec((HALO, width), lambda i: (jnp.maximum(i * blocks - 1, 0), 0))
    right = pl.BlockSpec((HALO, width), lambda i: (jnp.minimum((i + 1) * blocks, last), 0))
    return left, right


def _mod_spec(mod_base, per_seq, latent):
    if latent:
        return pl.BlockSpec((1, 1, 6 * D_MODEL), lambda i: (mod_base + i // per_seq, 0, 0))
    return pl.BlockSpec((1, 1, 6 * D_MODEL), lambda i: (mod_base, 0, 0))


def _merge_call(x, mod3, mod_base, seq_len, tm, latent, a_out, b_out, pool_in, wl):
    n = x.shape[0]
    per_seq = seq_len // tm if latent else 1

    def tile(i):
        return (i, 0)

    left, right = _halo_specs(tm, POOL_WIDTH, n)
    consts = [wl["wpool"], wl["pscale"], wl["wa"], wl["wb"], wl["wc"], wl["wo"]]
    in_specs = [pl.BlockSpec((tm, D_MODEL), tile), _mod_spec(mod_base, per_seq, latent),
                _const_spec(wl["gmix"].shape), _const_spec(wl["wg"].shape),
                pl.BlockSpec((tm, N_PAIRS * LANE), tile), pl.BlockSpec((tm, N_PAIRS * LANE), tile),
                pl.BlockSpec((tm, POOL_WIDTH), tile), left, right]
    in_specs += [_const_spec(a.shape) for a in consts]
    return pl.pallas_call(
        functools.partial(_merge_body, seq_len),
        grid=(n // tm,),
        in_specs=in_specs,
        out_specs=pl.BlockSpec((tm, D_MODEL), tile),
        out_shape=jax.ShapeDtypeStruct((n, D_MODEL), F32),
        compiler_params=pltpu.CompilerParams(
            dimension_semantics=("arbitrary",), vmem_limit_bytes=VMEM_LIMIT),
        name="merge_lat" if latent else "merge_ctx",
    )(x, mod3, wl["gmix"], wl["wg"], a_out, b_out, pool_in, pool_in, pool_in, *consts)


def _ffn_body(seq_len, final, x_ref, xl_ref, xr_ref, mod_ref, gffn_ref, wug_ref, wuv_ref, cw_ref, cb_ref,
              wd_ref, gfin_ref, o_ref):
    x = x_ref[...]
    tm = x.shape[0]
    mod = mod_ref[0]
    sh2 = mod[:, 3 * D_MODEL:4 * D_MODEL]
    sc2 = mod[:, 4 * D_MODEL:5 * D_MODEL]
    g2 = mod[:, 5 * D_MODEL:6 * D_MODEL]
    xe = jnp.concatenate([xl_ref[...], x, xr_ref[...]], axis=0)
    h2e = _rms(xe, gffn_ref[...]) * (1.0 + sc2) + sh2
    h2 = h2e[HALO:HALO + tm].astype(BF16)
    h2e = h2e.astype(BF16)

    pos = _seq_pos(tm, seq_len)
    has_prev = pos >= 1
    has_next = pos <= seq_len - 2
    cw = cw_ref[...]
    cb = cb_ref[...]
    acc = jnp.zeros((tm, D_MODEL), F32)
    for j in range(D_FF // FF_CHUNK):
        cols = slice(j * FF_CHUNK, (j + 1) * FF_CHUNK)
        ge = _dot(h2e, wug_ref[:, cols])
        g = (jnp.where(has_prev, ge[HALO - 1:HALO - 1 + tm], 0.0) * cw[0:1, cols]
             + ge[HALO:HALO + tm] * cw[1:2, cols]
             + jnp.where(has_next, ge[HALO + 1:HALO + 1 + tm], 0.0) * cw[2:3, cols]
             + cb[:, cols])
        val = _dot(h2, wuv_ref[:, cols])
        act = (g * jax.nn.sigmoid(g) * val).astype(BF16)
        acc = acc + _dot(act, wd_ref[cols, :])
    y = x + g2 * acc
    if final:
        y = _rms(y, gfin_ref[...])
    o_ref[...] = y


def _ffn_call(x, mod3, mod_base, seq_len, tm, latent, final, wl, g_final):
    n = x.shape[0]
    per_seq = seq_len // tm if latent else 1

    def tile(i):
        return (i, 0)

    left, right = _halo_specs(tm, D_MODEL, n)
    consts = [wl["gffn"], wl["wug"], wl["wuv"], wl["cw"], wl["cb"], wl["wd"], g_final]
    in_specs = [pl.BlockSpec((tm, D_MODEL), tile), left, right, _mod_spec(mod_base, per_seq, latent)]
    in_specs += [_const_spec(a.shape) for a in consts]
    return pl.pallas_call(
        functools.partial(_ffn_body, seq_len, final),
        grid=(n // tm,),
        in_specs=in_specs,
        out_specs=pl.BlockSpec((tm, D_MODEL), tile),
        out_shape=jax.ShapeDtypeStruct((n, D_MODEL), F32),
        compiler_params=pltpu.CompilerParams(
            dimension_semantics=("arbitrary",), vmem_limit_bytes=VMEM_LIMIT),
        name="ffn_lat" if latent else "ffn_ctx",
    )(x, x, x, mod3, *consts)


def _prep_layer(l, g_norm_mix, w_in, g_q_a, w_q_b, g_kv_a, w_kv_b, g_q_gqa, g_k_gqa, w_pool, pool_scale,
                w_br_a, w_br_b, w_br_c, w_out, g_norm_ffn, w_up, conv_w, conv_b, w_down):
    wi = w_in[l]
    sw64 = _swap_index(GQA_HEAD_DIM)
    sw32 = _swap_index(MLA_ROPE)
    hd = GQA_HEAD_DIM

    pair_heads = []
    for j in range(N_PAIRS):
        pair_heads += [j, j + GQA_HEADS // GQA_KV_HEADS]
    gq_cols = np.concatenate([_OFF_GQ + hh * hd + np.arange(hd) for hh in pair_heads])
    gq_sw_cols = np.concatenate([_OFF_GQ + hh * hd + sw64 for hh in pair_heads])
    gk_cols = _OFF_GK + np.arange(GQA_KV_HEADS * hd)
    gk_sw_cols = np.concatenate([_OFF_GK + hh * hd + sw64 for hh in range(GQA_KV_HEADS)])

    def kr_block(cols, lane_off):
        blk = jnp.zeros((D_MODEL, LANE), F32)
        return blk.at[:, lane_off:lane_off + MLA_ROPE].set(wi[:, cols])

    kr_cols = _OFF_KR + np.arange(MLA_ROPE)
    base = [wi[:, _OFF_QA:_OFF_QA + Q_LORA], wi[:, _OFF_CKV:_OFF_CKV + KV_LORA], wi[:, gq_cols],
            wi[:, gk_cols], wi[:, _OFF_GV:_OFF_GV + GQA_KV_HEADS * hd], wi[:, _OFF_POOL:_OFF_POOL + POOL_WIDTH]]
    w1_ctx = jnp.concatenate(base + [kr_block(kr_cols, 0)], axis=1).astype(BF16)
    w1_lat = jnp.concatenate(
        base + [kr_block(kr_cols, MLA_NOPE), wi[:, gq_sw_cols], wi[:, gk_sw_cols],
                kr_block(_OFF_KR + sw32, MLA_NOPE)], axis=1).astype(BF16)

    qb = w_q_b[l].reshape(Q_LORA, MLA_HEADS, MLA_NOPE + MLA_ROPE)
    pad = LANE - MLA_NOPE - MLA_ROPE
    zpad = jnp.zeros((Q_LORA, MLA_HEADS, pad), F32)
    wqb = jnp.concatenate([qb, zpad], axis=-1).reshape(Q_LORA, MLA_HEADS * LANE).astype(BF16)
    qb_sw = qb[:, :, MLA_NOPE + sw32]
    wqbsw = jnp.concatenate([jnp.zeros((Q_LORA, MLA_HEADS, MLA_NOPE), F32), qb_sw, zpad],
                            axis=-1).reshape(Q_LORA, MLA_HEADS * LANE).astype(BF16)

    kvb = w_kv_b[l].reshape(KV_LORA, MLA_HEADS, MLA_NOPE + MLA_V)
    wk = jnp.concatenate([kvb[:, :, :MLA_NOPE], jnp.zeros((KV_LORA, MLA_HEADS, LANE - MLA_NOPE), F32)],
                         axis=-1).reshape(KV_LORA, MLA_HEADS * LANE).astype(BF16)
    wv = kvb[:, :, MLA_NOPE:].reshape(KV_LORA, MLA_HEADS * MLA_V).astype(BF16)

    def place(lane_off):
        pk = np.zeros((LANE, MLA_HEADS * LANE), np.float32)
        for hh in range(MLA_HEADS):
            for t in range(MLA_ROPE):
                pk[lane_off + t, hh * LANE + MLA_NOPE + t] = 1.0
        return jnp.asarray(pk, BF16)

    gq = g_q_gqa[l]
    gk = g_k_gqa[l]
    row = lambda a: a.reshape(1, -1)
    wb_rows = np.concatenate([hh * hd + np.arange(hd) for hh in pair_heads])
    wu = w_up[l]
    return dict(
        gmix=row(g_norm_mix[l]), w1_ctx=w1_ctx, w1_lat=w1_lat, gqa=row(g_q_a[l]), wqb=wqb, wqbsw=wqbsw,
        gkva=row(g_kv_a[l]), wk=wk, pk_ctx=place(0), pk_lat=place(MLA_NOPE), wv=wv,
        gq2=row(jnp.concatenate([gq, gq])), gq2sw=row(jnp.concatenate([gq[sw64], gq[sw64]])),
        gk2=row(jnp.concatenate([gk, gk])), gk2sw=row(jnp.concatenate([gk[sw64], gk[sw64]])),
        wg=wi[:, _OFF_GATE:].astype(BF16), wpool=w_pool[l].astype(BF16), pscale=row(pool_scale[l]),
        wa=w_br_a[l].astype(BF16), wb=w_br_b[l][wb_rows].astype(BF16), wc=w_br_c[l].astype(BF16),
        wo=w_out[l].astype(BF16), gffn=row(g_norm_ffn[l]), wug=wu[:, :D_FF].astype(BF16),
        wuv=wu[:, D_FF:].astype(BF16), cw=conv_w[l], cb=row(conv_b[l]), wd=w_down[l].astype(BF16),
    )


def _layer(x, mod3, mod_base, batch, seq_len, tm, tq, pp, latent, final, wl, g_final, tabs, cache):
    pj = _inproj_call(x, mod3, mod_base, seq_len, tm, latent, wl, tabs)
    kc_m = vc_m = kc_g = vc_g = None
    if cache is not None:
        ckv_b, kr_pad_b, kc_g, vc_g = cache
        kc_m, vc_m = _cachekv_call(ckv_b, kr_pad_b, wl)
    mla_scale = (MLA_NOPE + MLA_ROPE) ** -0.5
    a_out = _attn_call(pj["q_mla"], pj["k_mla"], pj["v_mla"], kc_m, vc_m, batch, seq_len, tq, pp,
                       mla_scale, False, "attn_mla_lat" if latent else "attn_mla_ctx")
    n = x.shape[0]
    b_out = _attn_call(pj["q_gqa"], pj["k_gqa"].reshape(1, n, LANE), pj["v_gqa"].reshape(1, n, LANE),
                       kc_g, vc_g, batch, seq_len, tq, pp, 1.0, True,
                       "attn_gqa_lat" if latent else "attn_gqa_ctx")
    x1 = _merge_call(x, mod3, mod_base, seq_len, tm, latent, a_out, b_out, pj["pool"], wl)
    x2 = _ffn_call(x1, mod3, mod_base, seq_len, tm, latent, final, wl, g_final)
    return x2, pj


def kernel(x_prompt, x_sample, c, cache_mla_ckv, cache_mla_krope, cache_gqa_k, cache_gqa_v, c_ctx, w_ada, b_ada, g_norm_mix, w_in, g_q_a, w_q_b, g_kv_a, w_kv_b, g_q_gqa, g_k_gqa, w_pool, pool_scale, w_br_a, w_br_b, w_br_c, w_out, g_norm_ffn, w_up, conv_w, conv_b, w_down, g_final):
    depth = w_in.shape[0]
    bc, tc, _ = x_prompt.shape
    bl, tl, _ = x_sample.shape
    past = cache_mla_ckv.shape[2]
    mod_rows = 8
    assert 1 + bl <= mod_rows and tc & (tc - 1) == 0 and tl & (tl - 1) == 0

    cin = jnp.concatenate([c_ctx[None, :], c, jnp.zeros((mod_rows - 1 - bl, D_MODEL), F32)], axis=0)
    mods = _ada_call(cin, w_ada, b_ada)
    tabs = _rope_tables(tl)
    g_fin = g_final.reshape(1, D_MODEL)

    xc = x_prompt.reshape(bc * tc, D_MODEL)
    xl = x_sample.reshape(bl * tl, D_MODEL)
    st_ckv, st_kr, st_k, st_v = [], [], [], []
    for l in range(depth):
        wl = _prep_layer(l, g_norm_mix, w_in, g_q_a, w_q_b, g_kv_a, w_kv_b, g_q_gqa, g_k_gqa, w_pool,
                         pool_scale, w_br_a, w_br_b, w_br_c, w_out, g_norm_ffn, w_up, conv_w, conv_b, w_down)
        mod3 = mods[l].reshape(mod_rows, 1, 6 * D_MODEL)
        final = l == depth - 1
        xc, pj = _layer(xc, mod3, 0, bc, tc, 2 * tc, tc, N_PAIRS, False, final, wl, g_fin, None, None)
        st_ckv.append(pj["ckv_out"].reshape(bc, tc, KV_LORA))
        st_kr.append(pj["kr_out"].reshape(bc, tc, MLA_ROPE))
        st_k.append(pj["kg_out"].reshape(bc, tc, GQA_KV_HEADS, GQA_HEAD_DIM))
        st_v.append(pj["vg_out"].reshape(bc, tc, GQA_KV_HEADS, GQA_HEAD_DIM))

        kr_pad = jnp.zeros((bl * past, LANE), F32).at[:, MLA_NOPE:MLA_NOPE + MLA_ROPE].set(
            cache_mla_krope[:, l].reshape(bl * past, MLA_ROPE)).astype(BF16)
        cache = (cache_mla_ckv[:, l].reshape(bl * past, KV_LORA).astype(BF16), kr_pad,
                 cache_gqa_k[:, l].reshape(1, bl * past, LANE).astype(BF16),
                 cache_gqa_v[:, l].reshape(1, bl * past, LANE).astype(BF16))
        xl, _ = _layer(xl, mod3, 1, bl, tl, 512, 256, 1, True, final, wl, g_fin, tabs, cache)

    return (xc.reshape(bc, tc, D_MODEL), xl.reshape(bl, tl, D_MODEL),
            jnp.stack(st_ckv, axis=1), jnp.stack(st_kr, axis=1),
            jnp.stack(st_k, axis=1), jnp.stack(st_v, axis=1))
```

```python
import functools
import math

import numpy as np
import jax
import jax.numpy as jnp
from jax import lax
from jax.experimental import pallas as pl
from jax.experimental.pallas import tpu as pltpu

D_MODEL = 1024
GRID_W = 64
RMS_EPS = 1e-6
ROPE_THETA = 10000.0
MLA_HEADS = 8
MLA_NOPE = 64
MLA_ROPE = 32
MLA_V = 64
Q_LORA = 256
KV_LORA = 128
GQA_HEADS = 8
GQA_KV_HEADS = 2
GQA_HEAD_DIM = 64
POOL_WINDOWS = (2, 4, 8, 16)
POOL_GROUP = 128
POOL_WIDTH = POOL_GROUP * len(POOL_WINDOWS)
D_FF = 2816
N_BRANCH = 3

LANE = 128
HALO = 8
FF_CHUNK = 256
KEY_TILE = 256
N_PAIRS = MLA_HEADS // 2
VMEM_LIMIT = 56 * 1024 * 1024

BF16 = jnp.bfloat16
F32 = jnp.float32

_OFF_QA = 0
_OFF_CKV = _OFF_QA + Q_LORA
_OFF_KR = _OFF_CKV + KV_LORA
_OFF_GQ = _OFF_KR + MLA_ROPE
_OFF_GK = _OFF_GQ + GQA_HEADS * GQA_HEAD_DIM
_OFF_GV = _OFF_GK + GQA_KV_HEADS * GQA_HEAD_DIM
_OFF_POOL = _OFF_GV + GQA_KV_HEADS * GQA_HEAD_DIM
_OFF_GATE = _OFF_POOL + POOL_WIDTH

_W1_QA = 0
_W1_CKV = 256
_W1_GQ = 384
_W1_GK = 896
_W1_GV = 1024
_W1_POOL = 1152
_W1_KR = 1664
_W1_CTX_COLS = 1792
_W1_GQ_SW = 1792
_W1_GK_SW = 2304
_W1_KR_SW = 2432
_W1_LAT_COLS = 2560


def _const_spec(shape):
    nd = len(shape)
    return pl.BlockSpec(shape, lambda *_: (0,) * nd, pipeline_mode=pl.Buffered(1))


def _rms(x, g):
    ms = jnp.mean(x * x, axis=-1, keepdims=True)
    return x * lax.rsqrt(ms + RMS_EPS) * g


def _dot(a, b):
    return jnp.dot(a, b, preferred_element_type=F32)


def _swap_index(d):
    h = d // 2
    q = h // 2
    idx = []
    for base in (0, h):
        idx += list(range(base + q, base + h)) + list(range(base, base + q))
    return np.asarray(idx, np.int32)


def _rope_head_tables(n_tokens, d):
    dim_axis = d // 2
    t = np.arange(n_tokens)
    row = (t // GRID_W).astype(np.float64)
    col = (t % GRID_W).astype(np.float64)
    freqs = ROPE_THETA ** (-np.arange(0, dim_axis, 2, dtype=np.float64) / dim_axis)
    ar = row[:, None] * freqs[None, :]
    ac = col[:, None] * freqs[None, :]
    cos = np.concatenate([np.cos(ar), np.cos(ar), np.cos(ac), np.cos(ac)], axis=1)
    sin = np.concatenate([-np.sin(ar), np.sin(ar), -np.sin(ac), np.sin(ac)], axis=1)
    return cos, sin


def _rope_tables(n_tokens):
    cg, sg = _rope_head_tables(n_tokens, GQA_HEAD_DIM)
    cg = np.concatenate([cg, cg], axis=1)
    sg = np.concatenate([sg, sg], axis=1)
    cm32, sm32 = _rope_head_tables(n_tokens, MLA_ROPE)
    pad = LANE - MLA_NOPE - MLA_ROPE
    cm = np.concatenate([np.ones((n_tokens, MLA_NOPE)), cm32, np.zeros((n_tokens, pad))], axis=1)
    sm = np.concatenate([np.zeros((n_tokens, MLA_NOPE)), sm32, np.zeros((n_tokens, pad))], axis=1)
    return tuple(jnp.asarray(a, F32) for a in (cg, sg, cm, sm))


def _ada_body(c_ref, w_ref, b_ref, o_ref):
    c = c_ref[...]
    s = (c * jax.nn.sigmoid(c)).astype(BF16)
    o_ref[0] = _dot(s, w_ref[0].astype(BF16)) + b_ref[0]


def _ada_call(cin, w_ada, b_ada):
    depth, d, n = w_ada.shape
    cols = 2048
    rows = cin.shape[0]
    return pl.pallas_call(
        _ada_body,
        grid=(depth, n // cols),
        in_specs=[
            pl.BlockSpec((rows, d), lambda l, j: (0, 0)),
            pl.BlockSpec((1, d, cols), lambda l, j: (l, 0, j)),
            pl.BlockSpec((1, 1, cols), lambda l, j: (l, 0, j)),
        ],
        out_specs=pl.BlockSpec((1, rows, cols), lambda l, j: (l, 0, j)),
        out_shape=jax.ShapeDtypeStruct((depth, rows, n), F32),
        compiler_params=pltpu.CompilerParams(
            dimension_semantics=("arbitrary", "arbitrary"), vmem_limit_bytes=VMEM_LIMIT),
        name="ada_mod",
    )(cin, w_ada, b_ada.reshape(depth, 1, n))


def _two_head_rsqrt(xb, lo):
    x2 = xb * xb
    s_lo = jnp.sum(jnp.where(lo, x2, 0.0), axis=-1, keepdims=True)
    s_hi = jnp.sum(jnp.where(lo, 0.0, x2), axis=-1, keepdims=True)
    ms = jnp.where(lo, s_lo, s_hi) * (1.0 / GQA_HEAD_DIM)
    return lax.rsqrt(ms + RMS_EPS)


def _store_value_heads(v_ref, first, blk, lo):
    v_ref[first] = jnp.where(lo, blk, 1.0).astype(BF16)
    v_ref[first + 1] = jnp.where(lo, 1.0, blk).astype(BF16)


def _inproj_body(latent, names, *refs):
    r = dict(zip(names, refs))
    x = r["x"][...]
    tm = x.shape[0]
    mod = r["mod"][0]
    sh1 = mod[:, 0:D_MODEL]
    sc1 = mod[:, D_MODEL:2 * D_MODEL]
    h = (_rms(x, r["gmix"][...]) * (1.0 + sc1) + sh1).astype(BF16)
    z = _dot(h, r["w1"][...])

    lane = lax.broadcasted_iota(jnp.int32, (tm, LANE), 1)
    lo = lane < GQA_HEAD_DIM

    if latent:
        cg = r["cg"][...]
        sg = r["sg"][...]
        cm = r["cm"][...]
        sm = r["sm"][...]

    qn = _rms(z[:, _W1_QA:_W1_QA + Q_LORA], r["gqa"][...]).astype(BF16)
    qm = _dot(qn, r["wqb"][...])
    if latent:
        qms = _dot(qn, r["wqbsw"][...])
    for hh in range(MLA_HEADS):
        blk = qm[:, hh * LANE:(hh + 1) * LANE]
        if latent:
            blk = blk * cm + qms[:, hh * LANE:(hh + 1) * LANE] * sm
        r["q_mla"][hh] = blk.astype(BF16)

    ckv = _rms(z[:, _W1_CKV:_W1_CKV + KV_LORA], r["gkva"][...])
    kr = z[:, _W1_KR:_W1_KR + LANE]
    if latent:
        kr = kr * cm + z[:, _W1_KR_SW:_W1_KR_SW + LANE] * sm
    else:
        r["ckv_out"][...] = ckv
        r["kr_out"][...] = kr[:, 0:MLA_ROPE]
    ckv_b = ckv.astype(BF16)
    kfull = _dot(ckv_b, r["wk"][...]) + _dot(kr.astype(BF16), r["pk"][...])
    for hh in range(MLA_HEADS):
        r["k_mla"][hh] = kfull[:, hh * LANE:(hh + 1) * LANE].astype(BF16)
    vfull = _dot(ckv_b, r["wv"][...])
    for pp in range(N_PAIRS):
        _store_value_heads(r["v_mla"], 2 * pp, vfull[:, pp * LANE:(pp + 1) * LANE], lo)

    gq2 = r["gq2"][...]
    q_scale = GQA_HEAD_DIM ** -0.5
    for j in range(N_PAIRS):
        xb = z[:, _W1_GQ + j * LANE:_W1_GQ + (j + 1) * LANE]
        rs = _two_head_rsqrt(xb, lo)
        y = xb * rs * gq2
        if latent:
            xs = z[:, _W1_GQ_SW + j * LANE:_W1_GQ_SW + (j + 1) * LANE]
            y = y * cg + (xs * rs * r["gq2sw"][...]) * sg
        y = y * q_scale
        r["q_gqa"][2 * j] = jnp.where(lo, y, 0.0).astype(BF16)
        r["q_gqa"][2 * j + 1] = jnp.where(lo, 0.0, y).astype(BF16)
    kb = z[:, _W1_GK:_W1_GK + LANE]
    rs = _two_head_rsqrt(kb, lo)
    kg = kb * rs * r["gk2"][...]
    if latent:
        ks = z[:, _W1_GK_SW:_W1_GK_SW + LANE]
        kg = kg * cg + (ks * rs * r["gk2sw"][...]) * sg
    vg = z[:, _W1_GV:_W1_GV + LANE]
    if not latent:
        r["kg_out"][...] = kg
        r["vg_out"][...] = vg
    r["k_gqa"][...] = kg.astype(BF16)
    _store_value_heads(r["v_gqa"], 0, vg, lo)

    r["pool"][...] = z[:, _W1_POOL:_W1_POOL + POOL_WIDTH]


def _inproj_call(x, mod3, mod_base, seq_len, tm, latent, wl, tabs):
    n = x.shape[0]
    nt = n // tm
    per_seq = seq_len // tm if latent else 1

    def tile(i):
        return (i, 0)

    def mod_idx(i):
        if latent:
            return (mod_base + i // per_seq, 0, 0)
        return (mod_base, 0, 0)

    names = ["x", "mod", "gmix", "w1", "gqa", "wqb", "gkva", "wk", "pk", "wv", "gq2", "gk2"]
    args = [x, mod3, wl["gmix"], wl["w1_lat"] if latent else wl["w1_ctx"], wl["gqa"], wl["wqb"],
            wl["gkva"], wl["wk"], wl["pk_lat"] if latent else wl["pk_ctx"], wl["wv"], wl["gq2"], wl["gk2"]]
    specs = [pl.BlockSpec((tm, D_MODEL), tile), pl.BlockSpec((1, 1, 6 * D_MODEL), mod_idx)]
    specs += [_const_spec(a.shape) for a in args[2:]]
    if latent:
        extra = [("wqbsw", wl["wqbsw"]), ("gq2sw", wl["gq2sw"]), ("gk2sw", wl["gk2sw"])]
        for nm, a in extra:
            names.append(nm)
            args.append(a)
            specs.append(_const_spec(a.shape))
        for nm, a in zip(("cg", "sg", "cm", "sm"), tabs):
            names.append(nm)
            args.append(a)
            specs.append(pl.BlockSpec((tm, LANE), lambda i: (i % per_seq, 0)))

    head_spec = pl.BlockSpec((MLA_HEADS, tm, LANE), lambda i: (0, i, 0))
    kvh_spec = pl.BlockSpec((GQA_KV_HEADS, tm, LANE), lambda i: (0, i, 0))
    out_names = ["q_mla", "k_mla", "v_mla", "q_gqa", "k_gqa", "v_gqa", "pool"]
    out_shapes = [
        jax.ShapeDtypeStruct((MLA_HEADS, n, LANE), BF16),
        jax.ShapeDtypeStruct((MLA_HEADS, n, LANE), BF16),
        jax.ShapeDtypeStruct((MLA_HEADS, n, LANE), BF16),
        jax.ShapeDtypeStruct((GQA_HEADS, n, LANE), BF16),
        jax.ShapeDtypeStruct((n, LANE), BF16),
        jax.ShapeDtypeStruct((GQA_KV_HEADS, n, LANE), BF16),
        jax.ShapeDtypeStruct((n, POOL_WIDTH), F32),
    ]
    out_specs = [head_spec, head_spec, head_spec, head_spec,
                 pl.BlockSpec((tm, LANE), tile), kvh_spec,
                 pl.BlockSpec((tm, POOL_WIDTH), tile)]
    if not latent:
        out_names += ["ckv_out", "kr_out", "kg_out", "vg_out"]
        out_shapes += [jax.ShapeDtypeStruct((n, KV_LORA), F32), jax.ShapeDtypeStruct((n, MLA_ROPE), F32),
                       jax.ShapeDtypeStruct((n, LANE), F32), jax.ShapeDtypeStruct((n, LANE), F32)]
        out_specs += [pl.BlockSpec((tm, KV_LORA), tile), pl.BlockSpec((tm, MLA_ROPE), tile),
                      pl.BlockSpec((tm, LANE), tile), pl.BlockSpec((tm, LANE), tile)]

    outs = pl.pallas_call(
        functools.partial(_inproj_body, latent, names + out_names),
        grid=(nt,),
        in_specs=specs,
        out_specs=out_specs,
        out_shape=out_shapes,
        compiler_params=pltpu.CompilerParams(
            dimension_semantics=("arbitrary",), vmem_limit_bytes=VMEM_LIMIT),
        name="inproj_lat" if latent else "inproj_ctx",
    )(*args)
    return dict(zip(out_names, outs))


def _cachekv_body(ckv_ref, kr_ref, vg_ref, wk_ref, pk_ref, wv_ref, k_ref, v_ref, vgo_ref):
    ckv_b = ckv_ref[...]
    lo = lax.broadcasted_iota(jnp.int32, (ckv_b.shape[0], LANE), 1) < MLA_V
    kfull = _dot(ckv_b, wk_ref[...]) + _dot(kr_ref[...], pk_ref[...])
    for hh in range(MLA_HEADS):
        k_ref[hh] = kfull[:, hh * LANE:(hh + 1) * LANE].astype(BF16)
    vfull = _dot(ckv_b, wv_ref[...])
    for pp in range(N_PAIRS):
        _store_value_heads(v_ref, 2 * pp, vfull[:, pp * LANE:(pp + 1) * LANE], lo)
    _store_value_heads(vgo_ref, 0, vg_ref[...], lo)


def _cachekv_call(ckv_b, kr_pad_b, vg, wl):
    n = ckv_b.shape[0]
    args = [ckv_b, kr_pad_b, vg, wl["wk"], wl["pk_lat"], wl["wv"]]
    return pl.pallas_call(
        _cachekv_body,
        grid=(1,),
        in_specs=[_const_spec(a.shape) for a in args],
        out_specs=[_const_spec((MLA_HEADS, n, LANE)), _const_spec((MLA_HEADS, n, LANE)),
                   _const_spec((GQA_KV_HEADS, n, LANE))],
        out_shape=[jax.ShapeDtypeStruct((MLA_HEADS, n, LANE), BF16),
                   jax.ShapeDtypeStruct((MLA_HEADS, n, LANE), BF16),
                   jax.ShapeDtypeStruct((GQA_KV_HEADS, n, LANE), BF16)],
        compiler_params=pltpu.CompilerParams(
            dimension_semantics=("arbitrary",), vmem_limit_bytes=VMEM_LIMIT),
        name="cache_kv",
    )(*args)


def _normalise(o_full):
    return o_full / pltpu.roll(o_full, MLA_V, axis=1)


def _attn_body(pp, shared_kv, exp_scale, q_ref, k_ref, v_ref, o_ref):
    tq = q_ref.shape[1]
    lo = lax.broadcasted_iota(jnp.int32, (tq, LANE), 1) < MLA_V
    nt_dims = (((1,), (1,)), ((), ()))
    for pr in range(pp):
        outs = []
        for e in range(2):
            q = q_ref[2 * pr + e]
            k = k_ref[0] if shared_kv else k_ref[2 * pr + e]
            v = v_ref[e] if shared_kv else v_ref[2 * pr + e]
            s = lax.dot_general(q, k, nt_dims, preferred_element_type=F32)
            m = jnp.max(s, axis=-1, keepdims=True)
            p = jnp.exp2((s - m) * exp_scale)
            outs.append(_normalise(_dot(p.astype(BF16), v)))
        o_ref[:, pr * LANE:(pr + 1) * LANE] = jnp.where(lo, outs[0], outs[1]).astype(BF16)


def _attn_call(q, k, v, batch, seq_len, tq, pp, scale, shared_kv, name):
    n = q.shape[1]
    nq = seq_len // tq
    grid = (batch, N_PAIRS // pp, nq)
    exp_scale = scale * math.log2(math.e)
    if shared_kv:
        k_spec = pl.BlockSpec((1, seq_len, LANE), lambda b, p, i: (0, b, 0))
        v_spec = pl.BlockSpec((2, seq_len, LANE), lambda b, p, i: (0, b, 0))
    else:
        k_spec = pl.BlockSpec((2 * pp, seq_len, LANE), lambda b, p, i: (p, b, 0))
        v_spec = k_spec
    return pl.pallas_call(
        functools.partial(_attn_body, pp, shared_kv, exp_scale),
        grid=grid,
        in_specs=[pl.BlockSpec((2 * pp, tq, LANE), lambda b, p, i: (p, b * nq + i, 0)), k_spec, v_spec],
        out_specs=pl.BlockSpec((tq, pp * LANE), lambda b, p, i: (b * nq + i, p)),
        out_shape=jax.ShapeDtypeStruct((n, N_PAIRS * LANE), BF16),
        compiler_params=pltpu.CompilerParams(
            dimension_semantics=("arbitrary", "arbitrary", "arbitrary"), vmem_limit_bytes=VMEM_LIMIT),
        name=name,
    )(q, k, v)


def _attn_pipe_body(shared_kv, exp_scale, tq, q_ref, k_ref, v_ref, kc_ref, vc_ref, o_ref,
                    s0, s1, mr0, mr1, ac0, ac1, os0):
    seq = k_ref.shape[1]
    past = kc_ref.shape[1]
    nq = seq // tq
    n_new = seq // KEY_TILE
    n_tiles = n_new + past // KEY_TILE
    s_scr = (s0, s1)
    mrun_scr = (mr0, mr1)
    acc_scr = (ac0, ac1)
    nt_dims = (((1,), (1,)), ((), ()))
    lo = lax.broadcasted_iota(jnp.int32, (tq, LANE), 1) < MLA_V

    def rows(j):
        return pl.ds(pl.multiple_of(j * tq, tq), tq)

    def kv_tile(new_ref, cache_ref, head, t):
        if t < n_new:
            return new_ref[head, t * KEY_TILE:(t + 1) * KEY_TILE, :]
        t -= n_new
        return cache_ref[head, t * KEY_TILE:(t + 1) * KEY_TILE, :]

    def finish_prev(e_prev, j_prev):
        o_prev = _normalise(acc_scr[e_prev][...])
        if e_prev == 0:
            os0[...] = o_prev
        else:
            o_ref[rows(j_prev), :] = jnp.where(lo, os0[...], o_prev).astype(BF16)

    def region(j_scores, e_scores, e_cur, j_prev):
        if j_prev is not None:
            finish_prev(1 - e_cur, j_prev)
        if e_scores is not None:
            q = q_ref[e_scores, rows(j_scores), :]
            k_head = 0 if shared_kv else e_scores
        if e_cur is not None:
            m = jnp.max(mrun_scr[e_cur][...], axis=-1, keepdims=True)
            m_b = jnp.broadcast_to(m, (tq, LANE))
        acc = None
        for t in range(n_tiles):
            cols = slice(t * KEY_TILE, (t + 1) * KEY_TILE)
            if e_scores is not None:
                s_t = lax.dot_general(q, kv_tile(k_ref, kc_ref, k_head, t), nt_dims,
                                      preferred_element_type=F32)
                s_scr[e_scores][:, cols] = s_t
                m_t = jnp.maximum(s_t[:, 0:LANE], s_t[:, LANE:2 * LANE])
                if t > 0:
                    m_t = jnp.maximum(m_t, mrun_scr[e_scores][...])
                mrun_scr[e_scores][...] = m_t
            if e_cur is not None:
                p_parts = []
                for hh in range(KEY_TILE // LANE):
                    c0 = t * KEY_TILE + hh * LANE
                    s_h = s_scr[e_cur][:, c0:c0 + LANE]
                    p_parts.append(jnp.exp2((s_h - m_b) * exp_scale).astype(BF16))
                p_t = jnp.concatenate(p_parts, axis=1)
                part = _dot(p_t, kv_tile(v_ref, vc_ref, e_cur, t))
                acc = part if acc is None else acc + part
        if e_cur is not None:
            acc_scr[e_cur][...] = acc

    region(0, 0, None, None)
    region(0, 1, 0, None)
    region(1, 0, 1, 0)

    def body(j, carry):
        @pl.when(j > 0)
        def _():
            region(j, 1, 0, j - 1)

        @pl.when(j < nq)
        def _():
            region(j + 1, 0, 1, j)

        return carry

    lax.fori_loop(1, nq - 1, body, 0)
    region(nq - 1, 1, 0, nq - 2)
    region(None, None, 1, nq - 1)
    finish_prev(1, nq - 1)


def _attn_pipe_call(q, k, v, kc, vc, batch, seq_len, tq, scale, shared_kv, name):
    n = q.shape[1]
    past = kc.shape[1] // batch
    exp_scale = scale * math.log2(math.e)

    def kv_spec(shared, rows):
        if shared:
            return pl.BlockSpec((1, rows, LANE), lambda b, p: (0, b, 0))
        return pl.BlockSpec((2, rows, LANE), lambda b, p: (p, b, 0))

    def v_spec(rows):
        if shared_kv:
            return pl.BlockSpec((2, rows, LANE), lambda b, p: (0, b, 0))
        return pl.BlockSpec((2, rows, LANE), lambda b, p: (p, b, 0))

    total = seq_len + past
    return pl.pallas_call(
        functools.partial(_attn_pipe_body, shared_kv, exp_scale, tq),
        grid=(batch, N_PAIRS),
        in_specs=[pl.BlockSpec((2, seq_len, LANE), lambda b, p: (p, b, 0)),
                  kv_spec(shared_kv, seq_len), v_spec(seq_len), kv_spec(shared_kv, past), v_spec(past)],
        out_specs=pl.BlockSpec((seq_len, LANE), lambda b, p: (b, p)),
        out_shape=jax.ShapeDtypeStruct((n, N_PAIRS * LANE), BF16),
        scratch_shapes=[pltpu.VMEM((tq, total), F32), pltpu.VMEM((tq, total), F32)]
        + [pltpu.VMEM((tq, LANE), F32) for _ in range(5)],
        compiler_params=pltpu.CompilerParams(
            dimension_semantics=("arbitrary", "arbitrary"), vmem_limit_bytes=VMEM_LIMIT),
        name=name,
    )(q, k, v, kc, vc)


def _seq_pos(tm, seq_len):
    i = pl.program_id(0)
    row = lax.broadcasted_iota(jnp.int32, (tm, 1), 0) + i * tm
    return jnp.bitwise_and(row, seq_len - 1)


def _merge_body(seq_len, x_ref, mod_ref, gmix_ref, wg_ref, a_ref, b_ref, pc_ref, pl_ref, pr_ref,
                wpool_ref, pscale_ref, wa_ref, wb_ref, wc_ref, wo_ref, o_ref):
    x = x_ref[...]
    tm = x.shape[0]
    mod = mod_ref[0]
    sh1 = mod[:, 0:D_MODEL]
    sc1 = mod[:, D_MODEL:2 * D_MODEL]
    g1 = mod[:, 2 * D_MODEL:3 * D_MODEL]
    h = (_rms(x, gmix_ref[...]) * (1.0 + sc1) + sh1).astype(BF16)
    gates = jax.nn.sigmoid(_dot(h, wg_ref[...]))

    pos = _seq_pos(tm, seq_len)
    ext = jnp.concatenate([pl_ref[...], pc_ref[...], pr_ref[...]], axis=0)
    pscale = pscale_ref[...]
    c_parts = []
    for g, w in enumerate(POOL_WINDOWS):
        eg = ext[:, g * POOL_GROUP:(g + 1) * POOL_GROUP]
        wsum = jnp.zeros((tm, POOL_GROUP), F32)
        cnt = jnp.zeros((tm, 1), F32)
        for dlt in range(-(w // 2), w - w // 2):
            valid = jnp.logical_and(pos + dlt >= 0, pos + dlt < seq_len)
            wsum = wsum + jnp.where(valid, eg[HALO + dlt:HALO + dlt + tm], 0.0)
            cnt = cnt + valid.astype(F32)
        pooled = (wsum / cnt - eg[HALO:HALO + tm]).astype(BF16)
        c_parts.append(_dot(pooled, wpool_ref[g]) * pscale[:, g * POOL_GROUP:(g + 1) * POOL_GROUP])
    c_out = jnp.concatenate(c_parts, axis=-1).astype(BF16)

    merged = (gates[:, 0:D_MODEL] * _dot(a_ref[...], wa_ref[...])
              + gates[:, D_MODEL:2 * D_MODEL] * _dot(b_ref[...], wb_ref[...])
              + gates[:, 2 * D_MODEL:3 * D_MODEL] * _dot(c_out, wc_ref[...]))
    o_ref[...] = x + g1 * _dot(merged.astype(BF16), wo_ref[...])


def _halo_specs(tm, width, n):
    blocks = tm // HALO
    last = n // HALO - 1
    left = pl.BlockSpec((HALO, width), lambda i: (jnp.maximum(i * blocks - 1, 0), 0))
    right = pl.BlockSpec((HALO, width), lambda i: (jnp.minimum((i + 1) * blocks, last), 0))
    return left, right


def _mod_spec(mod_base, per_seq, latent):
    if latent:
        return pl.BlockSpec((1, 1, 6 * D_MODEL), lambda i: (mod_base + i // per_seq, 0, 0))
    return pl.BlockSpec((1, 1, 6 * D_MODEL), lambda i: (mod_base, 0, 0))


def _merge_call(x, mod3, mod_base, seq_len, tm, latent, a_out, b_out, pool_in, wl):
    n = x.shape[0]
    per_seq = seq_len // tm if latent else 1

    def tile(i):
        return (i, 0)

    left, right = _halo_specs(tm, POOL_WIDTH, n)
    consts = [wl["wpool"], wl["pscale"], wl["wa"], wl["wb"], wl["wc"], wl["wo"]]
    in_specs = [pl.BlockSpec((tm, D_MODEL), tile), _mod_spec(mod_base, per_seq, latent),
                _const_spec(wl["gmix"].shape), _const_spec(wl["wg"].shape),
                pl.BlockSpec((tm, N_PAIRS * LANE), tile), pl.BlockSpec((tm, N_PAIRS * LANE), tile),
                pl.BlockSpec((tm, POOL_WIDTH), tile), left, right]
    in_specs += [_const_spec(a.shape) for a in consts]
    return pl.pallas_call(
        functools.partial(_merge_body, seq_len),
        grid=(n // tm,),
        in_specs=in_specs,
        out_specs=pl.BlockSpec((tm, D_MODEL), tile),
        out_shape=jax.ShapeDtypeStruct((n, D_MODEL), F32),
        compiler_params=pltpu.CompilerParams(
            dimension_semantics=("arbitrary",), vmem_limit_bytes=VMEM_LIMIT),
        name="merge_lat" if latent else "merge_ctx",
    )(x, mod3, wl["gmix"], wl["wg"], a_out, b_out, pool_in, pool_in, pool_in, *consts)


def _ffn_body(seq_len, final, x_ref, xl_ref, xr_ref, mod_ref, gffn_ref, wug_ref, wuv_ref, cw_ref, cb_ref,
              wd_ref, gfin_ref, o_ref):
    x = x_ref[...]
    tm = x.shape[0]
    mod = mod_ref[0]
    sh2 = mod[:, 3 * D_MODEL:4 * D_MODEL]
    sc2 = mod[:, 4 * D_MODEL:5 * D_MODEL]
    g2 = mod[:, 5 * D_MODEL:6 * D_MODEL]
    xe = jnp.concatenate([xl_ref[...], x, xr_ref[...]], axis=0)
    h2e = _rms(xe, gffn_ref[...]) * (1.0 + sc2) + sh2
    h2 = h2e[HALO:HALO + tm].astype(BF16)
    h2e = h2e.astype(BF16)

    pos = _seq_pos(tm, seq_len)
    has_prev = pos >= 1
    has_next = pos <= seq_len - 2
    cw = cw_ref[...]
    cb = cb_ref[...]
    acc = jnp.zeros((tm, D_MODEL), F32)
    for j in range(D_FF // FF_CHUNK):
        cols = slice(j * FF_CHUNK, (j + 1) * FF_CHUNK)
        ge = _dot(h2e, wug_ref[:, cols])
        g = (jnp.where(has_prev, ge[HALO - 1:HALO - 1 + tm], 0.0) * cw[0:1, cols]
             + ge[HALO:HALO + tm] * cw[1:2, cols]
             + jnp.where(has_next, ge[HALO + 1:HALO + 1 + tm], 0.0) * cw[2:3, cols]
             + cb[:, cols])
        val = _dot(h2, wuv_ref[:, cols])
        act = (g * jax.nn.sigmoid(g) * val).astype(BF16)
        acc = acc + _dot(act, wd_ref[cols, :])
    y = x + g2 * acc
    if final:
        y = _rms(y, gfin_ref[...])
    o_ref[...] = y


def _ffn_call(x, mod3, mod_base, seq_len, tm, latent, final, wl, g_final):
    n = x.shape[0]
    per_seq = seq_len // tm if latent else 1

    def tile(i):
        return (i, 0)

    left, right = _halo_specs(tm, D_MODEL, n)
    consts = [wl["gffn"], wl["wug"], wl["wuv"], wl["cw"], wl["cb"], wl["wd"], g_final]
    in_specs = [pl.BlockSpec((tm, D_MODEL), tile), left, right, _mod_spec(mod_base, per_seq, latent)]
    in_specs += [_const_spec(a.shape) for a in consts]
    return pl.pallas_call(
        functools.partial(_ffn_body, seq_len, final),
        grid=(n // tm,),
        in_specs=in_specs,
        out_specs=pl.BlockSpec((tm, D_MODEL), tile),
        out_shape=jax.ShapeDtypeStruct((n, D_MODEL), F32),
        compiler_params=pltpu.CompilerParams(
            dimension_semantics=("arbitrary",), vmem_limit_bytes=VMEM_LIMIT),
        name="ffn_lat" if latent else "ffn_ctx",
    )(x, x, x, mod3, *consts)


def _prep_layer(l, g_norm_mix, w_in, g_q_a, w_q_b, g_kv_a, w_kv_b, g_q_gqa, g_k_gqa, w_pool, pool_scale,
                w_br_a, w_br_b, w_br_c, w_out, g_norm_ffn, w_up, conv_w, conv_b, w_down):
    wi = w_in[l]
    sw64 = _swap_index(GQA_HEAD_DIM)
    sw32 = _swap_index(MLA_ROPE)
    hd = GQA_HEAD_DIM

    pair_heads = []
    for j in range(N_PAIRS):
        pair_heads += [j, j + GQA_HEADS // GQA_KV_HEADS]
    gq_cols = np.concatenate([_OFF_GQ + hh * hd + np.arange(hd) for hh in pair_heads])
    gq_sw_cols = np.concatenate([_OFF_GQ + hh * hd + sw64 for hh in pair_heads])
    gk_cols = _OFF_GK + np.arange(GQA_KV_HEADS * hd)
    gk_sw_cols = np.concatenate([_OFF_GK + hh * hd + sw64 for hh in range(GQA_KV_HEADS)])

    def kr_block(cols, lane_off):
        blk = jnp.zeros((D_MODEL, LANE), F32)
        return blk.at[:, lane_off:lane_off + MLA_ROPE].set(wi[:, cols])

    kr_cols = _OFF_KR + np.arange(MLA_ROPE)
    base = [wi[:, _OFF_QA:_OFF_QA + Q_LORA], wi[:, _OFF_CKV:_OFF_CKV + KV_LORA], wi[:, gq_cols],
            wi[:, gk_cols], wi[:, _OFF_GV:_OFF_GV + GQA_KV_HEADS * hd], wi[:, _OFF_POOL:_OFF_POOL + POOL_WIDTH]]
    w1_ctx = jnp.concatenate(base + [kr_block(kr_cols, 0)], axis=1).astype(BF16)
    w1_lat = jnp.concatenate(
        base + [kr_block(kr_cols, MLA_NOPE), wi[:, gq_sw_cols], wi[:, gk_sw_cols],
                kr_block(_OFF_KR + sw32, MLA_NOPE)], axis=1).astype(BF16)

    qb = w_q_b[l].reshape(Q_LORA, MLA_HEADS, MLA_NOPE + MLA_ROPE)
    pad = LANE - MLA_NOPE - MLA_ROPE
    zpad = jnp.zeros((Q_LORA, MLA_HEADS, pad), F32)
    wqb = jnp.concatenate([qb, zpad], axis=-1).reshape(Q_LORA, MLA_HEADS * LANE).astype(BF16)
    qb_sw = qb[:, :, MLA_NOPE + sw32]
    wqbsw = jnp.concatenate([jnp.zeros((Q_LORA, MLA_HEADS, MLA_NOPE), F32), qb_sw, zpad],
                            axis=-1).reshape(Q_LORA, MLA_HEADS * LANE).astype(BF16)

    kvb = w_kv_b[l].reshape(KV_LORA, MLA_HEADS, MLA_NOPE + MLA_V)
    wk = jnp.concatenate([kvb[:, :, :MLA_NOPE], jnp.zeros((KV_LORA, MLA_HEADS, LANE - MLA_NOPE), F32)],
                         axis=-1).reshape(KV_LORA, MLA_HEADS * LANE).astype(BF16)
    wv = kvb[:, :, MLA_NOPE:].reshape(KV_LORA, MLA_HEADS * MLA_V).astype(BF16)

    def place(lane_off):
        pk = np.zeros((LANE, MLA_HEADS * LANE), np.float32)
        for hh in range(MLA_HEADS):
            for t in range(MLA_ROPE):
                pk[lane_off + t, hh * LANE + MLA_NOPE + t] = 1.0
        return jnp.asarray(pk, BF16)

    gq = g_q_gqa[l]
    gk = g_k_gqa[l]
    row = lambda a: a.reshape(1, -1)
    wb_rows = np.concatenate([hh * hd + np.arange(hd) for hh in pair_heads])
    wu = w_up[l]
    return dict(
        gmix=row(g_norm_mix[l]), w1_ctx=w1_ctx, w1_lat=w1_lat, gqa=row(g_q_a[l]), wqb=wqb, wqbsw=wqbsw,
        gkva=row(g_kv_a[l]), wk=wk, pk_ctx=place(0), pk_lat=place(MLA_NOPE), wv=wv,
        gq2=row(jnp.concatenate([gq, gq])), gq2sw=row(jnp.concatenate([gq[sw64], gq[sw64]])),
        gk2=row(jnp.concatenate([gk, gk])), gk2sw=row(jnp.concatenate([gk[sw64], gk[sw64]])),
        wg=wi[:, _OFF_GATE:].astype(BF16), wpool=w_pool[l].astype(BF16), pscale=row(pool_scale[l]),
        wa=w_br_a[l].astype(BF16), wb=w_br_b[l][wb_rows].astype(BF16), wc=w_br_c[l].astype(BF16),
        wo=w_out[l].astype(BF16), gffn=row(g_norm_ffn[l]), wug=wu[:, :D_FF].astype(BF16),
        wuv=wu[:, D_FF:].astype(BF16), cw=conv_w[l], cb=row(conv_b[l]), wd=w_down[l].astype(BF16),
    )


def _layer(x, mod3, mod_base, batch, seq_len, tm, tq, pp, latent, final, wl, g_final, tabs, cache):
    pj = _inproj_call(x, mod3, mod_base, seq_len, tm, latent, wl, tabs)
    kc_m = vc_m = kc_g = vc_g = None
    if cache is not None:
        ckv_b, kr_pad_b, kc_g, vg_b = cache
        kc_m, vc_m, vc_g = _cachekv_call(ckv_b, kr_pad_b, vg_b, wl)
    mla_scale = (MLA_NOPE + MLA_ROPE) ** -0.5
    n = x.shape[0]
    k_g = pj["k_gqa"].reshape(1, n, LANE)
    if latent:
        a_out = _attn_pipe_call(pj["q_mla"], pj["k_mla"], pj["v_mla"], kc_m, vc_m, batch, seq_len, tq,
                                mla_scale, False, "attn_mla_lat")
        b_out = _attn_pipe_call(pj["q_gqa"], k_g, pj["v_gqa"], kc_g, vc_g, batch, seq_len, tq, 1.0, True,
                                "attn_gqa_lat")
    else:
        a_out = _attn_call(pj["q_mla"], pj["k_mla"], pj["v_mla"], batch, seq_len, tq, pp,
                           mla_scale, False, "attn_mla_ctx")
        b_out = _attn_call(pj["q_gqa"], k_g, pj["v_gqa"], batch, seq_len, tq, pp, 1.0, True,
                           "attn_gqa_ctx")
    x1 = _merge_call(x, mod3, mod_base, seq_len, tm, latent, a_out, b_out, pj["pool"], wl)
    x2 = _ffn_call(x1, mod3, mod_base, seq_len, tm, latent, final, wl, g_final)
    return x2, pj


def kernel(x_prompt, x_sample, c, cache_mla_ckv, cache_mla_krope, cache_gqa_k, cache_gqa_v, c_ctx, w_ada, b_ada, g_norm_mix, w_in, g_q_a, w_q_b, g_kv_a, w_kv_b, g_q_gqa, g_k_gqa, w_pool, pool_scale, w_br_a, w_br_b, w_br_c, w_out, g_norm_ffn, w_up, conv_w, conv_b, w_down, g_final):
    depth = w_in.shape[0]
    bc, tc, _ = x_prompt.shape
    bl, tl, _ = x_sample.shape
    past = cache_mla_ckv.shape[2]
    mod_rows = 8
    assert 1 + bl <= mod_rows and tc & (tc - 1) == 0 and tl & (tl - 1) == 0

    cin = jnp.concatenate([c_ctx[None, :], c, jnp.zeros((mod_rows - 1 - bl, D_MODEL), F32)], axis=0)
    mods = _ada_call(cin, w_ada, b_ada)
    tabs = _rope_tables(tl)
    g_fin = g_final.reshape(1, D_MODEL)

    xc = x_prompt.reshape(bc * tc, D_MODEL)
    xl = x_sample.reshape(bl * tl, D_MODEL)
    st_ckv, st_kr, st_k, st_v = [], [], [], []
    for l in range(depth):
        wl = _prep_layer(l, g_norm_mix, w_in, g_q_a, w_q_b, g_kv_a, w_kv_b, g_q_gqa, g_k_gqa, w_pool,
                         pool_scale, w_br_a, w_br_b, w_br_c, w_out, g_norm_ffn, w_up, conv_w, conv_b, w_down)
        mod3 = mods[l].reshape(mod_rows, 1, 6 * D_MODEL)
        final = l == depth - 1
        xc, pj = _layer(xc, mod3, 0, bc, tc, 2 * tc, tc, N_PAIRS, False, final, wl, g_fin, None, None)
        st_ckv.append(pj["ckv_out"].reshape(bc, tc, KV_LORA))
        st_kr.append(pj["kr_out"].reshape(bc, tc, MLA_ROPE))
        st_k.append(pj["kg_out"].reshape(bc, tc, GQA_KV_HEADS, GQA_HEAD_DIM))
        st_v.append(pj["vg_out"].reshape(bc, tc, GQA_KV_HEADS, GQA_HEAD_DIM))

        kr_pad = jnp.zeros((bl * past, LANE), F32).at[:, MLA_NOPE:MLA_NOPE + MLA_ROPE].set(
            cache_mla_krope[:, l].reshape(bl * past, MLA_ROPE)).astype(BF16)
        cache = (cache_mla_ckv[:, l].reshape(bl * past, KV_LORA).astype(BF16), kr_pad,
                 cache_gqa_k[:, l].reshape(1, bl * past, LANE).astype(BF16),
                 cache_gqa_v[:, l].reshape(bl * past, LANE))
        xl, _ = _layer(xl, mod3, 1, bl, tl, 512, 256, 1, True, final, wl, g_fin, tabs, cache)

    return (xc.reshape(bc, tc, D_MODEL), xl.reshape(bl, tl, D_MODEL),
            jnp.stack(st_ckv, axis=1), jnp.stack(st_kr, axis=1),
            jnp.stack(st_k, axis=1), jnp.stack(st_v, axis=1))
```

```python
import functools
import math

import numpy as np
import jax
import jax.numpy as jnp
from jax import lax
from jax.experimental import pallas as pl
from jax.experimental.pallas import tpu as pltpu

D_MODEL = 1024
GRID_W = 64
RMS_EPS = 1e-6
ROPE_THETA = 10000.0
MLA_HEADS = 8
MLA_NOPE = 64
MLA_ROPE = 32
MLA_V = 64
Q_LORA = 256
KV_LORA = 128
GQA_HEADS = 8
GQA_KV_HEADS = 2
GQA_HEAD_DIM = 64
POOL_WINDOWS = (2, 4, 8, 16)
POOL_GROUP = 128
POOL_WIDTH = POOL_GROUP * len(POOL_WINDOWS)
D_FF = 2816
N_BRANCH = 3

LANE = 128
HALO = 8
FF_CHUNK = 256
KEY_TILE = 256
CTX_GROUP = 1
CTX_Q_ROWS = 256
TOKEN_TILE = 512
Q_TILE = 256
N_PAIRS = MLA_HEADS // 2
VMEM_LIMIT = 56 * 1024 * 1024

BF16 = jnp.bfloat16
F32 = jnp.float32

_OFF_QA = 0
_OFF_CKV = _OFF_QA + Q_LORA
_OFF_KR = _OFF_CKV + KV_LORA
_OFF_GQ = _OFF_KR + MLA_ROPE
_OFF_GK = _OFF_GQ + GQA_HEADS * GQA_HEAD_DIM
_OFF_GV = _OFF_GK + GQA_KV_HEADS * GQA_HEAD_DIM
_OFF_POOL = _OFF_GV + GQA_KV_HEADS * GQA_HEAD_DIM
_OFF_GATE = _OFF_POOL + POOL_WIDTH

_W1_QA = 0
_W1_CKV = 256
_W1_GQ = 384
_W1_GK = 896
_W1_GV = 1024
_W1_POOL = 1152
_W1_KR = 1664
_W1_CTX_COLS = 1792
_W1_GQ_SW = 1792
_W1_GK_SW = 2304
_W1_KR_SW = 2432
_W1_LAT_COLS = 2560


def _const_spec(shape):
    nd = len(shape)
    return pl.BlockSpec(shape, lambda *_: (0,) * nd, pipeline_mode=pl.Buffered(1))


def _wspec(a, layer, cols=None, col_block=0):
    shape = a.shape[1:] if cols is None else a.shape[1:-1] + (cols,)
    idx = (layer,) + (0,) * (len(shape) - 1) + (col_block,)
    return pl.BlockSpec((None,) + shape, lambda *_: idx, pipeline_mode=pl.Buffered(1))


def _rms(x, g):
    ms = jnp.mean(x * x, axis=-1, keepdims=True)
    return x * lax.rsqrt(ms + RMS_EPS) * g


def _dot(a, b):
    return jnp.dot(a, b, preferred_element_type=F32)


def _rope_head_tables(n_tokens, d):
    dim_axis = d // 2
    t = np.arange(n_tokens)
    row = (t // GRID_W).astype(np.float64)
    col = (t % GRID_W).astype(np.float64)
    freqs = ROPE_THETA ** (-np.arange(0, dim_axis, 2, dtype=np.float64) / dim_axis)
    ar = row[:, None] * freqs[None, :]
    ac = col[:, None] * freqs[None, :]
    cos = np.concatenate([np.cos(ar), np.cos(ar), np.cos(ac), np.cos(ac)], axis=1)
    sin = np.concatenate([-np.sin(ar), np.sin(ar), -np.sin(ac), np.sin(ac)], axis=1)
    return cos, sin


def _rope_tables(n_tokens):
    cg, sg = _rope_head_tables(n_tokens, GQA_HEAD_DIM)
    cg = np.concatenate([cg, cg], axis=1)
    sg = np.concatenate([sg, sg], axis=1)
    cm32, sm32 = _rope_head_tables(n_tokens, MLA_ROPE)
    pad = LANE - MLA_NOPE - MLA_ROPE
    cm = np.concatenate([np.ones((n_tokens, MLA_NOPE)), cm32, np.zeros((n_tokens, pad))], axis=1)
    sm = np.concatenate([np.zeros((n_tokens, MLA_NOPE)), sm32, np.zeros((n_tokens, pad))], axis=1)
    return tuple(jnp.asarray(a, F32) for a in (cg, sg, cm, sm))


def _ada_body(c_ref, w_ref, b_ref, o_ref):
    c = c_ref[...]
    s = (c * jax.nn.sigmoid(c)).astype(BF16)
    o_ref[0] = _dot(s, w_ref[0].astype(BF16)) + b_ref[0]


def _ada_call(cin, w_ada, b_ada):
    depth, d, n = w_ada.shape
    cols = 2048
    rows = cin.shape[0]
    return pl.pallas_call(
        _ada_body,
        grid=(depth, n // cols),
        in_specs=[
            pl.BlockSpec((rows, d), lambda l, j: (0, 0)),
            pl.BlockSpec((1, d, cols), lambda l, j: (l, 0, j)),
            pl.BlockSpec((1, 1, cols), lambda l, j: (l, 0, j)),
        ],
        out_specs=pl.BlockSpec((1, rows, cols), lambda l, j: (l, 0, j)),
        out_shape=jax.ShapeDtypeStruct((depth, rows, n), F32),
        compiler_params=pltpu.CompilerParams(
            dimension_semantics=("arbitrary", "arbitrary"), vmem_limit_bytes=VMEM_LIMIT),
        name="ada_mod",
    )(cin, w_ada, b_ada.reshape(depth, 1, n))


def _two_head_rsqrt(xb, lo):
    x2 = xb * xb
    s_lo = jnp.sum(jnp.where(lo, x2, 0.0), axis=-1, keepdims=True)
    s_hi = jnp.sum(jnp.where(lo, 0.0, x2), axis=-1, keepdims=True)
    ms = jnp.where(lo, s_lo, s_hi) * (1.0 / GQA_HEAD_DIM)
    return lax.rsqrt(ms + RMS_EPS)


def _store_value_heads(v_ref, first, blk, lo):
    v_ref[first] = jnp.where(lo, blk, 1.0).astype(BF16)
    v_ref[first + 1] = jnp.where(lo, 1.0, blk).astype(BF16)


def _inproj_body(latent, names, *refs):
    r = dict(zip(names, refs))
    x = r["x"][...]
    tm = x.shape[0]
    mod = r["mod"][0]
    sh1 = mod[:, 0:D_MODEL]
    sc1 = mod[:, D_MODEL:2 * D_MODEL]
    h = (_rms(x, r["gmix"][...]) * (1.0 + sc1) + sh1).astype(BF16)
    z = _dot(h, r["w1"][...])

    lane = lax.broadcasted_iota(jnp.int32, (tm, LANE), 1)
    lo = lane < GQA_HEAD_DIM

    if latent:
        cg = r["cg"][...]
        sg = r["sg"][...]
        cm = r["cm"][...]
        sm = r["sm"][...]

    qn = _rms(z[:, _W1_QA:_W1_QA + Q_LORA], r["gqa"][...]).astype(BF16)
    qm = _dot(qn, r["wqb"][...])
    if latent:
        qms = _dot(qn, r["wqbsw"][...])
    for hh in range(MLA_HEADS):
        blk = qm[:, hh * LANE:(hh + 1) * LANE]
        if latent:
            blk = blk * cm + qms[:, hh * LANE:(hh + 1) * LANE] * sm
        r["q_mla"][hh] = blk.astype(BF16)

    ckv = _rms(z[:, _W1_CKV:_W1_CKV + KV_LORA], r["gkva"][...])
    kr = z[:, _W1_KR:_W1_KR + LANE]
    if latent:
        kr = kr * cm + z[:, _W1_KR_SW:_W1_KR_SW + LANE] * sm
    else:
        r["ckv_out"][...] = ckv
        r["kr_out"][...] = kr[:, MLA_NOPE:MLA_NOPE + MLA_ROPE]
    ckv_b = ckv.astype(BF16)
    kfull = _dot(ckv_b, r["wk"][...]) + _dot(kr.astype(BF16), r["pk"][...])
    for hh in range(MLA_HEADS):
        r["k_mla"][hh] = kfull[:, hh * LANE:(hh + 1) * LANE].astype(BF16)
    vfull = _dot(ckv_b, r["wv"][...])
    for pp in range(N_PAIRS):
        _store_value_heads(r["v_mla"], 2 * pp, vfull[:, pp * LANE:(pp + 1) * LANE], lo)

    gq2 = r["gq2"][...]
    q_scale = GQA_HEAD_DIM ** -0.5
    for j in range(N_PAIRS):
        xb = z[:, _W1_GQ + j * LANE:_W1_GQ + (j + 1) * LANE]
        rs = _two_head_rsqrt(xb, lo)
        y = xb * rs * gq2
        if latent:
            xs = z[:, _W1_GQ_SW + j * LANE:_W1_GQ_SW + (j + 1) * LANE]
            y = y * cg + (xs * rs * r["gq2sw"][...]) * sg
        y = y * q_scale
        r["q_gqa"][2 * j] = jnp.where(lo, y, 0.0).astype(BF16)
        r["q_gqa"][2 * j + 1] = jnp.where(lo, 0.0, y).astype(BF16)
    kb = z[:, _W1_GK:_W1_GK + LANE]
    rs = _two_head_rsqrt(kb, lo)
    kg = kb * rs * r["gk2"][...]
    if latent:
        ks = z[:, _W1_GK_SW:_W1_GK_SW + LANE]
        kg = kg * cg + (ks * rs * r["gk2sw"][...]) * sg
    vg = z[:, _W1_GV:_W1_GV + LANE]
    if not latent:
        r["kg_out"][...] = kg
        r["vg_out"][...] = vg
    r["k_gqa"][...] = kg.astype(BF16)
    _store_value_heads(r["v_gqa"], 0, vg, lo)

    r["pool"][...] = z[:, _W1_POOL:_W1_POOL + POOL_WIDTH]


def _inproj_call(x, mod4, layer, mod_base, seq_len, tm, latent, wl, tabs):
    n = x.shape[0]
    nt = n // tm
    per_seq = seq_len // tm if latent else 1

    def tile(i):
        return (i, 0)

    def mod_idx(i):
        if latent:
            return (layer, mod_base + i // per_seq, 0, 0)
        return (layer, mod_base, 0, 0)

    names = ["x", "mod", "w1", "pk", "gmix", "gqa", "wqb", "gkva", "wk", "wv", "gq2", "gk2"]
    args = [x, mod4, wl["w1"], wl["pk"]] + [wl[nm] for nm in names[4:]]
    specs = [pl.BlockSpec((tm, D_MODEL), tile), pl.BlockSpec((None, 1, 1, 6 * D_MODEL), mod_idx),
             _wspec(wl["w1"], layer, cols=_W1_LAT_COLS if latent else _W1_CTX_COLS),
             _const_spec(wl["pk"].shape)]
    specs += [_wspec(a, layer) for a in args[4:]]
    if latent:
        for nm in ("wqbsw", "gq2sw", "gk2sw"):
            names.append(nm)
            args.append(wl[nm])
            specs.append(_wspec(wl[nm], layer))
        for nm, a in zip(("cg", "sg", "cm", "sm"), tabs):
            names.append(nm)
            args.append(a)
            specs.append(pl.BlockSpec((tm, LANE), lambda i: (i % per_seq, 0)))

    head_spec = pl.BlockSpec((MLA_HEADS, tm, LANE), lambda i: (0, i, 0))
    kvh_spec = pl.BlockSpec((GQA_KV_HEADS, tm, LANE), lambda i: (0, i, 0))
    out_names = ["q_mla", "k_mla", "v_mla", "q_gqa", "k_gqa", "v_gqa", "pool"]
    out_shapes = [
        jax.ShapeDtypeStruct((MLA_HEADS, n, LANE), BF16),
        jax.ShapeDtypeStruct((MLA_HEADS, n, LANE), BF16),
        jax.ShapeDtypeStruct((MLA_HEADS, n, LANE), BF16),
        jax.ShapeDtypeStruct((GQA_HEADS, n, LANE), BF16),
        jax.ShapeDtypeStruct((n, LANE), BF16),
        jax.ShapeDtypeStruct((GQA_KV_HEADS, n, LANE), BF16),
        jax.ShapeDtypeStruct((n, POOL_WIDTH), F32),
    ]
    out_specs = [head_spec, head_spec, head_spec, head_spec,
                 pl.BlockSpec((tm, LANE), tile), kvh_spec,
                 pl.BlockSpec((tm, POOL_WIDTH), tile)]
    if not latent:
        out_names += ["ckv_out", "kr_out", "kg_out", "vg_out"]
        out_shapes += [jax.ShapeDtypeStruct((n, KV_LORA), F32), jax.ShapeDtypeStruct((n, MLA_ROPE), F32),
                       jax.ShapeDtypeStruct((n, LANE), F32), jax.ShapeDtypeStruct((n, LANE), F32)]
        out_specs += [pl.BlockSpec((tm, KV_LORA), tile), pl.BlockSpec((tm, MLA_ROPE), tile),
                      pl.BlockSpec((tm, LANE), tile), pl.BlockSpec((tm, LANE), tile)]

    outs = pl.pallas_call(
        functools.partial(_inproj_body, latent, names + out_names),
        grid=(nt,),
        in_specs=specs,
        out_specs=out_specs,
        out_shape=out_shapes,
        compiler_params=pltpu.CompilerParams(
            dimension_semantics=("arbitrary",), vmem_limit_bytes=VMEM_LIMIT),
        name="inproj_lat" if latent else "inproj_ctx",
    )(*args)
    return dict(zip(out_names, outs))


def _cachekv_body(ckv_ref, kr_ref, kg_ref, vg_ref, wk_ref, pk_ref, wv_ref, k_ref, v_ref, kgo_ref, vgo_ref):
    rows = ckv_ref.shape[0] * ckv_ref.shape[1]
    ckv_b = ckv_ref[...].reshape(rows, LANE).astype(BF16)
    lo = lax.broadcasted_iota(jnp.int32, (rows, LANE), 1) < MLA_V
    kfull = _dot(ckv_b, wk_ref[...]) + _dot(kr_ref[...].reshape(rows, LANE).astype(BF16), pk_ref[...])
    kgo_ref[0] = kg_ref[...].reshape(rows, LANE).astype(BF16)
    for hh in range(MLA_HEADS):
        k_ref[hh] = kfull[:, hh * LANE:(hh + 1) * LANE].astype(BF16)
    vfull = _dot(ckv_b, wv_ref[...])
    for pp in range(N_PAIRS):
        _store_value_heads(v_ref, 2 * pp, vfull[:, pp * LANE:(pp + 1) * LANE], lo)
    _store_value_heads(vgo_ref, 0, vg_ref[...].reshape(rows, LANE), lo)


def _cachekv_call(caches, layer, wl):
    bsz, _, past, _ = caches[0].shape
    n = bsz * past
    cache_spec = pl.BlockSpec((bsz, None, past, LANE), lambda i: (0, layer, 0, 0))
    return pl.pallas_call(
        _cachekv_body,
        grid=(1,),
        in_specs=[cache_spec] * 4 + [_wspec(wl["wk"], layer), _const_spec(wl["pk"].shape),
                                     _wspec(wl["wv"], layer)],
        out_specs=[_const_spec((MLA_HEADS, n, LANE)), _const_spec((MLA_HEADS, n, LANE)),
                   _const_spec((1, n, LANE)), _const_spec((GQA_KV_HEADS, n, LANE))],
        out_shape=[jax.ShapeDtypeStruct((MLA_HEADS, n, LANE), BF16),
                   jax.ShapeDtypeStruct((MLA_HEADS, n, LANE), BF16),
                   jax.ShapeDtypeStruct((1, n, LANE), BF16),
                   jax.ShapeDtypeStruct((GQA_KV_HEADS, n, LANE), BF16)],
        compiler_params=pltpu.CompilerParams(
            dimension_semantics=("arbitrary",), vmem_limit_bytes=VMEM_LIMIT),
        name="cache_kv",
    )(*caches, wl["wk"], wl["pk"], wl["wv"])


def _normalise(o_full):
    return o_full / pltpu.roll(o_full, MLA_V, axis=1)


def _attn_body(group, seq_len, shared_kv, exp_scale, q_ref, k_ref, v_ref, o_ref):
    lo = lax.broadcasted_iota(jnp.int32, (CTX_Q_ROWS, LANE), 1) < MLA_V
    nt_dims = (((1,), (1,)), ((), ()))
    for g in range(group):
        rows = slice(g * seq_len, (g + 1) * seq_len)
        for pr in range(N_PAIRS):
            for qb in range(seq_len // CTX_Q_ROWS):
                qrows = slice(g * seq_len + qb * CTX_Q_ROWS, g * seq_len + (qb + 1) * CTX_Q_ROWS)
                outs = []
                for e in range(2):
                    q = q_ref[2 * pr + e, qrows, :]
                    k = k_ref[0 if shared_kv else 2 * pr + e, rows, :]
                    v = v_ref[e if shared_kv else 2 * pr + e, rows, :]
                    s = lax.dot_general(q, k, nt_dims, preferred_element_type=F32)
                    m = jnp.max(s, axis=-1, keepdims=True)
                    p = jnp.exp2((s - m) * exp_scale)
                    outs.append(_normalise(_dot(p.astype(BF16), v)))
                o_ref[qrows, pr * LANE:(pr + 1) * LANE] = jnp.where(lo, outs[0], outs[1]).astype(BF16)


def _attn_call(q, k, v, batch, seq_len, group, scale, shared_kv, name):
    n = q.shape[1]
    rows = group * seq_len
    exp_scale = scale * math.log2(math.e)

    def spec(heads):
        return pl.BlockSpec((heads, rows, LANE), lambda b: (0, b, 0))

    return pl.pallas_call(
        functools.partial(_attn_body, group, seq_len, shared_kv, exp_scale),
        grid=(batch // group,),
        in_specs=[spec(q.shape[0]), spec(k.shape[0]), spec(v.shape[0])],
        out_specs=pl.BlockSpec((rows, N_PAIRS * LANE), lambda b: (b, 0)),
        out_shape=jax.ShapeDtypeStruct((n, N_PAIRS * LANE), BF16),
        compiler_params=pltpu.CompilerParams(
            dimension_semantics=("arbitrary",), vmem_limit_bytes=VMEM_LIMIT),
        name=name,
    )(q, k, v)


def _attn_pipe_body(shared_kv, exp_scale, tq, q_ref, k_ref, v_ref, kc_ref, vc_ref, o_ref,
                    s0, s1, mr0, mr1, ac0, ac1, os0):
    seq = k_ref.shape[1]
    past = kc_ref.shape[1]
    nq = seq // tq
    n_new = seq // KEY_TILE
    n_tiles = n_new + past // KEY_TILE
    s_scr = (s0, s1)
    mrun_scr = (mr0, mr1)
    acc_scr = (ac0, ac1)
    nt_dims = (((1,), (1,)), ((), ()))
    lo = lax.broadcasted_iota(jnp.int32, (tq, LANE), 1) < MLA_V

    def rows(j):
        return pl.ds(pl.multiple_of(j * tq, tq), tq)

    def kv_tile(new_ref, cache_ref, head, t):
        if t < n_new:
            return new_ref[head, t * KEY_TILE:(t + 1) * KEY_TILE, :]
        t -= n_new
        return cache_ref[head, t * KEY_TILE:(t + 1) * KEY_TILE, :]

    def finish_prev(e_prev, j_prev):
        o_prev = _normalise(acc_scr[e_prev][...])
        if e_prev == 0:
            os0[...] = o_prev
        else:
            o_ref[rows(j_prev), :] = jnp.where(lo, os0[...], o_prev).astype(BF16)

    def region(j_scores, e_scores, e_cur, j_prev):
        if j_prev is not None:
            finish_prev(1 - e_cur, j_prev)
        if e_scores is not None:
            q = q_ref[e_scores, rows(j_scores), :]
            k_head = 0 if shared_kv else e_scores
        if e_cur is not None:
            m = jnp.max(mrun_scr[e_cur][...], axis=-1, keepdims=True)
            m_b = jnp.broadcast_to(m, (tq, LANE))
        acc = None
        for t in range(n_tiles):
            cols = slice(t * KEY_TILE, (t + 1) * KEY_TILE)
            if e_scores is not None:
                s_t = lax.dot_general(q, kv_tile(k_ref, kc_ref, k_head, t), nt_dims,
                                      preferred_element_type=F32)
                s_scr[e_scores][:, cols] = s_t
                m_t = jnp.maximum(s_t[:, 0:LANE], s_t[:, LANE:2 * LANE])
                if t > 0:
                    m_t = jnp.maximum(m_t, mrun_scr[e_scores][...])
                mrun_scr[e_scores][...] = m_t
            if e_cur is not None:
                p_parts = []
                for hh in range(KEY_TILE // LANE):
                    c0 = t * KEY_TILE + hh * LANE
                    s_h = s_scr[e_cur][:, c0:c0 + LANE]
                    p_parts.append(jnp.exp2((s_h - m_b) * exp_scale).astype(BF16))
                p_t = jnp.concatenate(p_parts, axis=1)
                part = _dot(p_t, kv_tile(v_ref, vc_ref, e_cur, t))
                acc = part if acc is None else acc + part
        if e_cur is not None:
            acc_scr[e_cur][...] = acc

    region(0, 0, None, None)
    region(0, 1, 0, None)
    region(1, 0, 1, 0)

    def body(j, carry):
        @pl.when(j > 0)
        def _():
            region(j, 1, 0, j - 1)

        @pl.when(j < nq)
        def _():
            region(j + 1, 0, 1, j)

        return carry

    lax.fori_loop(1, nq - 1, body, 0)
    region(nq - 1, 1, 0, nq - 2)
    region(None, None, 1, nq - 1)
    finish_prev(1, nq - 1)


def _attn_pipe_call(q, k, v, kc, vc, batch, seq_len, tq, scale, shared_kv, name):
    n = q.shape[1]
    past = kc.shape[1] // batch
    exp_scale = scale * math.log2(math.e)

    def kv_spec(shared, rows):
        if shared:
            return pl.BlockSpec((1, rows, LANE), lambda b, p: (0, b, 0))
        return pl.BlockSpec((2, rows, LANE), lambda b, p: (p, b, 0))

    def v_spec(rows):
        if shared_kv:
            return pl.BlockSpec((2, rows, LANE), lambda b, p: (0, b, 0))
        return pl.BlockSpec((2, rows, LANE), lambda b, p: (p, b, 0))

    total = seq_len + past
    return pl.pallas_call(
        functools.partial(_attn_pipe_body, shared_kv, exp_scale, tq),
        grid=(batch, N_PAIRS),
        in_specs=[pl.BlockSpec((2, seq_len, LANE), lambda b, p: (p, b, 0)),
                  kv_spec(shared_kv, seq_len), v_spec(seq_len), kv_spec(shared_kv, past), v_spec(past)],
        out_specs=pl.BlockSpec((seq_len, LANE), lambda b, p: (b, p)),
        out_shape=jax.ShapeDtypeStruct((n, N_PAIRS * LANE), BF16),
        scratch_shapes=[pltpu.VMEM((tq, total), F32), pltpu.VMEM((tq, total), F32)]
        + [pltpu.VMEM((tq, LANE), F32) for _ in range(5)],
        compiler_params=pltpu.CompilerParams(
            dimension_semantics=("arbitrary", "arbitrary"), vmem_limit_bytes=VMEM_LIMIT),
        name=name,
    )(q, k, v, kc, vc)


def _seq_pos(tm, seq_len):
    i = pl.program_id(0)
    row = lax.broadcasted_iota(jnp.int32, (tm, 1), 0) + i * tm
    return jnp.bitwise_and(row, seq_len - 1)


def _merge_body(seq_len, x_ref, mod_ref, gmix_ref, wg_ref, a_ref, b_ref, pc_ref, pl_ref, pr_ref,
                wpool_ref, pscale_ref, wa_ref, wb_ref, wc_ref, wo_ref, o_ref):
    x = x_ref[...]
    tm = x.shape[0]
    mod = mod_ref[0]
    sh1 = mod[:, 0:D_MODEL]
    sc1 = mod[:, D_MODEL:2 * D_MODEL]
    g1 = mod[:, 2 * D_MODEL:3 * D_MODEL]
    h = (_rms(x, gmix_ref[...]) * (1.0 + sc1) + sh1).astype(BF16)
    gates = jax.nn.sigmoid(_dot(h, wg_ref[...]))

    pos = _seq_pos(tm, seq_len)
    ext = jnp.concatenate([pl_ref[...], pc_ref[...], pr_ref[...]], axis=0)
    pscale = pscale_ref[...]
    c_parts = []
    for g, w in enumerate(POOL_WINDOWS):
        eg = ext[:, g * POOL_GROUP:(g + 1) * POOL_GROUP]
        wsum = jnp.zeros((tm, POOL_GROUP), F32)
        cnt = jnp.zeros((tm, 1), F32)
        for dlt in range(-(w // 2), w - w // 2):
            valid = jnp.logical_and(pos + dlt >= 0, pos + dlt < seq_len)
            wsum = wsum + jnp.where(valid, eg[HALO + dlt:HALO + dlt + tm], 0.0)
            cnt = cnt + valid.astype(F32)
        pooled = (wsum / cnt - eg[HALO:HALO + tm]).astype(BF16)
        c_parts.append(_dot(pooled, wpool_ref[g]) * pscale[:, g * POOL_GROUP:(g + 1) * POOL_GROUP])
    c_out = jnp.concatenate(c_parts, axis=-1).astype(BF16)

    merged = (gates[:, 0:D_MODEL] * _dot(a_ref[...], wa_ref[...])
              + gates[:, D_MODEL:2 * D_MODEL] * _dot(b_ref[...], wb_ref[...])
              + gates[:, 2 * D_MODEL:3 * D_MODEL] * _dot(c_out, wc_ref[...]))
    o_ref[...] = x + g1 * _dot(merged.astype(BF16), wo_ref[...])


def _halo_specs(tm, width, n):
    blocks = tm // HALO
    last = n // HALO - 1
    left = pl.BlockSpec((HALO, width), lambda i: (jnp.maximum(i * blocks - 1, 0), 0))
    right = pl.BlockSpec((HALO, width), lambda i: (jnp.minimum((i + 1) * blocks, last), 0))
    return left, right


def _mod_spec(layer, mod_base, per_seq, latent):
    if latent:
        return pl.BlockSpec((None, 1, 1, 6 * D_MODEL), lambda i: (layer, mod_base + i // per_seq, 0, 0))
    return pl.BlockSpec((None, 1, 1, 6 * D_MODEL), lambda i: (layer, mod_base, 0, 0))


def _merge_call(x, mod4, layer, mod_base, seq_len, tm, latent, a_out, b_out, pool_in, wl):
    n = x.shape[0]
    per_seq = seq_len // tm if latent else 1

    def tile(i):
        return (i, 0)

    left, right = _halo_specs(tm, POOL_WIDTH, n)
    consts = [wl["wpool"], wl["pscale"], wl["wa"], wl["wb"], wl["wc"], wl["wo"]]
    in_specs = [pl.BlockSpec((tm, D_MODEL), tile), _mod_spec(layer, mod_base, per_seq, latent),
                _wspec(wl["gmix"], layer), _wspec(wl["wg"], layer),
                pl.BlockSpec((tm, N_PAIRS * LANE), tile), pl.BlockSpec((tm, N_PAIRS * LANE), tile),
                pl.BlockSpec((tm, POOL_WIDTH), tile), left, right]
    in_specs += [_wspec(a, layer) for a in consts]
    return pl.pallas_call(
        functools.partial(_merge_body, seq_len),
        grid=(n // tm,),
        in_specs=in_specs,
        out_specs=pl.BlockSpec((tm, D_MODEL), tile),
        out_shape=jax.ShapeDtypeStruct((n, D_MODEL), F32),
        compiler_params=pltpu.CompilerParams(
            dimension_semantics=("arbitrary",), vmem_limit_bytes=VMEM_LIMIT),
        name="merge_lat" if latent else "merge_ctx",
    )(x, mod4, wl["gmix"], wl["wg"], a_out, b_out, pool_in, pool_in, pool_in, *consts)


def _ffn_body(seq_len, final, x_ref, xl_ref, xr_ref, mod_ref, gffn_ref, wug_ref, wuv_ref, cw_ref, cb_ref,
              wd_ref, gfin_ref, o_ref):
    x = x_ref[...]
    tm = x.shape[0]
    mod = mod_ref[0]
    sh2 = mod[:, 3 * D_MODEL:4 * D_MODEL]
    sc2 = mod[:, 4 * D_MODEL:5 * D_MODEL]
    g2 = mod[:, 5 * D_MODEL:6 * D_MODEL]
    xe = jnp.concatenate([xl_ref[...], x, xr_ref[...]], axis=0)
    h2e = _rms(xe, gffn_ref[...]) * (1.0 + sc2) + sh2
    h2 = h2e[HALO:HALO + tm].astype(BF16)
    h2e = h2e.astype(BF16)

    pos = _seq_pos(tm, seq_len)
    has_prev = pos >= 1
    has_next = pos <= seq_len - 2
    cw = cw_ref[...]
    cb = cb_ref[...]
    acc = jnp.zeros((tm, D_MODEL), F32)
    for j in range(D_FF // FF_CHUNK):
        cols = slice(j * FF_CHUNK, (j + 1) * FF_CHUNK)
        ge = _dot(h2e, wug_ref[:, cols])
        g = (jnp.where(has_prev, ge[HALO - 1:HALO - 1 + tm], 0.0) * cw[0:1, cols]
             + ge[HALO:HALO + tm] * cw[1:2, cols]
             + jnp.where(has_next, ge[HALO + 1:HALO + 1 + tm], 0.0) * cw[2:3, cols]
             + cb[:, cols])
        val = _dot(h2, wuv_ref[:, cols])
        act = (g * jax.nn.sigmoid(g) * val).astype(BF16)
        acc = acc + _dot(act, wd_ref[cols, :])
    y = x + g2 * acc
    if final:
        y = _rms(y, gfin_ref[...])
    o_ref[...] = y


def _ffn_call(x, mod4, layer, mod_base, seq_len, tm, latent, final, wl, g_final):
    n = x.shape[0]
    per_seq = seq_len // tm if latent else 1

    def tile(i):
        return (i, 0)

    left, right = _halo_specs(tm, D_MODEL, n)
    consts = [wl["gffn"], wl["wu"], wl["wu"], wl["cw"], wl["cb"], wl["wd"], g_final]
    in_specs = [pl.BlockSpec((tm, D_MODEL), tile), left, right, _mod_spec(layer, mod_base, per_seq, latent),
                _wspec(wl["gffn"], layer), _wspec(wl["wu"], layer, cols=D_FF, col_block=0),
                _wspec(wl["wu"], layer, cols=D_FF, col_block=1), _wspec(wl["cw"], layer),
                _wspec(wl["cb"], layer), _wspec(wl["wd"], layer), _const_spec(g_final.shape)]
    return pl.pallas_call(
        functools.partial(_ffn_body, seq_len, final),
        grid=(n // tm,),
        in_specs=in_specs,
        out_specs=pl.BlockSpec((tm, D_MODEL), tile),
        out_shape=jax.ShapeDtypeStruct((n, D_MODEL), F32),
        compiler_params=pltpu.CompilerParams(
            dimension_semantics=("arbitrary",), vmem_limit_bytes=VMEM_LIMIT),
        name="ffn_lat" if latent else "ffn_ctx",
    )(x, x, x, mod4, *consts)


def _rot_partner(a):
    d = a.shape[-1]
    return jnp.flip(a.reshape(a.shape[:-1] + (2, 2, d // 4)), axis=-2).reshape(a.shape)


def _prep_weights(g_norm_mix, w_in, g_q_a, w_q_b, g_kv_a, w_kv_b, g_q_gqa, g_k_gqa, w_pool, pool_scale,
                  w_br_a, w_br_b, w_br_c, w_out, g_norm_ffn, w_up, conv_w, conv_b, w_down):
    depth = w_in.shape[0]
    hd = GQA_HEAD_DIM
    grp = GQA_HEADS // GQA_KV_HEADS

    def lane_pad(a, before, width=LANE):
        cfg = [(0, 0)] * (a.ndim - 1) + [(before, width - before - a.shape[-1])]
        return jnp.pad(a, cfg)

    def pair_heads(a):
        lead = a.shape[:-1]
        return a.reshape(lead + (GQA_KV_HEADS, grp, hd)).swapaxes(-3, -2).reshape(lead + (GQA_HEADS * hd,))

    def per_head_partner(a, heads, dim):
        lead = a.shape[:-1]
        return _rot_partner(a.reshape(lead + (heads, dim))).reshape(lead + (heads * dim,))

    gq_w = pair_heads(w_in[:, :, _OFF_GQ:_OFF_GK])
    gk_w = w_in[:, :, _OFF_GK:_OFF_GV]
    kr_w = w_in[:, :, _OFF_KR:_OFF_GQ]
    w1 = jnp.concatenate(
        [w_in[:, :, _OFF_QA:_OFF_KR], gq_w, gk_w, w_in[:, :, _OFF_GV:_OFF_GATE], lane_pad(kr_w, MLA_NOPE),
         per_head_partner(gq_w, GQA_HEADS, hd), per_head_partner(gk_w, GQA_KV_HEADS, hd),
         lane_pad(_rot_partner(kr_w), MLA_NOPE)], axis=-1).astype(BF16)

    qb = w_q_b.reshape(depth, Q_LORA, MLA_HEADS, MLA_NOPE + MLA_ROPE)
    wqb = lane_pad(qb, 0).reshape(depth, Q_LORA, MLA_HEADS * LANE).astype(BF16)
    wqbsw = lane_pad(_rot_partner(qb[..., MLA_NOPE:]), MLA_NOPE).reshape(
        depth, Q_LORA, MLA_HEADS * LANE).astype(BF16)

    kvb = w_kv_b.reshape(depth, KV_LORA, MLA_HEADS, MLA_NOPE + MLA_V)
    wk = lane_pad(kvb[..., :MLA_NOPE], 0).reshape(depth, KV_LORA, MLA_HEADS * LANE).astype(BF16)
    wv = kvb[..., MLA_NOPE:].reshape(depth, KV_LORA, MLA_HEADS * MLA_V).astype(BF16)

    pk = np.zeros((LANE, MLA_HEADS * LANE), np.float32)
    for hh in range(MLA_HEADS):
        for t in range(MLA_ROPE):
            pk[MLA_NOPE + t, hh * LANE + MLA_NOPE + t] = 1.0

    row = lambda a: a[:, None, :]
    two = lambda a: jnp.concatenate([a, a], axis=-1)
    wb = w_br_b.reshape(depth, GQA_KV_HEADS, grp, hd, D_MODEL).swapaxes(1, 2).reshape(depth, GQA_HEADS * hd, D_MODEL)
    return dict(
        gmix=row(g_norm_mix), w1=w1, gqa=row(g_q_a), wqb=wqb, wqbsw=wqbsw, gkva=row(g_kv_a), wk=wk,
        pk=jnp.asarray(pk, BF16), wv=wv, gq2=row(two(g_q_gqa)), gq2sw=row(two(_rot_partner(g_q_gqa))),
        gk2=row(two(g_k_gqa)), gk2sw=row(two(_rot_partner(g_k_gqa))),
        wg=w_in[:, :, _OFF_GATE:].astype(BF16), wpool=w_pool.astype(BF16), pscale=row(pool_scale),
        wa=w_br_a.astype(BF16), wb=wb.astype(BF16), wc=w_br_c.astype(BF16), wo=w_out.astype(BF16),
        gffn=row(g_norm_ffn), wu=w_up.astype(BF16), cw=conv_w, cb=row(conv_b), wd=w_down.astype(BF16),
    )


def _layer(x, mod4, layer, mod_base, batch, seq_len, latent, final, wl, g_final, tabs, caches):
    tm = TOKEN_TILE
    tq = Q_TILE
    assert tm % seq_len == 0 or seq_len % tm == 0
    pj = _inproj_call(x, mod4, layer, mod_base, seq_len, tm, latent, wl, tabs)
    if caches is not None:
        kc_m, vc_m, kc_g, vc_g = _cachekv_call(caches, layer, wl)
    mla_scale = (MLA_NOPE + MLA_ROPE) ** -0.5
    n = x.shape[0]
    k_g = pj["k_gqa"].reshape(1, n, LANE)
    if latent:
        a_out = _attn_pipe_call(pj["q_mla"], pj["k_mla"], pj["v_mla"], kc_m, vc_m, batch, seq_len, tq,
                                mla_scale, False, "attn_mla_lat")
        b_out = _attn_pipe_call(pj["q_gqa"], k_g, pj["v_gqa"], kc_g, vc_g, batch, seq_len, tq, 1.0, True,
                                "attn_gqa_lat")
    else:
        a_out = _attn_call(pj["q_mla"], pj["k_mla"], pj["v_mla"], batch, seq_len, CTX_GROUP,
                           mla_scale, False, "attn_mla_ctx")
        b_out = _attn_call(pj["q_gqa"], k_g, pj["v_gqa"], batch, seq_len, CTX_GROUP, 1.0, True,
                           "attn_gqa_ctx")
    x1 = _merge_call(x, mod4, layer, mod_base, seq_len, tm, latent, a_out, b_out, pj["pool"], wl)
    x2 = _ffn_call(x1, mod4, layer, mod_base, seq_len, tm, latent, final, wl, g_final)
    return x2, pj


def kernel(x_prompt, x_sample, c, cache_mla_ckv, cache_mla_krope, cache_gqa_k, cache_gqa_v, c_ctx, w_ada, b_ada, g_norm_mix, w_in, g_q_a, w_q_b, g_kv_a, w_kv_b, g_q_gqa, g_k_gqa, w_pool, pool_scale, w_br_a, w_br_b, w_br_c, w_out, g_norm_ffn, w_up, conv_w, conv_b, w_down, g_final):
    depth = w_in.shape[0]
    bc, tc, _ = x_prompt.shape
    bl, tl, _ = x_sample.shape
    past = cache_mla_ckv.shape[2]
    mod_rows = 8
    assert 1 + bl <= mod_rows and tc & (tc - 1) == 0 and tl & (tl - 1) == 0

    cin = jnp.concatenate([c_ctx[None, :], c, jnp.zeros((mod_rows - 1 - bl, D_MODEL), F32)], axis=0)
    mod4 = _ada_call(cin, w_ada, b_ada).reshape(depth, mod_rows, 1, 6 * D_MODEL)
    tabs = _rope_tables(tl)
    g_fin = g_final.reshape(1, D_MODEL)
    wl = _prep_weights(g_norm_mix, w_in, g_q_a, w_q_b, g_kv_a, w_kv_b, g_q_gqa, g_k_gqa, w_pool,
                       pool_scale, w_br_a, w_br_b, w_br_c, w_out, g_norm_ffn, w_up, conv_w, conv_b, w_down)
    kr_pad = jnp.pad(cache_mla_krope, ((0, 0), (0, 0), (0, 0), (MLA_NOPE, LANE - MLA_NOPE - MLA_ROPE)))
    caches = (cache_mla_ckv, kr_pad, cache_gqa_k.reshape(bl, depth, past, LANE),
              cache_gqa_v.reshape(bl, depth, past, LANE))

    xc = x_prompt.reshape(bc * tc, D_MODEL)
    xl = x_sample.reshape(bl * tl, D_MODEL)
    st_ckv, st_kr, st_k, st_v = [], [], [], []
    for l in range(depth):
        final = l == depth - 1
        xc, pj = _layer(xc, mod4, l, 0, bc, tc, False, final, wl, g_fin, None, None)
        st_ckv.append(pj["ckv_out"].reshape(bc, tc, KV_LORA))
        st_kr.append(pj["kr_out"].reshape(bc, tc, MLA_ROPE))
        st_k.append(pj["kg_out"].reshape(bc, tc, GQA_KV_HEADS, GQA_HEAD_DIM))
        st_v.append(pj["vg_out"].reshape(bc, tc, GQA_KV_HEADS, GQA_HEAD_DIM))
        xl, _ = _layer(xl, mod4, l, 1, bl, tl, True, final, wl, g_fin, tabs, caches)

    return (xc.reshape(bc, tc, D_MODEL), xl.reshape(bl, tl, D_MODEL),
            jnp.stack(st_ckv, axis=1), jnp.stack(st_kr, axis=1),
            jnp.stack(st_k, axis=1), jnp.stack(st_v, axis=1))
```

```python
import functools
import math

import numpy as np
import jax
import jax.numpy as jnp
from jax import lax
from jax.experimental import pallas as pl
from jax.experimental.pallas import tpu as pltpu

D_MODEL = 1024
GRID_W = 64
RMS_EPS = 1e-6
ROPE_THETA = 10000.0
MLA_HEADS = 8
MLA_NOPE = 64
MLA_ROPE = 32
MLA_V = 64
Q_LORA = 256
KV_LORA = 128
GQA_HEADS = 8
GQA_KV_HEADS = 2
GQA_HEAD_DIM = 64
POOL_WINDOWS = (2, 4, 8, 16)
POOL_GROUP = 128
POOL_WIDTH = POOL_GROUP * len(POOL_WINDOWS)
D_FF = 2816
N_BRANCH = 3

LANE = 128
HALO = 8
FF_CHUNK = 256
KEY_TILE = 256
CTX_GROUP = 1
CTX_Q_ROWS = 256
TOKEN_TILE = 512
Q_TILE = 256
N_PAIRS = MLA_HEADS // 2
VMEM_LIMIT = 56 * 1024 * 1024

BF16 = jnp.bfloat16
F32 = jnp.float32

_OFF_QA = 0
_OFF_CKV = _OFF_QA + Q_LORA
_OFF_KR = _OFF_CKV + KV_LORA
_OFF_GQ = _OFF_KR + MLA_ROPE
_OFF_GK = _OFF_GQ + GQA_HEADS * GQA_HEAD_DIM
_OFF_GV = _OFF_GK + GQA_KV_HEADS * GQA_HEAD_DIM
_OFF_POOL = _OFF_GV + GQA_KV_HEADS * GQA_HEAD_DIM
_OFF_GATE = _OFF_POOL + POOL_WIDTH

_W1_QA = 0
_W1_CKV = 256
_W1_GQ = 384
_W1_GK = 896
_W1_GV = 1024
_W1_POOL = 1152
_W1_KR = 1664
_W1_CTX_COLS = 1792
_W1_GQ_SW = 1792
_W1_GK_SW = 2304
_W1_KR_SW = 2432
_W1_LAT_COLS = 2560


def _const_spec(shape):
    nd = len(shape)
    return pl.BlockSpec(shape, lambda *_: (0,) * nd, pipeline_mode=pl.Buffered(1))


def _wspec(a, layer, cols=None, col_block=0):
    shape = a.shape[1:] if cols is None else a.shape[1:-1] + (cols,)
    idx = (layer,) + (0,) * (len(shape) - 1) + (col_block,)
    return pl.BlockSpec((None,) + shape, lambda *_: idx, pipeline_mode=pl.Buffered(1))


def _rms(x, g):
    ms = jnp.mean(x * x, axis=-1, keepdims=True)
    return x * lax.rsqrt(ms + RMS_EPS) * g


def _dot(a, b):
    return jnp.dot(a, b, preferred_element_type=F32)


def _rope_head_tables(n_tokens, d):
    dim_axis = d // 2
    t = np.arange(n_tokens)
    row = (t // GRID_W).astype(np.float64)
    col = (t % GRID_W).astype(np.float64)
    freqs = ROPE_THETA ** (-np.arange(0, dim_axis, 2, dtype=np.float64) / dim_axis)
    ar = row[:, None] * freqs[None, :]
    ac = col[:, None] * freqs[None, :]
    cos = np.concatenate([np.cos(ar), np.cos(ar), np.cos(ac), np.cos(ac)], axis=1)
    sin = np.concatenate([-np.sin(ar), np.sin(ar), -np.sin(ac), np.sin(ac)], axis=1)
    return cos, sin


def _rope_tables(n_tokens):
    cg, sg = _rope_head_tables(n_tokens, GQA_HEAD_DIM)
    cg = np.concatenate([cg, cg], axis=1)
    sg = np.concatenate([sg, sg], axis=1)
    cm32, sm32 = _rope_head_tables(n_tokens, MLA_ROPE)
    pad = LANE - MLA_NOPE - MLA_ROPE
    cm = np.concatenate([np.ones((n_tokens, MLA_NOPE)), cm32, np.zeros((n_tokens, pad))], axis=1)
    sm = np.concatenate([np.zeros((n_tokens, MLA_NOPE)), sm32, np.zeros((n_tokens, pad))], axis=1)
    return tuple(jnp.asarray(a, F32) for a in (cg, sg, cm, sm))


def _ada_body(c_ref, w_ref, b_ref, o_ref):
    c = c_ref[...]
    s = (c * jax.nn.sigmoid(c)).astype(BF16)
    o_ref[0] = _dot(s, w_ref[0].astype(BF16)) + b_ref[0]


def _ada_call(cin, w_ada, b_ada):
    depth, d, n = w_ada.shape
    cols = 2048
    rows = cin.shape[0]
    return pl.pallas_call(
        _ada_body,
        grid=(depth, n // cols),
        in_specs=[
            pl.BlockSpec((rows, d), lambda l, j: (0, 0)),
            pl.BlockSpec((1, d, cols), lambda l, j: (l, 0, j)),
            pl.BlockSpec((1, 1, cols), lambda l, j: (l, 0, j)),
        ],
        out_specs=pl.BlockSpec((1, rows, cols), lambda l, j: (l, 0, j)),
        out_shape=jax.ShapeDtypeStruct((depth, rows, n), F32),
        compiler_params=pltpu.CompilerParams(
            dimension_semantics=("arbitrary", "arbitrary"), vmem_limit_bytes=VMEM_LIMIT),
        name="ada_mod",
    )(cin, w_ada, b_ada.reshape(depth, 1, n))


def _two_head_rsqrt(xb, lo):
    x2 = xb * xb
    s_lo = jnp.sum(jnp.where(lo, x2, 0.0), axis=-1, keepdims=True)
    s_hi = jnp.sum(jnp.where(lo, 0.0, x2), axis=-1, keepdims=True)
    ms = jnp.where(lo, s_lo, s_hi) * (1.0 / GQA_HEAD_DIM)
    return lax.rsqrt(ms + RMS_EPS)


def _store_value_heads(v_ref, first, blk, lo):
    v_ref[first] = jnp.where(lo, blk, 1.0).astype(BF16)
    v_ref[first + 1] = jnp.where(lo, 1.0, blk).astype(BF16)


def _inproj_body(latent, names, *refs):
    r = dict(zip(names, refs))
    x = r["x"][...]
    tm = x.shape[0]
    mod = r["mod"][0]
    sh1 = mod[:, 0:D_MODEL]
    sc1 = mod[:, D_MODEL:2 * D_MODEL]
    h = (_rms(x, r["gmix"][...]) * (1.0 + sc1) + sh1).astype(BF16)
    z = _dot(h, r["w1"][...])

    lane = lax.broadcasted_iota(jnp.int32, (tm, LANE), 1)
    lo = lane < GQA_HEAD_DIM

    if latent:
        cg = r["cg"][...]
        sg = r["sg"][...]
        cm = r["cm"][...]
        sm = r["sm"][...]

    qn = _rms(z[:, _W1_QA:_W1_QA + Q_LORA], r["gqa"][...]).astype(BF16)
    qm = _dot(qn, r["wqb"][...])
    if latent:
        qms = _dot(qn, r["wqbsw"][...])
    for hh in range(MLA_HEADS):
        blk = qm[:, hh * LANE:(hh + 1) * LANE]
        if latent:
            blk = blk * cm + qms[:, hh * LANE:(hh + 1) * LANE] * sm
        r["q_mla"][hh] = blk.astype(BF16)

    ckv = _rms(z[:, _W1_CKV:_W1_CKV + KV_LORA], r["gkva"][...])
    kr = z[:, _W1_KR:_W1_KR + LANE]
    if latent:
        kr = kr * cm + z[:, _W1_KR_SW:_W1_KR_SW + LANE] * sm
    else:
        r["ckv_out"][...] = ckv
        r["kr_out"][...] = kr[:, MLA_NOPE:MLA_NOPE + MLA_ROPE]
    ckv_b = ckv.astype(BF16)
    kfull = _dot(ckv_b, r["wk"][...]) + _dot(kr.astype(BF16), r["pk"][...])
    for hh in range(MLA_HEADS):
        r["k_mla"][hh] = kfull[:, hh * LANE:(hh + 1) * LANE].astype(BF16)
    vfull = _dot(ckv_b, r["wv"][...])
    for pp in range(N_PAIRS):
        _store_value_heads(r["v_mla"], 2 * pp, vfull[:, pp * LANE:(pp + 1) * LANE], lo)

    gq2 = r["gq2"][...]
    q_scale = GQA_HEAD_DIM ** -0.5
    for j in range(N_PAIRS):
        xb = z[:, _W1_GQ + j * LANE:_W1_GQ + (j + 1) * LANE]
        rs = _two_head_rsqrt(xb, lo)
        y = xb * rs * gq2
        if latent:
            xs = z[:, _W1_GQ_SW + j * LANE:_W1_GQ_SW + (j + 1) * LANE]
            y = y * cg + (xs * rs * r["gq2sw"][...]) * sg
        y = y * q_scale
        r["q_gqa"][2 * j] = jnp.where(lo, y, 0.0).astype(BF16)
        r["q_gqa"][2 * j + 1] = jnp.where(lo, 0.0, y).astype(BF16)
    kb = z[:, _W1_GK:_W1_GK + LANE]
    rs = _two_head_rsqrt(kb, lo)
    kg = kb * rs * r["gk2"][...]
    if latent:
        ks = z[:, _W1_GK_SW:_W1_GK_SW + LANE]
        kg = kg * cg + (ks * rs * r["gk2sw"][...]) * sg
    vg = z[:, _W1_GV:_W1_GV + LANE]
    if not latent:
        r["kg_out"][...] = kg
        r["vg_out"][...] = vg
    r["k_gqa"][...] = kg.astype(BF16)
    _store_value_heads(r["v_gqa"], 0, vg, lo)

    r["pool"][...] = z[:, _W1_POOL:_W1_POOL + POOL_WIDTH]


def _inproj_call(x, mod4, layer, mod_base, seq_len, tm, latent, wl, tabs):
    n = x.shape[0]
    nt = n // tm
    per_seq = seq_len // tm if latent else 1

    def tile(i):
        return (i, 0)

    def mod_idx(i):
        if latent:
            return (layer, mod_base + i // per_seq, 0, 0)
        return (layer, mod_base, 0, 0)

    names = ["x", "mod", "w1", "pk", "gmix", "gqa", "wqb", "gkva", "wk", "wv", "gq2", "gk2"]
    args = [x, mod4, wl["w1"], wl["pk"]] + [wl[nm] for nm in names[4:]]
    specs = [pl.BlockSpec((tm, D_MODEL), tile), pl.BlockSpec((None, 1, 1, 6 * D_MODEL), mod_idx),
             _wspec(wl["w1"], layer, cols=_W1_LAT_COLS if latent else _W1_CTX_COLS),
             _const_spec(wl["pk"].shape)]
    specs += [_wspec(a, layer) for a in args[4:]]
    if latent:
        for nm in ("wqbsw", "gq2sw", "gk2sw"):
            names.append(nm)
            args.append(wl[nm])
            specs.append(_wspec(wl[nm], layer))
        for nm, a in zip(("cg", "sg", "cm", "sm"), tabs):
            names.append(nm)
            args.append(a)
            specs.append(pl.BlockSpec((tm, LANE), lambda i: (i % per_seq, 0)))

    head_spec = pl.BlockSpec((MLA_HEADS, tm, LANE), lambda i: (0, i, 0))
    kvh_spec = pl.BlockSpec((GQA_KV_HEADS, tm, LANE), lambda i: (0, i, 0))
    out_names = ["q_mla", "k_mla", "v_mla", "q_gqa", "k_gqa", "v_gqa", "pool"]
    out_shapes = [
        jax.ShapeDtypeStruct((MLA_HEADS, n, LANE), BF16),
        jax.ShapeDtypeStruct((MLA_HEADS, n, LANE), BF16),
        jax.ShapeDtypeStruct((MLA_HEADS, n, LANE), BF16),
        jax.ShapeDtypeStruct((GQA_HEADS, n, LANE), BF16),
        jax.ShapeDtypeStruct((n, LANE), BF16),
        jax.ShapeDtypeStruct((GQA_KV_HEADS, n, LANE), BF16),
        jax.ShapeDtypeStruct((n, POOL_WIDTH), F32),
    ]
    out_specs = [head_spec, head_spec, head_spec, head_spec,
                 pl.BlockSpec((tm, LANE), tile), kvh_spec,
                 pl.BlockSpec((tm, POOL_WIDTH), tile)]
    if not latent:
        out_names += ["ckv_out", "kr_out", "kg_out", "vg_out"]
        out_shapes += [jax.ShapeDtypeStruct((n, KV_LORA), F32), jax.ShapeDtypeStruct((n, MLA_ROPE), F32),
                       jax.ShapeDtypeStruct((n, LANE), F32), jax.ShapeDtypeStruct((n, LANE), F32)]
        out_specs += [pl.BlockSpec((tm, KV_LORA), tile), pl.BlockSpec((tm, MLA_ROPE), tile),
                      pl.BlockSpec((tm, LANE), tile), pl.BlockSpec((tm, LANE), tile)]

    outs = pl.pallas_call(
        functools.partial(_inproj_body, latent, names + out_names),
        grid=(nt,),
        in_specs=specs,
        out_specs=out_specs,
        out_shape=out_shapes,
        compiler_params=pltpu.CompilerParams(
            dimension_semantics=("arbitrary",), vmem_limit_bytes=VMEM_LIMIT),
        name="inproj_lat" if latent else "inproj_ctx",
    )(*args)
    return dict(zip(out_names, outs))


def _cachekv_body(ckv_ref, kr_ref, kg_ref, vg_ref, wk_ref, pk_ref, wv_ref, k_ref, v_ref, kgo_ref, vgo_ref):
    rows = ckv_ref.shape[0] * ckv_ref.shape[1]
    ckv_b = ckv_ref[...].reshape(rows, LANE).astype(BF16)
    lo = lax.broadcasted_iota(jnp.int32, (rows, LANE), 1) < MLA_V
    kfull = _dot(ckv_b, wk_ref[...]) + _dot(kr_ref[...].reshape(rows, LANE).astype(BF16), pk_ref[...])
    kgo_ref[0] = kg_ref[...].reshape(rows, LANE).astype(BF16)
    for hh in range(MLA_HEADS):
        k_ref[hh] = kfull[:, hh * LANE:(hh + 1) * LANE].astype(BF16)
    vfull = _dot(ckv_b, wv_ref[...])
    for pp in range(N_PAIRS):
        _store_value_heads(v_ref, 2 * pp, vfull[:, pp * LANE:(pp + 1) * LANE], lo)
    _store_value_heads(vgo_ref, 0, vg_ref[...].reshape(rows, LANE), lo)


def _cachekv_call(caches, layer, wl):
    bsz, _, past, _ = caches[0].shape
    n = bsz * past
    cache_spec = pl.BlockSpec((bsz, None, past, LANE), lambda i: (0, layer, 0, 0))
    return pl.pallas_call(
        _cachekv_body,
        grid=(1,),
        in_specs=[cache_spec] * 4 + [_wspec(wl["wk"], layer), _const_spec(wl["pk"].shape),
                                     _wspec(wl["wv"], layer)],
        out_specs=[_const_spec((MLA_HEADS, n, LANE)), _const_spec((MLA_HEADS, n, LANE)),
                   _const_spec((1, n, LANE)), _const_spec((GQA_KV_HEADS, n, LANE))],
        out_shape=[jax.ShapeDtypeStruct((MLA_HEADS, n, LANE), BF16),
                   jax.ShapeDtypeStruct((MLA_HEADS, n, LANE), BF16),
                   jax.ShapeDtypeStruct((1, n, LANE), BF16),
                   jax.ShapeDtypeStruct((GQA_KV_HEADS, n, LANE), BF16)],
        compiler_params=pltpu.CompilerParams(
            dimension_semantics=("arbitrary",), vmem_limit_bytes=VMEM_LIMIT),
        name="cache_kv",
    )(*caches, wl["wk"], wl["pk"], wl["wv"])


def _normalise(o_full):
    return o_full / pltpu.roll(o_full, MLA_V, axis=1)


def _attn_body(group, seq_len, shared_kv, exp_scale, q_ref, k_ref, v_ref, o_ref):
    lo = lax.broadcasted_iota(jnp.int32, (CTX_Q_ROWS, LANE), 1) < MLA_V
    nt_dims = (((1,), (1,)), ((), ()))
    for g in range(group):
        rows = slice(g * seq_len, (g + 1) * seq_len)
        for pr in range(N_PAIRS):
            for qb in range(seq_len // CTX_Q_ROWS):
                qrows = slice(g * seq_len + qb * CTX_Q_ROWS, g * seq_len + (qb + 1) * CTX_Q_ROWS)
                outs = []
                for e in range(2):
                    q = q_ref[2 * pr + e, qrows, :]
                    k = k_ref[0 if shared_kv else 2 * pr + e, rows, :]
                    v = v_ref[e if shared_kv else 2 * pr + e, rows, :]
                    s = lax.dot_general(q, k, nt_dims, preferred_element_type=F32)
                    m = jnp.max(s, axis=-1, keepdims=True)
                    p = jnp.exp2((s - m) * exp_scale)
                    outs.append(_normalise(_dot(p.astype(BF16), v)))
                o_ref[pr, qrows, :] = jnp.where(lo, outs[0], outs[1]).astype(BF16)


def _attn_call(q, k, v, batch, seq_len, group, scale, shared_kv, name):
    n = q.shape[1]
    rows = group * seq_len
    exp_scale = scale * math.log2(math.e)

    def spec(heads):
        return pl.BlockSpec((heads, rows, LANE), lambda b: (0, b, 0))

    return pl.pallas_call(
        functools.partial(_attn_body, group, seq_len, shared_kv, exp_scale),
        grid=(batch // group,),
        in_specs=[spec(q.shape[0]), spec(k.shape[0]), spec(v.shape[0])],
        out_specs=pl.BlockSpec((N_PAIRS, rows, LANE), lambda b: (0, b, 0)),
        out_shape=jax.ShapeDtypeStruct((N_PAIRS, n, LANE), BF16),
        compiler_params=pltpu.CompilerParams(
            dimension_semantics=("arbitrary",), vmem_limit_bytes=VMEM_LIMIT),
        name=name,
    )(q, k, v)


def _attn_pipe_body(shared_kv, exp_scale, tq, q_ref, k_ref, v_ref, kc_ref, vc_ref, o_ref,
                    s0, s1, mr0, mr1, ac0, ac1, os0):
    seq = k_ref.shape[1]
    past = kc_ref.shape[1]
    nq = seq // tq
    n_units = N_PAIRS * nq
    n_new = seq // KEY_TILE
    n_tiles = n_new + past // KEY_TILE
    s_scr = (s0, s1)
    mrun_scr = (mr0, mr1)
    acc_scr = (ac0, ac1)
    nt_dims = (((1,), (1,)), ((), ()))
    lo = lax.broadcasted_iota(jnp.int32, (tq, LANE), 1) < MLA_V

    def pair_rows(t):
        j = t % nq
        return t // nq, pl.ds(pl.multiple_of(j * tq, tq), tq)

    def kv_tile(new_ref, cache_ref, head, blk):
        if blk < n_new:
            return new_ref[head, blk * KEY_TILE:(blk + 1) * KEY_TILE, :]
        blk -= n_new
        return cache_ref[head, blk * KEY_TILE:(blk + 1) * KEY_TILE, :]

    def finish_prev(e_prev, t_prev):
        o_prev = _normalise(acc_scr[e_prev][...])
        if e_prev == 0:
            os0[...] = o_prev
        else:
            pair, rows = pair_rows(t_prev)
            o_ref[pair, rows, :] = jnp.where(lo, os0[...], o_prev).astype(BF16)

    def region(t_scores, e_scores, t_cur, e_cur, t_prev):
        if t_prev is not None:
            finish_prev(1 - e_cur, t_prev)
        if e_scores is not None:
            pair, rows = pair_rows(t_scores)
            q = q_ref[2 * pair + e_scores, rows, :]
            k_head = 0 if shared_kv else 2 * pair + e_scores
        if e_cur is not None:
            v_head = e_cur if shared_kv else 2 * (t_cur // nq) + e_cur
            m = jnp.max(mrun_scr[e_cur][...], axis=-1, keepdims=True)
            m_b = jnp.broadcast_to(m, (tq, LANE))
        acc = None
        for blk in range(n_tiles):
            cols = slice(blk * KEY_TILE, (blk + 1) * KEY_TILE)
            if e_scores is not None:
                s_t = lax.dot_general(q, kv_tile(k_ref, kc_ref, k_head, blk), nt_dims,
                                      preferred_element_type=F32)
                s_scr[e_scores][:, cols] = s_t
                m_t = jnp.maximum(s_t[:, 0:LANE], s_t[:, LANE:2 * LANE])
                if blk > 0:
                    m_t = jnp.maximum(m_t, mrun_scr[e_scores][...])
                mrun_scr[e_scores][...] = m_t
            if e_cur is not None:
                p_parts = []
                for hh in range(KEY_TILE // LANE):
                    c0 = blk * KEY_TILE + hh * LANE
                    s_h = s_scr[e_cur][:, c0:c0 + LANE]
                    p_parts.append(jnp.exp2((s_h - m_b) * exp_scale).astype(BF16))
                p_t = jnp.concatenate(p_parts, axis=1)
                part = _dot(p_t, kv_tile(v_ref, vc_ref, v_head, blk))
                acc = part if acc is None else acc + part
        if e_cur is not None:
            acc_scr[e_cur][...] = acc

    region(0, 0, None, None, None)
    region(0, 1, 0, 0, None)
    region(1, 0, 0, 1, 0)

    def body(t, carry):
        @pl.when(t > 0)
        def _():
            region(t, 1, t, 0, t - 1)

        @pl.when(t < n_units)
        def _():
            region(t + 1, 0, t, 1, t)

        return carry

    lax.fori_loop(1, n_units - 1, body, 0)
    region(n_units - 1, 1, n_units - 1, 0, n_units - 2)
    region(None, None, n_units - 1, 1, n_units - 1)
    finish_prev(1, n_units - 1)


def _attn_pipe_call(q, k, v, kc, vc, batch, seq_len, tq, scale, shared_kv, name):
    n = q.shape[1]
    past = kc.shape[1] // batch
    exp_scale = scale * math.log2(math.e)

    def spec(a, rows):
        return pl.BlockSpec((a.shape[0], rows, LANE), lambda b: (0, b, 0), pipeline_mode=pl.Buffered(1))

    total = seq_len + past
    return pl.pallas_call(
        functools.partial(_attn_pipe_body, shared_kv, exp_scale, tq),
        grid=(batch,),
        in_specs=[spec(q, seq_len), spec(k, seq_len), spec(v, seq_len), spec(kc, past), spec(vc, past)],
        out_specs=pl.BlockSpec((N_PAIRS, seq_len, LANE), lambda b: (0, b, 0)),
        out_shape=jax.ShapeDtypeStruct((N_PAIRS, n, LANE), BF16),
        scratch_shapes=[pltpu.VMEM((tq, total), F32), pltpu.VMEM((tq, total), F32)]
        + [pltpu.VMEM((tq, LANE), F32) for _ in range(5)],
        compiler_params=pltpu.CompilerParams(
            dimension_semantics=("arbitrary",), vmem_limit_bytes=VMEM_LIMIT),
        name=name,
    )(q, k, v, kc, vc)


def _seq_pos(tm, seq_len):
    i = pl.program_id(0)
    row = lax.broadcasted_iota(jnp.int32, (tm, 1), 0) + i * tm
    return jnp.bitwise_and(row, seq_len - 1)


def _merge_body(seq_len, x_ref, mod_ref, gmix_ref, wg_ref, a_ref, b_ref, pc_ref, pl_ref, pr_ref,
                wpool_ref, pscale_ref, wa_ref, wb_ref, wc_ref, wo_ref, o_ref, merged_scr, ab_scr):
    x = x_ref[...]
    tm = x.shape[0]
    mod = mod_ref[0]
    sh1 = mod[:, 0:D_MODEL]
    sc1 = mod[:, D_MODEL:2 * D_MODEL]
    g1 = mod[:, 2 * D_MODEL:3 * D_MODEL]
    h = (_rms(x, gmix_ref[...]) * (1.0 + sc1) + sh1).astype(BF16)

    pos = _seq_pos(tm, seq_len)
    ext = jnp.concatenate([pl_ref[...], pc_ref[...], pr_ref[...]], axis=0)
    pscale = pscale_ref[...]

    def pool_group(g):
        w = POOL_WINDOWS[g]
        eg = ext[:, g * POOL_GROUP:(g + 1) * POOL_GROUP]
        centre = eg[HALO:HALO + tm]
        wsum = centre
        for dlt in range(-(w // 2), w - w // 2):
            if dlt == 0:
                continue
            valid = pos >= -dlt if dlt < 0 else pos < seq_len - dlt
            wsum = wsum + jnp.where(valid, eg[HALO + dlt:HALO + dlt + tm], 0.0)
        cnt = jnp.minimum(pos + (w - w // 2), seq_len) - jnp.maximum(pos - w // 2, 0)
        pooled = (wsum / cnt.astype(F32) - centre).astype(BF16)
        return _dot(pooled, wpool_ref[g]) * pscale[:, g * POOL_GROUP:(g + 1) * POOL_GROUP]

    def gated(branch, src, w_ref, j):
        c0 = branch * D_MODEL + j * FF_CHUNK
        gate = jax.nn.sigmoid(_dot(h, wg_ref[:, c0:c0 + FF_CHUNK]))
        return gate * _dot(src, w_ref[:, j * FF_CHUNK:(j + 1) * FF_CHUNK])

    a_out = jnp.concatenate([a_ref[p] for p in range(N_PAIRS)], axis=1)
    b_out = jnp.concatenate([b_ref[p] for p in range(N_PAIRS)], axis=1)
    n_chunks = D_MODEL // FF_CHUNK
    c_parts = []
    for j in range(n_chunks):
        cols = slice(j * FF_CHUNK, (j + 1) * FF_CHUNK)
        for g in range(j * len(POOL_WINDOWS) // n_chunks, (j + 1) * len(POOL_WINDOWS) // n_chunks):
            c_parts.append(pool_group(g))
        ab_scr[:, cols] = gated(0, a_out, wa_ref, j) + gated(1, b_out, wb_ref, j)
    c_out = jnp.concatenate(c_parts, axis=-1).astype(BF16)
    for j in range(n_chunks):
        cols = slice(j * FF_CHUNK, (j + 1) * FF_CHUNK)
        merged_scr[:, cols] = (ab_scr[:, cols] + gated(2, c_out, wc_ref, j)).astype(BF16)
    o_ref[...] = x + g1 * _dot(merged_scr[...], wo_ref[...])


def _halo_specs(tm, width, n):
    blocks = tm // HALO
    last = n // HALO - 1
    left = pl.BlockSpec((HALO, width), lambda i: (jnp.maximum(i * blocks - 1, 0), 0))
    right = pl.BlockSpec((HALO, width), lambda i: (jnp.minimum((i + 1) * blocks, last), 0))
    return left, right


def _mod_spec(layer, mod_base, per_seq, latent):
    if latent:
        return pl.BlockSpec((None, 1, 1, 6 * D_MODEL), lambda i: (layer, mod_base + i // per_seq, 0, 0))
    return pl.BlockSpec((None, 1, 1, 6 * D_MODEL), lambda i: (layer, mod_base, 0, 0))


def _merge_call(x, mod4, layer, mod_base, seq_len, tm, latent, a_out, b_out, pool_in, wl):
    n = x.shape[0]
    per_seq = seq_len // tm if latent else 1

    def tile(i):
        return (i, 0)

    left, right = _halo_specs(tm, POOL_WIDTH, n)
    consts = [wl["wpool"], wl["pscale"], wl["wa"], wl["wb"], wl["wc"], wl["wo"]]
    in_specs = [pl.BlockSpec((tm, D_MODEL), tile), _mod_spec(layer, mod_base, per_seq, latent),
                _wspec(wl["gmix"], layer), _wspec(wl["wg"], layer),
                pl.BlockSpec((N_PAIRS, tm, LANE), lambda i: (0, i, 0)),
                pl.BlockSpec((N_PAIRS, tm, LANE), lambda i: (0, i, 0)),
                pl.BlockSpec((tm, POOL_WIDTH), tile), left, right]
    in_specs += [_wspec(a, layer) for a in consts]
    return pl.pallas_call(
        functools.partial(_merge_body, seq_len),
        grid=(n // tm,),
        in_specs=in_specs,
        out_specs=pl.BlockSpec((tm, D_MODEL), tile),
        out_shape=jax.ShapeDtypeStruct((n, D_MODEL), F32),
        scratch_shapes=[pltpu.VMEM((tm, D_MODEL), BF16), pltpu.VMEM((tm, D_MODEL), F32)],
        compiler_params=pltpu.CompilerParams(
            dimension_semantics=("arbitrary",), vmem_limit_bytes=VMEM_LIMIT),
        name="merge_lat" if latent else "merge_ctx",
    )(x, mod4, wl["gmix"], wl["wg"], a_out, b_out, pool_in, pool_in, pool_in, *consts)


def _ffn_body(seq_len, final, x_ref, xl_ref, xr_ref, mod_ref, gffn_ref, wug_ref, wuv_ref, cw_ref, cb_ref,
              wd_ref, gfin_ref, o_ref, act_scr):
    x = x_ref[...]
    tm = x.shape[0]
    mod = mod_ref[0]
    sh2 = mod[:, 3 * D_MODEL:4 * D_MODEL]
    sc2 = mod[:, 4 * D_MODEL:5 * D_MODEL]
    g2 = mod[:, 5 * D_MODEL:6 * D_MODEL]
    xe = jnp.concatenate([xl_ref[...], x, xr_ref[...]], axis=0)
    h2e = _rms(xe, gffn_ref[...]) * (1.0 + sc2) + sh2
    h2 = h2e[HALO:HALO + tm].astype(BF16)
    h2e = h2e.astype(BF16)

    pos = _seq_pos(tm, seq_len)
    has_prev = pos >= 1
    has_next = pos <= seq_len - 2
    cw = cw_ref[...]
    cb = cb_ref[...]
    for j in range(D_FF // FF_CHUNK):
        cols = slice(j * FF_CHUNK, (j + 1) * FF_CHUNK)
        ge = _dot(h2e, wug_ref[:, cols])
        g = (jnp.where(has_prev, ge[HALO - 1:HALO - 1 + tm], 0.0) * cw[0:1, cols]
             + ge[HALO:HALO + tm] * cw[1:2, cols]
             + jnp.where(has_next, ge[HALO + 1:HALO + 1 + tm], 0.0) * cw[2:3, cols]
             + cb[:, cols])
        val = _dot(h2, wuv_ref[:, cols])
        act_scr[:, cols] = (g * jax.nn.sigmoid(g) * val).astype(BF16)
    y = x + g2 * _dot(act_scr[...], wd_ref[...])
    if final:
        y = _rms(y, gfin_ref[...])
    o_ref[...] = y


def _ffn_call(x, mod4, layer, mod_base, seq_len, tm, latent, final, wl, g_final):
    n = x.shape[0]
    per_seq = seq_len // tm if latent else 1

    def tile(i):
        return (i, 0)

    left, right = _halo_specs(tm, D_MODEL, n)
    consts = [wl["gffn"], wl["wu"], wl["wu"], wl["cw"], wl["cb"], wl["wd"], g_final]
    in_specs = [pl.BlockSpec((tm, D_MODEL), tile), left, right, _mod_spec(layer, mod_base, per_seq, latent),
                _wspec(wl["gffn"], layer), _wspec(wl["wu"], layer, cols=D_FF, col_block=0),
                _wspec(wl["wu"], layer, cols=D_FF, col_block=1), _wspec(wl["cw"], layer),
                _wspec(wl["cb"], layer), _wspec(wl["wd"], layer), _const_spec(g_final.shape)]
    return pl.pallas_call(
        functools.partial(_ffn_body, seq_len, final),
        grid=(n // tm,),
        in_specs=in_specs,
        out_specs=pl.BlockSpec((tm, D_MODEL), tile),
        out_shape=jax.ShapeDtypeStruct((n, D_MODEL), F32),
        scratch_shapes=[pltpu.VMEM((tm, D_FF), BF16)],
        compiler_params=pltpu.CompilerParams(
            dimension_semantics=("arbitrary",), vmem_limit_bytes=VMEM_LIMIT),
        name="ffn_lat" if latent else "ffn_ctx",
    )(x, x, x, mod4, *consts)


def _rot_partner(a):
    d = a.shape[-1]
    return jnp.flip(a.reshape(a.shape[:-1] + (2, 2, d // 4)), axis=-2).reshape(a.shape)


def _prep_weights(g_norm_mix, w_in, g_q_a, w_q_b, g_kv_a, w_kv_b, g_q_gqa, g_k_gqa, w_pool, pool_scale,
                  w_br_a, w_br_b, w_br_c, w_out, g_norm_ffn, w_up, conv_w, conv_b, w_down):
    depth = w_in.shape[0]
    hd = GQA_HEAD_DIM
    grp = GQA_HEADS // GQA_KV_HEADS

    def lane_pad(a, before, width=LANE):
        cfg = [(0, 0)] * (a.ndim - 1) + [(before, width - before - a.shape[-1])]
        return jnp.pad(a, cfg)

    def pair_heads(a):
        lead = a.shape[:-1]
        return a.reshape(lead + (GQA_KV_HEADS, grp, hd)).swapaxes(-3, -2).reshape(lead + (GQA_HEADS * hd,))

    def per_head_partner(a, heads, dim):
        lead = a.shape[:-1]
        return _rot_partner(a.reshape(lead + (heads, dim))).reshape(lead + (heads * dim,))

    gq_w = pair_heads(w_in[:, :, _OFF_GQ:_OFF_GK])
    gk_w = w_in[:, :, _OFF_GK:_OFF_GV]
    kr_w = w_in[:, :, _OFF_KR:_OFF_GQ]
    w1 = jnp.concatenate(
        [w_in[:, :, _OFF_QA:_OFF_KR], gq_w, gk_w, w_in[:, :, _OFF_GV:_OFF_GATE], lane_pad(kr_w, MLA_NOPE),
         per_head_partner(gq_w, GQA_HEADS, hd), per_head_partner(gk_w, GQA_KV_HEADS, hd),
         lane_pad(_rot_partner(kr_w), MLA_NOPE)], axis=-1).astype(BF16)

    qb = w_q_b.reshape(depth, Q_LORA, MLA_HEADS, MLA_NOPE + MLA_ROPE)
    wqb = lane_pad(qb, 0).reshape(depth, Q_LORA, MLA_HEADS * LANE).astype(BF16)
    wqbsw = lane_pad(_rot_partner(qb[..., MLA_NOPE:]), MLA_NOPE).reshape(
        depth, Q_LORA, MLA_HEADS * LANE).astype(BF16)

    kvb = w_kv_b.reshape(depth, KV_LORA, MLA_HEADS, MLA_NOPE + MLA_V)
    wk = lane_pad(kvb[..., :MLA_NOPE], 0).reshape(depth, KV_LORA, MLA_HEADS * LANE).astype(BF16)
    wv = kvb[..., MLA_NOPE:].reshape(depth, KV_LORA, MLA_HEADS * MLA_V).astype(BF16)

    pk = np.zeros((LANE, MLA_HEADS * LANE), np.float32)
    for hh in range(MLA_HEADS):
        for t in range(MLA_ROPE):
            pk[MLA_NOPE + t, hh * LANE + MLA_NOPE + t] = 1.0

    row = lambda a: a[:, None, :]
    two = lambda a: jnp.concatenate([a, a], axis=-1)
    wb = w_br_b.reshape(depth, GQA_KV_HEADS, grp, hd, D_MODEL).swapaxes(1, 2).reshape(depth, GQA_HEADS * hd, D_MODEL)
    return dict(
        gmix=row(g_norm_mix), w1=w1, gqa=row(g_q_a), wqb=wqb, wqbsw=wqbsw, gkva=row(g_kv_a), wk=wk,
        pk=jnp.asarray(pk, BF16), wv=wv, gq2=row(two(g_q_gqa)), gq2sw=row(two(_rot_partner(g_q_gqa))),
        gk2=row(two(g_k_gqa)), gk2sw=row(two(_rot_partner(g_k_gqa))),
        wg=w_in[:, :, _OFF_GATE:].astype(BF16), wpool=w_pool.astype(BF16), pscale=row(pool_scale),
        wa=w_br_a.astype(BF16), wb=wb.astype(BF16), wc=w_br_c.astype(BF16), wo=w_out.astype(BF16),
        gffn=row(g_norm_ffn), wu=w_up.astype(BF16), cw=conv_w, cb=row(conv_b), wd=w_down.astype(BF16),
    )


def _layer(x, mod4, layer, mod_base, batch, seq_len, latent, final, wl, g_final, tabs, caches):
    tm = TOKEN_TILE
    tq = Q_TILE
    assert tm % seq_len == 0 or seq_len % tm == 0
    pj = _inproj_call(x, mod4, layer, mod_base, seq_len, tm, latent, wl, tabs)
    if caches is not None:
        kc_m, vc_m, kc_g, vc_g = _cachekv_call(caches, layer, wl)
    mla_scale = (MLA_NOPE + MLA_ROPE) ** -0.5
    n = x.shape[0]
    k_g = pj["k_gqa"].reshape(1, n, LANE)
    if latent:
        a_out = _attn_pipe_call(pj["q_mla"], pj["k_mla"], pj["v_mla"], kc_m, vc_m, batch, seq_len, tq,
                                mla_scale, False, "attn_mla_lat")
        b_out = _attn_pipe_call(pj["q_gqa"], k_g, pj["v_gqa"], kc_g, vc_g, batch, seq_len, tq, 1.0, True,
                                "attn_gqa_lat")
    else:
        a_out = _attn_call(pj["q_mla"], pj["k_mla"], pj["v_mla"], batch, seq_len, CTX_GROUP,
                           mla_scale, False, "attn_mla_ctx")
        b_out = _attn_call(pj["q_gqa"], k_g, pj["v_gqa"], batch, seq_len, CTX_GROUP, 1.0, True,
                           "attn_gqa_ctx")
    x1 = _merge_call(x, mod4, layer, mod_base, seq_len, tm, latent, a_out, b_out, pj["pool"], wl)
    x2 = _ffn_call(x1, mod4, layer, mod_base, seq_len, tm, latent, final, wl, g_final)
    return x2, pj


def kernel(x_prompt, x_sample, c, cache_mla_ckv, cache_mla_krope, cache_gqa_k, cache_gqa_v, c_ctx, w_ada, b_ada, g_norm_mix, w_in, g_q_a, w_q_b, g_kv_a, w_kv_b, g_q_gqa, g_k_gqa, w_pool, pool_scale, w_br_a, w_br_b, w_br_c, w_out, g_norm_ffn, w_up, conv_w, conv_b, w_down, g_final):
    depth = w_in.shape[0]
    bc, tc, _ = x_prompt.shape
    bl, tl, _ = x_sample.shape
    past = cache_mla_ckv.shape[2]
    mod_rows = 8
    assert 1 + bl <= mod_rows and tc & (tc - 1) == 0 and tl & (tl - 1) == 0

    cin = jnp.concatenate([c_ctx[None, :], c, jnp.zeros((mod_rows - 1 - bl, D_MODEL), F32)], axis=0)
    mod4 = _ada_call(cin, w_ada, b_ada).reshape(depth, mod_rows, 1, 6 * D_MODEL)
    tabs = _rope_tables(tl)
    g_fin = g_final.reshape(1, D_MODEL)
    wl = _prep_weights(g_norm_mix, w_in, g_q_a, w_q_b, g_kv_a, w_kv_b, g_q_gqa, g_k_gqa, w_pool,
                       pool_scale, w_br_a, w_br_b, w_br_c, w_out, g_norm_ffn, w_up, conv_w, conv_b, w_down)
    kr_pad = jnp.pad(cache_mla_krope, ((0, 0), (0, 0), (0, 0), (MLA_NOPE, LANE - MLA_NOPE - MLA_ROPE)))
    caches = (cache_mla_ckv, kr_pad, cache_gqa_k.reshape(bl, depth, past, LANE),
              cache_gqa_v.reshape(bl, depth, past, LANE))

    xc = x_prompt.reshape(bc * tc, D_MODEL)
    xl = x_sample.reshape(bl * tl, D_MODEL)
    st_ckv, st_kr, st_k, st_v = [], [], [], []
    for l in range(depth):
        final = l == depth - 1
        xc, pj = _layer(xc, mod4, l, 0, bc, tc, False, final, wl, g_fin, None, None)
        st_ckv.append(pj["ckv_out"].reshape(bc, tc, KV_LORA))
        st_kr.append(pj["kr_out"].reshape(bc, tc, MLA_ROPE))
        st_k.append(pj["kg_out"].reshape(bc, tc, GQA_KV_HEADS, GQA_HEAD_DIM))
        st_v.append(pj["vg_out"].reshape(bc, tc, GQA_KV_HEADS, GQA_HEAD_DIM))
        xl, _ = _layer(xl, mod4, l, 1, bl, tl, True, final, wl, g_fin, tabs, caches)

    return (xc.reshape(bc, tc, D_MODEL), xl.reshape(bl, tl, D_MODEL),
            jnp.stack(st_ckv, axis=1), jnp.stack(st_kr, axis=1),
            jnp.stack(st_k, axis=1), jnp.stack(st_v, axis=1))
```

```python
import functools
import math

import numpy as np
import jax
import jax.numpy as jnp
from jax import lax
from jax.experimental import pallas as pl
from jax.experimental.pallas import tpu as pltpu

D_MODEL = 1024
GRID_W = 64
RMS_EPS = 1e-6
ROPE_THETA = 10000.0
MLA_HEADS = 8
MLA_NOPE = 64
MLA_ROPE = 32
MLA_V = 64
Q_LORA = 256
KV_LORA = 128
GQA_HEADS = 8
GQA_KV_HEADS = 2
GQA_HEAD_DIM = 64
POOL_WINDOWS = (2, 4, 8, 16)
POOL_GROUP = 128
POOL_WIDTH = POOL_GROUP * len(POOL_WINDOWS)
D_FF = 2816
N_BRANCH = 3

LANE = 128
HALO = 8
FF_CHUNK = 256
KEY_TILE = 256
CTX_GROUP = 1
CTX_Q_ROWS = 256
TOKEN_TILE = 512
Q_TILE = 512
LAT_PAIRS = 2
N_PAIRS = MLA_HEADS // 2
VMEM_LIMIT = 56 * 1024 * 1024

BF16 = jnp.bfloat16
F32 = jnp.float32

_OFF_QA = 0
_OFF_CKV = _OFF_QA + Q_LORA
_OFF_KR = _OFF_CKV + KV_LORA
_OFF_GQ = _OFF_KR + MLA_ROPE
_OFF_GK = _OFF_GQ + GQA_HEADS * GQA_HEAD_DIM
_OFF_GV = _OFF_GK + GQA_KV_HEADS * GQA_HEAD_DIM
_OFF_POOL = _OFF_GV + GQA_KV_HEADS * GQA_HEAD_DIM
_OFF_GATE = _OFF_POOL + POOL_WIDTH

_W1_QA = 0
_W1_CKV = 256
_W1_GQ = 384
_W1_GK = 896
_W1_GV = 1024
_W1_POOL = 1152
_W1_KR = 1664
_W1_CTX_COLS = 1792
_W1_GQ_SW = 1792
_W1_GK_SW = 2304
_W1_KR_SW = 2432
_W1_LAT_COLS = 2560


def _const_spec(shape):
    nd = len(shape)
    return pl.BlockSpec(shape, lambda *_: (0,) * nd, pipeline_mode=pl.Buffered(1))


def _wspec(a, layer, cols=None, col_block=0):
    shape = a.shape[1:] if cols is None else a.shape[1:-1] + (cols,)
    idx = (layer,) + (0,) * (len(shape) - 1) + (col_block,)
    return pl.BlockSpec((None,) + shape, lambda *_: idx, pipeline_mode=pl.Buffered(1))


def _rms(x, g):
    ms = jnp.mean(x * x, axis=-1, keepdims=True)
    return x * lax.rsqrt(ms + RMS_EPS) * g


def _dot(a, b):
    return jnp.dot(a, b, preferred_element_type=F32)


def _rope_head_tables(n_tokens, d):
    dim_axis = d // 2
    t = np.arange(n_tokens)
    row = (t // GRID_W).astype(np.float64)
    col = (t % GRID_W).astype(np.float64)
    freqs = ROPE_THETA ** (-np.arange(0, dim_axis, 2, dtype=np.float64) / dim_axis)
    ar = row[:, None] * freqs[None, :]
    ac = col[:, None] * freqs[None, :]
    cos = np.concatenate([np.cos(ar), np.cos(ar), np.cos(ac), np.cos(ac)], axis=1)
    sin = np.concatenate([-np.sin(ar), np.sin(ar), -np.sin(ac), np.sin(ac)], axis=1)
    return cos, sin


def _rope_tables(n_tokens):
    cg, sg = _rope_head_tables(n_tokens, GQA_HEAD_DIM)
    cg = np.concatenate([cg, cg], axis=1)
    sg = np.concatenate([sg, sg], axis=1)
    cm32, sm32 = _rope_head_tables(n_tokens, MLA_ROPE)
    pad = LANE - MLA_NOPE - MLA_ROPE
    cm = np.concatenate([np.ones((n_tokens, MLA_NOPE)), cm32, np.zeros((n_tokens, pad))], axis=1)
    sm = np.concatenate([np.zeros((n_tokens, MLA_NOPE)), sm32, np.zeros((n_tokens, pad))], axis=1)
    return tuple(jnp.asarray(a, F32) for a in (cg, sg, cm, sm))


def _ada_body(c_ref, w_ref, b_ref, o_ref):
    c = c_ref[...]
    s = (c * jax.nn.sigmoid(c)).astype(BF16)
    o_ref[0] = _dot(s, w_ref[0].astype(BF16)) + b_ref[0]


def _ada_call(cin, w_ada, b_ada):
    depth, d, n = w_ada.shape
    cols = 2048
    rows = cin.shape[0]
    return pl.pallas_call(
        _ada_body,
        grid=(depth, n // cols),
        in_specs=[
            pl.BlockSpec((rows, d), lambda l, j: (0, 0)),
            pl.BlockSpec((1, d, cols), lambda l, j: (l, 0, j)),
            pl.BlockSpec((1, 1, cols), lambda l, j: (l, 0, j)),
        ],
        out_specs=pl.BlockSpec((1, rows, cols), lambda l, j: (l, 0, j)),
        out_shape=jax.ShapeDtypeStruct((depth, rows, n), F32),
        compiler_params=pltpu.CompilerParams(
            dimension_semantics=("arbitrary", "arbitrary"), vmem_limit_bytes=VMEM_LIMIT),
        name="ada_mod",
    )(cin, w_ada, b_ada.reshape(depth, 1, n))


def _two_head_rsqrt(xb, lo):
    x2 = xb * xb
    s_lo = jnp.sum(jnp.where(lo, x2, 0.0), axis=-1, keepdims=True)
    s_hi = jnp.sum(jnp.where(lo, 0.0, x2), axis=-1, keepdims=True)
    ms = jnp.where(lo, s_lo, s_hi) * (1.0 / GQA_HEAD_DIM)
    return lax.rsqrt(ms + RMS_EPS)


def _store_value_heads(v_ref, first, blk, lo):
    v_ref[first] = jnp.where(lo, blk, 1.0).astype(BF16)
    v_ref[first + 1] = jnp.where(lo, 1.0, blk).astype(BF16)


def _inproj_body(latent, names, *refs):
    r = dict(zip(names, refs))
    x = r["x"][...]
    tm = x.shape[0]
    mod = r["mod"][0]
    sh1 = mod[:, 0:D_MODEL]
    sc1 = mod[:, D_MODEL:2 * D_MODEL]
    h = (_rms(x, r["gmix"][...]) * (1.0 + sc1) + sh1).astype(BF16)
    z = _dot(h, r["w1"][...])

    lane = lax.broadcasted_iota(jnp.int32, (tm, LANE), 1)
    lo = lane < GQA_HEAD_DIM

    if latent:
        cg = r["cg"][...]
        sg = r["sg"][...]
        cm = r["cm"][...]
        sm = r["sm"][...]

    qn = _rms(z[:, _W1_QA:_W1_QA + Q_LORA], r["gqa"][...]).astype(BF16)
    qm = _dot(qn, r["wqb"][...])
    if latent:
        qms = _dot(qn, r["wqbsw"][...])
    for hh in range(MLA_HEADS):
        blk = qm[:, hh * LANE:(hh + 1) * LANE]
        if latent:
            blk = blk * cm + qms[:, hh * LANE:(hh + 1) * LANE] * sm
        r["q_mla"][hh] = blk.astype(BF16)

    ckv = _rms(z[:, _W1_CKV:_W1_CKV + KV_LORA], r["gkva"][...])
    kr = z[:, _W1_KR:_W1_KR + LANE]
    if latent:
        kr = kr * cm + z[:, _W1_KR_SW:_W1_KR_SW + LANE] * sm
    else:
        r["ckv_out"][...] = ckv
        r["kr_out"][...] = kr[:, MLA_NOPE:MLA_NOPE + MLA_ROPE]
    ckv_b = ckv.astype(BF16)
    kfull = _dot(ckv_b, r["wk"][...]) + _dot(kr.astype(BF16), r["pk"][...])
    for hh in range(MLA_HEADS):
        r["k_mla"][hh] = kfull[:, hh * LANE:(hh + 1) * LANE].astype(BF16)
    vfull = _dot(ckv_b, r["wv"][...])
    for pp in range(N_PAIRS):
        _store_value_heads(r["v_mla"], 2 * pp, vfull[:, pp * LANE:(pp + 1) * LANE], lo)

    gq2 = r["gq2"][...]
    q_scale = GQA_HEAD_DIM ** -0.5
    for j in range(N_PAIRS):
        xb = z[:, _W1_GQ + j * LANE:_W1_GQ + (j + 1) * LANE]
        rs = _two_head_rsqrt(xb, lo)
        y = xb * rs * gq2
        if latent:
            xs = z[:, _W1_GQ_SW + j * LANE:_W1_GQ_SW + (j + 1) * LANE]
            y = y * cg + (xs * rs * r["gq2sw"][...]) * sg
        y = y * q_scale
        r["q_gqa"][2 * j] = jnp.where(lo, y, 0.0).astype(BF16)
        r["q_gqa"][2 * j + 1] = jnp.where(lo, 0.0, y).astype(BF16)
    kb = z[:, _W1_GK:_W1_GK + LANE]
    rs = _two_head_rsqrt(kb, lo)
    kg = kb * rs * r["gk2"][...]
    if latent:
        ks = z[:, _W1_GK_SW:_W1_GK_SW + LANE]
        kg = kg * cg + (ks * rs * r["gk2sw"][...]) * sg
    vg = z[:, _W1_GV:_W1_GV + LANE]
    if not latent:
        r["kg_out"][...] = kg
        r["vg_out"][...] = vg
    r["k_gqa"][...] = kg.astype(BF16)
    _store_value_heads(r["v_gqa"], 0, vg, lo)

    r["pool"][...] = z[:, _W1_POOL:_W1_POOL + POOL_WIDTH]


def _inproj_call(x, mod4, layer, mod_base, seq_len, tm, latent, wl, tabs):
    n = x.shape[0]
    nt = n // tm
    per_seq = seq_len // tm if latent else 1

    def tile(i):
        return (i, 0)

    def mod_idx(i):
        if latent:
            return (layer, mod_base + i // per_seq, 0, 0)
        return (layer, mod_base, 0, 0)

    names = ["x", "mod", "w1", "pk", "gmix", "gqa", "wqb", "gkva", "wk", "wv", "gq2", "gk2"]
    args = [x, mod4, wl["w1"], wl["pk"]] + [wl[nm] for nm in names[4:]]
    specs = [pl.BlockSpec((tm, D_MODEL), tile), pl.BlockSpec((None, 1, 1, 6 * D_MODEL), mod_idx),
             _wspec(wl["w1"], layer, cols=_W1_LAT_COLS if latent else _W1_CTX_COLS),
             _const_spec(wl["pk"].shape)]
    specs += [_wspec(a, layer) for a in args[4:]]
    if latent:
        for nm in ("wqbsw", "gq2sw", "gk2sw"):
            names.append(nm)
            args.append(wl[nm])
            specs.append(_wspec(wl[nm], layer))
        for nm, a in zip(("cg", "sg", "cm", "sm"), tabs):
            names.append(nm)
            args.append(a)
            specs.append(pl.BlockSpec((tm, LANE), lambda i: (i % per_seq, 0)))

    head_spec = pl.BlockSpec((MLA_HEADS, tm, LANE), lambda i: (0, i, 0))
    kvh_spec = pl.BlockSpec((GQA_KV_HEADS, tm, LANE), lambda i: (0, i, 0))
    out_names = ["q_mla", "k_mla", "v_mla", "q_gqa", "k_gqa", "v_gqa", "pool"]
    out_shapes = [
        jax.ShapeDtypeStruct((MLA_HEADS, n, LANE), BF16),
        jax.ShapeDtypeStruct((MLA_HEADS, n, LANE), BF16),
        jax.ShapeDtypeStruct((MLA_HEADS, n, LANE), BF16),
        jax.ShapeDtypeStruct((GQA_HEADS, n, LANE), BF16),
        jax.ShapeDtypeStruct((n, LANE), BF16),
        jax.ShapeDtypeStruct((GQA_KV_HEADS, n, LANE), BF16),
        jax.ShapeDtypeStruct((n, POOL_WIDTH), F32),
    ]
    out_specs = [head_spec, head_spec, head_spec, head_spec,
                 pl.BlockSpec((tm, LANE), tile), kvh_spec,
                 pl.BlockSpec((tm, POOL_WIDTH), tile)]
    if not latent:
        out_names += ["ckv_out", "kr_out", "kg_out", "vg_out"]
        out_shapes += [jax.ShapeDtypeStruct((n, KV_LORA), F32), jax.ShapeDtypeStruct((n, MLA_ROPE), F32),
                       jax.ShapeDtypeStruct((n, LANE), F32), jax.ShapeDtypeStruct((n, LANE), F32)]
        out_specs += [pl.BlockSpec((tm, KV_LORA), tile), pl.BlockSpec((tm, MLA_ROPE), tile),
                      pl.BlockSpec((tm, LANE), tile), pl.BlockSpec((tm, LANE), tile)]

    outs = pl.pallas_call(
        functools.partial(_inproj_body, latent, names + out_names),
        grid=(nt,),
        in_specs=specs,
        out_specs=out_specs,
        out_shape=out_shapes,
        compiler_params=pltpu.CompilerParams(
            dimension_semantics=("arbitrary",), vmem_limit_bytes=VMEM_LIMIT),
        name="inproj_lat" if latent else "inproj_ctx",
    )(*args)
    return dict(zip(out_names, outs))


def _cachekv_body(ckv_ref, kr_ref, kg_ref, vg_ref, wk_ref, pk_ref, wv_ref, k_ref, v_ref, kgo_ref, vgo_ref):
    rows = ckv_ref.shape[0] * ckv_ref.shape[1]
    ckv_b = ckv_ref[...].reshape(rows, LANE).astype(BF16)
    lo = lax.broadcasted_iota(jnp.int32, (rows, LANE), 1) < MLA_V
    kfull = _dot(ckv_b, wk_ref[...]) + _dot(kr_ref[...].reshape(rows, LANE).astype(BF16), pk_ref[...])
    kgo_ref[0] = kg_ref[...].reshape(rows, LANE).astype(BF16)
    for hh in range(MLA_HEADS):
        k_ref[hh] = kfull[:, hh * LANE:(hh + 1) * LANE].astype(BF16)
    vfull = _dot(ckv_b, wv_ref[...])
    for pp in range(N_PAIRS):
        _store_value_heads(v_ref, 2 * pp, vfull[:, pp * LANE:(pp + 1) * LANE], lo)
    _store_value_heads(vgo_ref, 0, vg_ref[...].reshape(rows, LANE), lo)


def _cachekv_call(caches, layer, wl):
    bsz, _, past, _ = caches[0].shape
    n = bsz * past
    cache_spec = pl.BlockSpec((bsz, None, past, LANE), lambda i: (0, layer, 0, 0))
    return pl.pallas_call(
        _cachekv_body,
        grid=(1,),
        in_specs=[cache_spec] * 4 + [_wspec(wl["wk"], layer), _const_spec(wl["pk"].shape),
                                     _wspec(wl["wv"], layer)],
        out_specs=[_const_spec((MLA_HEADS, n, LANE)), _const_spec((MLA_HEADS, n, LANE)),
                   _const_spec((1, n, LANE)), _const_spec((GQA_KV_HEADS, n, LANE))],
        out_shape=[jax.ShapeDtypeStruct((MLA_HEADS, n, LANE), BF16),
                   jax.ShapeDtypeStruct((MLA_HEADS, n, LANE), BF16),
                   jax.ShapeDtypeStruct((1, n, LANE), BF16),
                   jax.ShapeDtypeStruct((GQA_KV_HEADS, n, LANE), BF16)],
        compiler_params=pltpu.CompilerParams(
            dimension_semantics=("arbitrary",), vmem_limit_bytes=VMEM_LIMIT),
        name="cache_kv",
    )(*caches, wl["wk"], wl["pk"], wl["wv"])


def _normalise_pair(acc_a, acc_b, lo):
    num = jnp.where(lo, acc_a, acc_b)
    den = pltpu.roll(jnp.where(lo, acc_b, acc_a), MLA_V, axis=1)
    return num / den


def _attn_body(group, seq_len, shared_kv, exp_scale, q_ref, k_ref, v_ref, o_ref):
    lo = lax.broadcasted_iota(jnp.int32, (CTX_Q_ROWS, LANE), 1) < MLA_V
    nt_dims = (((1,), (1,)), ((), ()))
    units = [(g, pr, qb, e) for g in range(group) for pr in range(N_PAIRS)
             for qb in range(seq_len // CTX_Q_ROWS) for e in range(2)]

    def unit_rows(u):
        g, _, qb, _ = u
        q0 = g * seq_len + qb * CTX_Q_ROWS
        return slice(g * seq_len, (g + 1) * seq_len), slice(q0, q0 + CTX_Q_ROWS)

    scores, row_max, accs = {}, {}, {}
    for i in range(len(units) + 2):
        if i < len(units):
            _, pr, _, e = units[i]
            rows, qrows = unit_rows(units[i])
            q = q_ref[2 * pr + e, qrows, :]
            k = k_ref[0 if shared_kv else 2 * pr + e, rows, :]
            scores[i] = lax.dot_general(q, k, nt_dims, preferred_element_type=F32)
        if 0 <= i - 1 < len(units):
            row_max[i - 1] = jnp.max(scores[i - 1], axis=-1, keepdims=True)
        if 0 <= i - 2 < len(units):
            u = i - 2
            _, pr, _, e = units[u]
            rows, qrows = unit_rows(units[u])
            v = v_ref[e if shared_kv else 2 * pr + e, rows, :]
            p = jnp.exp2((scores.pop(u) - row_max.pop(u)) * exp_scale)
            accs[u] = _dot(p.astype(BF16), v)
            if e == 1:
                o_ref[pr, qrows, :] = _normalise_pair(accs.pop(u - 1), accs.pop(u), lo).astype(BF16)


def _attn_call(q, k, v, batch, seq_len, group, scale, shared_kv, name):
    n = q.shape[1]
    rows = group * seq_len
    exp_scale = scale * math.log2(math.e)

    def spec(heads):
        return pl.BlockSpec((heads, rows, LANE), lambda b: (0, b, 0))

    return pl.pallas_call(
        functools.partial(_attn_body, group, seq_len, shared_kv, exp_scale),
        grid=(batch // group,),
        in_specs=[spec(q.shape[0]), spec(k.shape[0]), spec(v.shape[0])],
        out_specs=pl.BlockSpec((N_PAIRS, rows, LANE), lambda b: (0, b, 0)),
        out_shape=jax.ShapeDtypeStruct((N_PAIRS, n, LANE), BF16),
        compiler_params=pltpu.CompilerParams(
            dimension_semantics=("arbitrary",), vmem_limit_bytes=VMEM_LIMIT),
        name=name,
    )(q, k, v)


def _attn_pipe_body(shared_kv, exp_scale, tq, pairs, q_ref, k_ref, v_ref, kc_ref, vc_ref, o_ref,
                    s0, s1, mr0, mr1, ac0, ac1):
    seq = k_ref.shape[1]
    past = kc_ref.shape[1]
    nq = seq // tq
    n_units = pairs * nq
    n_new = seq // KEY_TILE
    n_tiles = n_new + past // KEY_TILE
    s_scr = (s0, s1)
    mrun_scr = (mr0, mr1)
    acc_scr = (ac0, ac1)
    nt_dims = (((1,), (1,)), ((), ()))
    lo = lax.broadcasted_iota(jnp.int32, (tq, LANE), 1) < MLA_V

    def pair_rows(t):
        j = t % nq
        return t // nq, pl.ds(pl.multiple_of(j * tq, tq), tq)

    def kv_tile(new_ref, cache_ref, head, blk):
        if blk < n_new:
            return new_ref[head, blk * KEY_TILE:(blk + 1) * KEY_TILE, :]
        blk -= n_new
        return cache_ref[head, blk * KEY_TILE:(blk + 1) * KEY_TILE, :]

    def finish_prev(e_prev, t_prev):
        if e_prev == 1:
            pair, rows = pair_rows(t_prev)
            o_ref[pair, rows, :] = _normalise_pair(ac0[...], ac1[...], lo).astype(BF16)

    def region(t_scores, e_scores, t_cur, e_cur, t_prev):
        if t_prev is not None:
            finish_prev(1 - e_cur, t_prev)
        if e_scores is not None:
            pair, rows = pair_rows(t_scores)
            q = q_ref[2 * pair + e_scores, rows, :]
            k_head = 0 if shared_kv else 2 * pair + e_scores
        if e_cur is not None:
            v_head = e_cur if shared_kv else 2 * (t_cur // nq) + e_cur
            m = jnp.max(mrun_scr[e_cur][...], axis=-1, keepdims=True)
            m_b = jnp.broadcast_to(m, (tq, LANE))
        acc = None
        for blk in range(n_tiles):
            cols = slice(blk * KEY_TILE, (blk + 1) * KEY_TILE)
            if e_scores is not None:
                s_t = lax.dot_general(q, kv_tile(k_ref, kc_ref, k_head, blk), nt_dims,
                                      preferred_element_type=F32)
                s_scr[e_scores][:, cols] = s_t
                m_t = jnp.maximum(s_t[:, 0:LANE], s_t[:, LANE:2 * LANE])
                if blk > 0:
                    m_t = jnp.maximum(m_t, mrun_scr[e_scores][...])
                mrun_scr[e_scores][...] = m_t
            if e_cur is not None:
                p_parts = []
                for hh in range(KEY_TILE // LANE):
                    c0 = blk * KEY_TILE + hh * LANE
                    s_h = s_scr[e_cur][:, c0:c0 + LANE]
                    p_parts.append(jnp.exp2((s_h - m_b) * exp_scale).astype(BF16))
                p_t = jnp.concatenate(p_parts, axis=1)
                part = _dot(p_t, kv_tile(v_ref, vc_ref, v_head, blk))
                acc = part if acc is None else acc + part
        if e_cur is not None:
            acc_scr[e_cur][...] = acc

    region(0, 0, None, None, None)
    region(0, 1, 0, 0, None)
    region(1, 0, 0, 1, 0)

    def body(t, carry):
        @pl.when(t > 0)
        def _():
            region(t, 1, t, 0, t - 1)

        @pl.when(t < n_units)
        def _():
            region(t + 1, 0, t, 1, t)

        return carry

    lax.fori_loop(1, n_units - 1, body, 0)
    region(n_units - 1, 1, n_units - 1, 0, n_units - 2)
    region(None, None, n_units - 1, 1, n_units - 1)
    finish_prev(1, n_units - 1)


def _attn_pipe_call(q, k, v, kc, vc, batch, seq_len, tq, scale, shared_kv, name, pairs=LAT_PAIRS):
    n = q.shape[1]
    past = kc.shape[1] // batch
    exp_scale = scale * math.log2(math.e)

    def spec(a, rows):
        heads = a.shape[0]
        if heads < 2 * N_PAIRS:
            return pl.BlockSpec((heads, rows, LANE), lambda b, p: (0, b, 0))
        return pl.BlockSpec((2 * pairs, rows, LANE), lambda b, p: (p, b, 0))

    total = seq_len + past
    return pl.pallas_call(
        functools.partial(_attn_pipe_body, shared_kv, exp_scale, tq, pairs),
        grid=(batch, N_PAIRS // pairs),
        in_specs=[spec(q, seq_len), spec(k, seq_len), spec(v, seq_len), spec(kc, past), spec(vc, past)],
        out_specs=pl.BlockSpec((pairs, seq_len, LANE), lambda b, p: (p, b, 0)),
        out_shape=jax.ShapeDtypeStruct((N_PAIRS, n, LANE), BF16),
        scratch_shapes=[pltpu.VMEM((tq, total), F32), pltpu.VMEM((tq, total), F32)]
        + [pltpu.VMEM((tq, LANE), F32) for _ in range(4)],
        compiler_params=pltpu.CompilerParams(
            dimension_semantics=("arbitrary", "arbitrary"), vmem_limit_bytes=VMEM_LIMIT),
        name=name,
    )(q, k, v, kc, vc)


def _seq_pos(tm, seq_len):
    i = pl.program_id(0)
    row = lax.broadcasted_iota(jnp.int32, (tm, 1), 0) + i * tm
    return jnp.bitwise_and(row, seq_len - 1)


def _merge_body(seq_len, x_ref, mod_ref, gmix_ref, wg_ref, a_ref, b_ref, pc_ref, pl_ref, pr_ref,
                wpool_ref, pscale_ref, wa_ref, wb_ref, wc_ref, wo_ref, o_ref, merged_scr, ab_scr):
    x = x_ref[...]
    tm = x.shape[0]
    mod = mod_ref[0]
    sh1 = mod[:, 0:D_MODEL]
    sc1 = mod[:, D_MODEL:2 * D_MODEL]
    g1 = mod[:, 2 * D_MODEL:3 * D_MODEL]
    h = (_rms(x, gmix_ref[...]) * (1.0 + sc1) + sh1).astype(BF16)

    pos = _seq_pos(tm, seq_len)
    ext = jnp.concatenate([pl_ref[...], pc_ref[...], pr_ref[...]], axis=0)
    pscale = pscale_ref[...]

    def pool_group(g):
        w = POOL_WINDOWS[g]
        eg = ext[:, g * POOL_GROUP:(g + 1) * POOL_GROUP]
        centre = eg[HALO:HALO + tm]
        wsum = centre
        for dlt in range(-(w // 2), w - w // 2):
            if dlt == 0:
                continue
            valid = pos >= -dlt if dlt < 0 else pos < seq_len - dlt
            wsum = wsum + jnp.where(valid, eg[HALO + dlt:HALO + dlt + tm], 0.0)
        cnt = jnp.minimum(pos + (w - w // 2), seq_len) - jnp.maximum(pos - w // 2, 0)
        pooled = (wsum / cnt.astype(F32) - centre).astype(BF16)
        return _dot(pooled, wpool_ref[g]) * pscale[:, g * POOL_GROUP:(g + 1) * POOL_GROUP]

    def gated(branch, src, w_ref, j):
        c0 = branch * D_MODEL + j * FF_CHUNK
        gate = jax.nn.sigmoid(_dot(h, wg_ref[:, c0:c0 + FF_CHUNK]))
        return gate * _dot(src, w_ref[:, j * FF_CHUNK:(j + 1) * FF_CHUNK])

    a_out = jnp.concatenate([a_ref[p] for p in range(N_PAIRS)], axis=1)
    b_out = jnp.concatenate([b_ref[p] for p in range(N_PAIRS)], axis=1)
    n_chunks = D_MODEL // FF_CHUNK
    c_parts = []
    for j in range(n_chunks):
        cols = slice(j * FF_CHUNK, (j + 1) * FF_CHUNK)
        for g in range(j * len(POOL_WINDOWS) // n_chunks, (j + 1) * len(POOL_WINDOWS) // n_chunks):
            c_parts.append(pool_group(g))
        ab_scr[:, cols] = gated(0, a_out, wa_ref, j) + gated(1, b_out, wb_ref, j)
    c_out = jnp.concatenate(c_parts, axis=-1).astype(BF16)
    for j in range(n_chunks):
        cols = slice(j * FF_CHUNK, (j + 1) * FF_CHUNK)
        merged_scr[:, cols] = (ab_scr[:, cols] + gated(2, c_out, wc_ref, j)).astype(BF16)
    o_ref[...] = x + g1 * _dot(merged_scr[...], wo_ref[...])


def _halo_specs(tm, width, n):
    blocks = tm // HALO
    last = n // HALO - 1
    left = pl.BlockSpec((HALO, width), lambda i: (jnp.maximum(i * blocks - 1, 0), 0))
    right = pl.BlockSpec((HALO, width), lambda i: (jnp.minimum((i + 1) * blocks, last), 0))
    return left, right


def _mod_spec(layer, mod_base, per_seq, latent):
    if latent:
        return pl.BlockSpec((None, 1, 1, 6 * D_MODEL), lambda i: (layer, mod_base + i // per_seq, 0, 0))
    return pl.BlockSpec((None, 1, 1, 6 * D_MODEL), lambda i: (layer, mod_base, 0, 0))


def _merge_call(x, mod4, layer, mod_base, seq_len, tm, latent, a_out, b_out, pool_in, wl):
    n = x.shape[0]
    per_seq = seq_len // tm if latent else 1

    def tile(i):
        return (i, 0)

    left, right = _halo_specs(tm, POOL_WIDTH, n)
    consts = [wl["wpool"], wl["pscale"], wl["wa"], wl["wb"], wl["wc"], wl["wo"]]
    in_specs = [pl.BlockSpec((tm, D_MODEL), tile), _mod_spec(layer, mod_base, per_seq, latent),
                _wspec(wl["gmix"], layer), _wspec(wl["wg"], layer),
                pl.BlockSpec((N_PAIRS, tm, LANE), lambda i: (0, i, 0)),
                pl.BlockSpec((N_PAIRS, tm, LANE), lambda i: (0, i, 0)),
                pl.BlockSpec((tm, POOL_WIDTH), tile), left, right]
    in_specs += [_wspec(a, layer) for a in consts]
    return pl.pallas_call(
        functools.partial(_merge_body, seq_len),
        grid=(n // tm,),
        in_specs=in_specs,
        out_specs=pl.BlockSpec((tm, D_MODEL), tile),
        out_shape=jax.ShapeDtypeStruct((n, D_MODEL), F32),
        scratch_shapes=[pltpu.VMEM((tm, D_MODEL), BF16), pltpu.VMEM((tm, D_MODEL), F32)],
        compiler_params=pltpu.CompilerParams(
            dimension_semantics=("arbitrary",), vmem_limit_bytes=VMEM_LIMIT),
        name="merge_lat" if latent else "merge_ctx",
    )(x, mod4, wl["gmix"], wl["wg"], a_out, b_out, pool_in, pool_in, pool_in, *consts)


def _ffn_body(seq_len, final, x_ref, xl_ref, xr_ref, mod_ref, gffn_ref, wug_ref, wuv_ref, cw_ref, cb_ref,
              wd_ref, gfin_ref, o_ref, act_scr):
    x = x_ref[...]
    tm = x.shape[0]
    mod = mod_ref[0]
    sh2 = mod[:, 3 * D_MODEL:4 * D_MODEL]
    sc2 = mod[:, 4 * D_MODEL:5 * D_MODEL]
    g2 = mod[:, 5 * D_MODEL:6 * D_MODEL]
    xe = jnp.concatenate([xl_ref[...], x, xr_ref[...]], axis=0)
    h2e = _rms(xe, gffn_ref[...]) * (1.0 + sc2) + sh2
    h2 = h2e[HALO:HALO + tm].astype(BF16)
    h2e = h2e.astype(BF16)

    pos = _seq_pos(tm, seq_len)
    has_prev = pos >= 1
    has_next = pos <= seq_len - 2
    cw = cw_ref[...]
    cb = cb_ref[...]
    for j in range(D_FF // FF_CHUNK):
        cols = slice(j * FF_CHUNK, (j + 1) * FF_CHUNK)
        ge = _dot(h2e, wug_ref[:, cols])
        g = (jnp.where(has_prev, ge[HALO - 1:HALO - 1 + tm], 0.0) * cw[0:1, cols]
             + ge[HALO:HALO + tm] * cw[1:2, cols]
             + jnp.where(has_next, ge[HALO + 1:HALO + 1 + tm], 0.0) * cw[2:3, cols]
             + cb[:, cols])
        val = _dot(h2, wuv_ref[:, cols])
        act_scr[:, cols] = (g * jax.nn.sigmoid(g) * val).astype(BF16)
    y = x + g2 * _dot(act_scr[...], wd_ref[...])
    if final:
        y = _rms(y, gfin_ref[...])
    o_ref[...] = y


def _ffn_call(x, mod4, layer, mod_base, seq_len, tm, latent, final, wl, g_final):
    n = x.shape[0]
    per_seq = seq_len // tm if latent else 1

    def tile(i):
        return (i, 0)

    left, right = _halo_specs(tm, D_MODEL, n)
    consts = [wl["gffn"], wl["wu"], wl["wu"], wl["cw"], wl["cb"], wl["wd"], g_final]
    in_specs = [pl.BlockSpec((tm, D_MODEL), tile), left, right, _mod_spec(layer, mod_base, per_seq, latent),
                _wspec(wl["gffn"], layer), _wspec(wl["wu"], layer, cols=D_FF, col_block=0),
                _wspec(wl["wu"], layer, cols=D_FF, col_block=1), _wspec(wl["cw"], layer),
                _wspec(wl["cb"], layer), _wspec(wl["wd"], layer), _const_spec(g_final.shape)]
    return pl.pallas_call(
        functools.partial(_ffn_body, seq_len, final),
        grid=(n // tm,),
        in_specs=in_specs,
        out_specs=pl.BlockSpec((tm, D_MODEL), tile),
        out_shape=jax.ShapeDtypeStruct((n, D_MODEL), F32),
        scratch_shapes=[pltpu.VMEM((tm, D_FF), BF16)],
        compiler_params=pltpu.CompilerParams(
            dimension_semantics=("arbitrary",), vmem_limit_bytes=VMEM_LIMIT),
        name="ffn_lat" if latent else "ffn_ctx",
    )(x, x, x, mod4, *consts)


def _rot_partner(a):
    d = a.shape[-1]
    return jnp.flip(a.reshape(a.shape[:-1] + (2, 2, d // 4)), axis=-2).reshape(a.shape)


def _prep_weights(g_norm_mix, w_in, g_q_a, w_q_b, g_kv_a, w_kv_b, g_q_gqa, g_k_gqa, w_pool, pool_scale,
                  w_br_a, w_br_b, w_br_c, w_out, g_norm_ffn, w_up, conv_w, conv_b, w_down):
    depth = w_in.shape[0]
    hd = GQA_HEAD_DIM
    grp = GQA_HEADS // GQA_KV_HEADS

    def lane_pad(a, before, width=LANE):
        cfg = [(0, 0)] * (a.ndim - 1) + [(before, width - before - a.shape[-1])]
        return jnp.pad(a, cfg)

    def pair_heads(a):
        lead = a.shape[:-1]
        return a.reshape(lead + (GQA_KV_HEADS, grp, hd)).swapaxes(-3, -2).reshape(lead + (GQA_HEADS * hd,))

    def per_head_partner(a, heads, dim):
        lead = a.shape[:-1]
        return _rot_partner(a.reshape(lead + (heads, dim))).reshape(lead + (heads * dim,))

    gq_w = pair_heads(w_in[:, :, _OFF_GQ:_OFF_GK])
    gk_w = w_in[:, :, _OFF_GK:_OFF_GV]
    kr_w = w_in[:, :, _OFF_KR:_OFF_GQ]
    w1 = jnp.concatenate(
        [w_in[:, :, _OFF_QA:_OFF_KR], gq_w, gk_w, w_in[:, :, _OFF_GV:_OFF_GATE], lane_pad(kr_w, MLA_NOPE),
         per_head_partner(gq_w, GQA_HEADS, hd), per_head_partner(gk_w, GQA_KV_HEADS, hd),
         lane_pad(_rot_partner(kr_w), MLA_NOPE)], axis=-1).astype(BF16)

    qb = w_q_b.reshape(depth, Q_LORA, MLA_HEADS, MLA_NOPE + MLA_ROPE)
    wqb = lane_pad(qb, 0).reshape(depth, Q_LORA, MLA_HEADS * LANE).astype(BF16)
    wqbsw = lane_pad(_rot_partner(qb[..., MLA_NOPE:]), MLA_NOPE).reshape(
        depth, Q_LORA, MLA_HEADS * LANE).astype(BF16)

    kvb = w_kv_b.reshape(depth, KV_LORA, MLA_HEADS, MLA_NOPE + MLA_V)
    wk = lane_pad(kvb[..., :MLA_NOPE], 0).reshape(depth, KV_LORA, MLA_HEADS * LANE).astype(BF16)
    wv = kvb[..., MLA_NOPE:].reshape(depth, KV_LORA, MLA_HEADS * MLA_V).astype(BF16)

    pk = np.zeros((LANE, MLA_HEADS * LANE), np.float32)
    for hh in range(MLA_HEADS):
        for t in range(MLA_ROPE):
            pk[MLA_NOPE + t, hh * LANE + MLA_NOPE + t] = 1.0

    row = lambda a: a[:, None, :]
    two = lambda a: jnp.concatenate([a, a], axis=-1)
    wb = w_br_b.reshape(depth, GQA_KV_HEADS, grp, hd, D_MODEL).swapaxes(1, 2).reshape(depth, GQA_HEADS * hd, D_MODEL)
    return dict(
        gmix=row(g_norm_mix), w1=w1, gqa=row(g_q_a), wqb=wqb, wqbsw=wqbsw, gkva=row(g_kv_a), wk=wk,
        pk=jnp.asarray(pk, BF16), wv=wv, gq2=row(two(g_q_gqa)), gq2sw=row(two(_rot_partner(g_q_gqa))),
        gk2=row(two(g_k_gqa)), gk2sw=row(two(_rot_partner(g_k_gqa))),
        wg=w_in[:, :, _OFF_GATE:].astype(BF16), wpool=w_pool.astype(BF16), pscale=row(pool_scale),
        wa=w_br_a.astype(BF16), wb=wb.astype(BF16), wc=w_br_c.astype(BF16), wo=w_out.astype(BF16),
        gffn=row(g_norm_ffn), wu=w_up.astype(BF16), cw=conv_w, cb=row(conv_b), wd=w_down.astype(BF16),
    )


def _layer(x, mod4, layer, mod_base, batch, seq_len, latent, final, wl, g_final, tabs, caches):
    tm = TOKEN_TILE
    tq = Q_TILE
    assert tm % seq_len == 0 or seq_len % tm == 0
    pj = _inproj_call(x, mod4, layer, mod_base, seq_len, tm, latent, wl, tabs)
    if caches is not None:
        kc_m, vc_m, kc_g, vc_g = _cachekv_call(caches, layer, wl)
    mla_scale = (MLA_NOPE + MLA_ROPE) ** -0.5
    n = x.shape[0]
    k_g = pj["k_gqa"].reshape(1, n, LANE)
    if latent:
        a_out = _attn_pipe_call(pj["q_mla"], pj["k_mla"], pj["v_mla"], kc_m, vc_m, batch, seq_len, tq,
                                mla_scale, False, "attn_mla_lat")
        b_out = _attn_pipe_call(pj["q_gqa"], k_g, pj["v_gqa"], kc_g, vc_g, batch, seq_len, tq, 1.0, True,
                                "attn_gqa_lat")
    else:
        a_out = _attn_call(pj["q_mla"], pj["k_mla"], pj["v_mla"], batch, seq_len, CTX_GROUP,
                           mla_scale, False, "attn_mla_ctx")
        b_out = _attn_call(pj["q_gqa"], k_g, pj["v_gqa"], batch, seq_len, CTX_GROUP, 1.0, True,
                           "attn_gqa_ctx")
    x1 = _merge_call(x, mod4, layer, mod_base, seq_len, tm, latent, a_out, b_out, pj["pool"], wl)
    x2 = _ffn_call(x1, mod4, layer, mod_base, seq_len, tm, latent, final, wl, g_final)
    return x2, pj


def kernel(x_prompt, x_sample, c, cache_mla_ckv, cache_mla_krope, cache_gqa_k, cache_gqa_v, c_ctx, w_ada, b_ada, g_norm_mix, w_in, g_q_a, w_q_b, g_kv_a, w_kv_b, g_q_gqa, g_k_gqa, w_pool, pool_scale, w_br_a, w_br_b, w_br_c, w_out, g_norm_ffn, w_up, conv_w, conv_b, w_down, g_final):
    depth = w_in.shape[0]
    bc, tc, _ = x_prompt.shape
    bl, tl, _ = x_sample.shape
    past = cache_mla_ckv.shape[2]
    mod_rows = 8
    assert 1 + bl <= mod_rows and tc & (tc - 1) == 0 and tl & (tl - 1) == 0

    cin = jnp.concatenate([c_ctx[None, :], c, jnp.zeros((mod_rows - 1 - bl, D_MODEL), F32)], axis=0)
    mod4 = _ada_call(cin, w_ada, b_ada).reshape(depth, mod_rows, 1, 6 * D_MODEL)
    tabs = _rope_tables(tl)
    g_fin = g_final.reshape(1, D_MODEL)
    wl = _prep_weights(g_norm_mix, w_in, g_q_a, w_q_b, g_kv_a, w_kv_b, g_q_gqa, g_k_gqa, w_pool,
                       pool_scale, w_br_a, w_br_b, w_br_c, w_out, g_norm_ffn, w_up, conv_w, conv_b, w_down)
    kr_pad = jnp.pad(cache_mla_krope, ((0, 0), (0, 0), (0, 0), (MLA_NOPE, LANE - MLA_NOPE - MLA_ROPE)))
    caches = (cache_mla_ckv, kr_pad, cache_gqa_k.reshape(bl, depth, past, LANE),
              cache_gqa_v.reshape(bl, depth, past, LANE))

    xc = x_prompt.reshape(bc * tc, D_MODEL)
    xl = x_sample.reshape(bl * tl, D_MODEL)
    st_ckv, st_kr, st_k, st_v = [], [], [], []
    for l in range(depth):
        final = l == depth - 1
        xc, pj = _layer(xc, mod4, l, 0, bc, tc, False, final, wl, g_fin, None, None)
        st_ckv.append(pj["ckv_out"].reshape(bc, tc, KV_LORA))
        st_kr.append(pj["kr_out"].reshape(bc, tc, MLA_ROPE))
        st_k.append(pj["kg_out"].reshape(bc, tc, GQA_KV_HEADS, GQA_HEAD_DIM))
        st_v.append(pj["vg_out"].reshape(bc, tc, GQA_KV_HEADS, GQA_HEAD_DIM))
        xl, _ = _layer(xl, mod4, l, 1, bl, tl, True, final, wl, g_fin, tabs, caches)

    return (xc.reshape(bc, tc, D_MODEL), xl.reshape(bl, tl, D_MODEL),
            jnp.stack(st_ckv, axis=1), jnp.stack(st_kr, axis=1),
            jnp.stack(st_k, axis=1), jnp.stack(st_v, axis=1))
```

```python
import functools
import math

import numpy as np
import jax
import jax.numpy as jnp
from jax import lax
from jax.experimental import pallas as pl
from jax.experimental.pallas import tpu as pltpu

D_MODEL = 1024
GRID_W = 64
RMS_EPS = 1e-6
ROPE_THETA = 10000.0
MLA_HEADS = 8
MLA_NOPE = 64
MLA_ROPE = 32
MLA_V = 64
Q_LORA = 256
KV_LORA = 128
GQA_HEADS = 8
GQA_KV_HEADS = 2
GQA_HEAD_DIM = 64
POOL_WINDOWS = (2, 4, 8, 16)
POOL_GROUP = 128
POOL_WIDTH = POOL_GROUP * len(POOL_WINDOWS)
D_FF = 2816
N_BRANCH = 3

LANE = 128
HALO = 8
FF_CHUNK = 256
KEY_TILE = 256
CTX_GROUP = 1
CTX_Q_ROWS = 256
TOKEN_TILE = 512
Q_TILE = 512
LAT_PAIRS = 2
N_PAIRS = MLA_HEADS // 2
VMEM_LIMIT = 56 * 1024 * 1024

BF16 = jnp.bfloat16
F32 = jnp.float32

_OFF_QA = 0
_OFF_CKV = _OFF_QA + Q_LORA
_OFF_KR = _OFF_CKV + KV_LORA
_OFF_GQ = _OFF_KR + MLA_ROPE
_OFF_GK = _OFF_GQ + GQA_HEADS * GQA_HEAD_DIM
_OFF_GV = _OFF_GK + GQA_KV_HEADS * GQA_HEAD_DIM
_OFF_POOL = _OFF_GV + GQA_KV_HEADS * GQA_HEAD_DIM
_OFF_GATE = _OFF_POOL + POOL_WIDTH

_W1_QA = 0
_W1_CKV = 256
_W1_GQ = 384
_W1_GK = 896
_W1_GV = 1024
_W1_POOL = 1152
_W1_KR = 1664
_W1_COLS = 1792


def _const_spec(shape):
    nd = len(shape)
    return pl.BlockSpec(shape, lambda *_: (0,) * nd, pipeline_mode=pl.Buffered(1))


def _wspec(a, layer, cols=None, col_block=0):
    shape = a.shape[1:] if cols is None else a.shape[1:-1] + (cols,)
    idx = (layer,) + (0,) * (len(shape) - 1) + (col_block,)
    return pl.BlockSpec((None,) + shape, lambda *_: idx, pipeline_mode=pl.Buffered(1))


def _rms(x, g):
    ms = jnp.mean(x * x, axis=-1, keepdims=True)
    return x * lax.rsqrt(ms + RMS_EPS) * g


def _dot(a, b):
    return jnp.dot(a, b, preferred_element_type=F32)


def _rope_head_tables(n_tokens, d):
    dim_axis = d // 2
    t = np.arange(n_tokens)
    row = (t // GRID_W).astype(np.float64)
    col = (t % GRID_W).astype(np.float64)
    freqs = ROPE_THETA ** (-np.arange(0, dim_axis, 2, dtype=np.float64) / dim_axis)
    ar = row[:, None] * freqs[None, :]
    ac = col[:, None] * freqs[None, :]
    cos = np.concatenate([np.cos(ar), np.cos(ar), np.cos(ac), np.cos(ac)], axis=1)
    sin = np.concatenate([-np.sin(ar), np.sin(ar), -np.sin(ac), np.sin(ac)], axis=1)
    return cos, sin


def _rope_tables(n_tokens):
    cg, sg = _rope_head_tables(n_tokens, GQA_HEAD_DIM)
    cg = np.concatenate([cg, cg], axis=1)
    sg = np.concatenate([sg, sg], axis=1)
    cm32, sm32 = _rope_head_tables(n_tokens, MLA_ROPE)
    pad = LANE - MLA_NOPE - MLA_ROPE
    cm = np.concatenate([np.ones((n_tokens, MLA_NOPE)), cm32, np.zeros((n_tokens, pad))], axis=1)
    sm = np.concatenate([np.zeros((n_tokens, MLA_NOPE)), sm32, np.zeros((n_tokens, pad))], axis=1)
    return tuple(jnp.asarray(a, F32) for a in (cg, sg, cm, sm))


def _ada_body(c_ref, w_ref, b_ref, o_ref):
    c = c_ref[...]
    s = (c * jax.nn.sigmoid(c)).astype(BF16)
    o_ref[0] = _dot(s, w_ref[0].astype(BF16)) + b_ref[0]


def _ada_call(cin, w_ada, b_ada):
    depth, d, n = w_ada.shape
    cols = 2048
    rows = cin.shape[0]
    return pl.pallas_call(
        _ada_body,
        grid=(depth, n // cols),
        in_specs=[
            pl.BlockSpec((rows, d), lambda l, j: (0, 0)),
            pl.BlockSpec((1, d, cols), lambda l, j: (l, 0, j)),
            pl.BlockSpec((1, 1, cols), lambda l, j: (l, 0, j)),
        ],
        out_specs=pl.BlockSpec((1, rows, cols), lambda l, j: (l, 0, j)),
        out_shape=jax.ShapeDtypeStruct((depth, rows, n), F32),
        compiler_params=pltpu.CompilerParams(
            dimension_semantics=("arbitrary", "arbitrary"), vmem_limit_bytes=VMEM_LIMIT),
        name="ada_mod",
    )(cin, w_ada, b_ada.reshape(depth, 1, n))


def _two_head_rsqrt(xb, lo):
    x2 = xb * xb
    s_lo = jnp.sum(jnp.where(lo, x2, 0.0), axis=-1, keepdims=True)
    s_hi = jnp.sum(jnp.where(lo, 0.0, x2), axis=-1, keepdims=True)
    ms = jnp.where(lo, s_lo, s_hi) * (1.0 / GQA_HEAD_DIM)
    return lax.rsqrt(ms + RMS_EPS)


def _rot_partner_lanes(xb, lane, quarter):
    first = jnp.bitwise_and(lane, quarter) == 0
    return jnp.where(first, pltpu.roll(xb, LANE - quarter, axis=1), pltpu.roll(xb, quarter, axis=1))


def _store_value_heads(v_ref, first, blk, lo):
    v_ref[first] = jnp.where(lo, blk, 1.0).astype(BF16)
    v_ref[first + 1] = jnp.where(lo, 1.0, blk).astype(BF16)


def _inproj_body(latent, names, *refs):
    r = dict(zip(names, refs))
    x = r["x"][...]
    tm = x.shape[0]
    mod = r["mod"][0]
    sh1 = mod[:, 0:D_MODEL]
    sc1 = mod[:, D_MODEL:2 * D_MODEL]
    h = (_rms(x, r["gmix"][...]) * (1.0 + sc1) + sh1).astype(BF16)
    z = _dot(h, r["w1"][...])

    lane = lax.broadcasted_iota(jnp.int32, (tm, LANE), 1)
    lo = lane < GQA_HEAD_DIM

    if latent:
        cg = r["cg"][...]
        sg = r["sg"][...]
        cm = r["cm"][...]
        sm = r["sm"][...]

    qn = _rms(z[:, _W1_QA:_W1_QA + Q_LORA], r["gqa"][...]).astype(BF16)
    qm = _dot(qn, r["wqb"][...])
    if latent:
        qms = _dot(qn, r["wqbsw"][...])
    for hh in range(MLA_HEADS):
        blk = qm[:, hh * LANE:(hh + 1) * LANE]
        if latent:
            blk = blk * cm + qms[:, hh * LANE:(hh + 1) * LANE] * sm
        r["q_mla"][hh] = blk.astype(BF16)

    ckv = _rms(z[:, _W1_CKV:_W1_CKV + KV_LORA], r["gkva"][...])
    kr = z[:, _W1_KR:_W1_KR + LANE]
    if latent:
        kr = kr * cm + _rot_partner_lanes(kr, lane, MLA_ROPE // 4) * sm
    else:
        r["ckv_out"][...] = ckv
        r["kr_out"][...] = kr[:, MLA_NOPE:MLA_NOPE + MLA_ROPE]
    ckv_b = ckv.astype(BF16)
    kfull = _dot(jnp.concatenate([ckv_b, kr.astype(BF16)], axis=1), r["wkp"][...])
    for hh in range(MLA_HEADS):
        r["k_mla"][hh] = kfull[:, hh * LANE:(hh + 1) * LANE].astype(BF16)
    vfull = _dot(ckv_b, r["wv"][...])
    for pp in range(N_PAIRS):
        _store_value_heads(r["v_mla"], 2 * pp, vfull[:, pp * LANE:(pp + 1) * LANE], lo)

    gq2 = r["gq2"][...]
    q_scale = GQA_HEAD_DIM ** -0.5
    for j in range(N_PAIRS):
        xb = z[:, _W1_GQ + j * LANE:_W1_GQ + (j + 1) * LANE]
        rs = _two_head_rsqrt(xb, lo)
        y = xb * rs * gq2
        if latent:
            y = y * cg + _rot_partner_lanes(y, lane, GQA_HEAD_DIM // 4) * sg
        y = y * q_scale
        r["q_gqa"][2 * j] = jnp.where(lo, y, 0.0).astype(BF16)
        r["q_gqa"][2 * j + 1] = jnp.where(lo, 0.0, y).astype(BF16)
    kb = z[:, _W1_GK:_W1_GK + LANE]
    rs = _two_head_rsqrt(kb, lo)
    kg = kb * rs * r["gk2"][...]
    if latent:
        kg = kg * cg + _rot_partner_lanes(kg, lane, GQA_HEAD_DIM // 4) * sg
    vg = z[:, _W1_GV:_W1_GV + LANE]
    if not latent:
        r["kg_out"][...] = kg
        r["vg_out"][...] = vg
    r["k_gqa"][...] = kg.astype(BF16)
    _store_value_heads(r["v_gqa"], 0, vg, lo)

    r["pool"][...] = z[:, _W1_POOL:_W1_POOL + POOL_WIDTH]


def _inproj_call(x, mod4, layer, mod_base, seq_len, tm, latent, wl, tabs):
    n = x.shape[0]
    nt = n // tm
    per_seq = seq_len // tm if latent else 1

    def tile(i):
        return (i, 0)

    def mod_idx(i):
        if latent:
            return (layer, mod_base + i // per_seq, 0, 0)
        return (layer, mod_base, 0, 0)

    names = ["x", "mod", "w1", "gmix", "gqa", "wqb", "gkva", "wkp", "wv", "gq2", "gk2"]
    args = [x, mod4, wl["w1"]] + [wl[nm] for nm in names[3:]]
    specs = [pl.BlockSpec((tm, D_MODEL), tile), pl.BlockSpec((None, 1, 1, 6 * D_MODEL), mod_idx),
             _wspec(wl["w1"], layer)]
    specs += [_wspec(a, layer) for a in args[3:]]
    if latent:
        names.append("wqbsw")
        args.append(wl["wqbsw"])
        specs.append(_wspec(wl["wqbsw"], layer))
        for nm, a in zip(("cg", "sg", "cm", "sm"), tabs):
            names.append(nm)
            args.append(a)
            specs.append(pl.BlockSpec((tm, LANE), lambda i: (i % per_seq, 0)))

    head_spec = pl.BlockSpec((MLA_HEADS, tm, LANE), lambda i: (0, i, 0))
    kvh_spec = pl.BlockSpec((GQA_KV_HEADS, tm, LANE), lambda i: (0, i, 0))
    out_names = ["q_mla", "k_mla", "v_mla", "q_gqa", "k_gqa", "v_gqa", "pool"]
    out_shapes = [
        jax.ShapeDtypeStruct((MLA_HEADS, n, LANE), BF16),
        jax.ShapeDtypeStruct((MLA_HEADS, n, LANE), BF16),
        jax.ShapeDtypeStruct((MLA_HEADS, n, LANE), BF16),
        jax.ShapeDtypeStruct((GQA_HEADS, n, LANE), BF16),
        jax.ShapeDtypeStruct((n, LANE), BF16),
        jax.ShapeDtypeStruct((GQA_KV_HEADS, n, LANE), BF16),
        jax.ShapeDtypeStruct((n, POOL_WIDTH), F32),
    ]
    out_specs = [head_spec, head_spec, head_spec, head_spec,
                 pl.BlockSpec((tm, LANE), tile), kvh_spec,
                 pl.BlockSpec((tm, POOL_WIDTH), tile)]
    if not latent:
        out_names += ["ckv_out", "kr_out", "kg_out", "vg_out"]
        out_shapes += [jax.ShapeDtypeStruct((n, KV_LORA), F32), jax.ShapeDtypeStruct((n, MLA_ROPE), F32),
                       jax.ShapeDtypeStruct((n, LANE), F32), jax.ShapeDtypeStruct((n, LANE), F32)]
        out_specs += [pl.BlockSpec((tm, KV_LORA), tile), pl.BlockSpec((tm, MLA_ROPE), tile),
                      pl.BlockSpec((tm, LANE), tile), pl.BlockSpec((tm, LANE), tile)]

    outs = pl.pallas_call(
        functools.partial(_inproj_body, latent, names + out_names),
        grid=(nt,),
        in_specs=specs,
        out_specs=out_specs,
        out_shape=out_shapes,
        compiler_params=pltpu.CompilerParams(
            dimension_semantics=("arbitrary",), vmem_limit_bytes=VMEM_LIMIT),
        name="inproj_lat" if latent else "inproj_ctx",
    )(*args)
    return dict(zip(out_names, outs))


def _cachekv_body(ckv_ref, kr_ref, kg_ref, vg_ref, wkp_ref, wv_ref, k_ref, v_ref, kgo_ref, vgo_ref):
    rows = ckv_ref.shape[0] * ckv_ref.shape[1]
    ckv_b = ckv_ref[...].reshape(rows, LANE).astype(BF16)
    lo = lax.broadcasted_iota(jnp.int32, (rows, LANE), 1) < MLA_V
    kr_b = kr_ref[...].reshape(rows, LANE).astype(BF16)
    kfull = _dot(jnp.concatenate([ckv_b, kr_b], axis=1), wkp_ref[...])
    kgo_ref[0] = kg_ref[...].reshape(rows, LANE).astype(BF16)
    for hh in range(MLA_HEADS):
        k_ref[hh] = kfull[:, hh * LANE:(hh + 1) * LANE].astype(BF16)
    vfull = _dot(ckv_b, wv_ref[...])
    for pp in range(N_PAIRS):
        _store_value_heads(v_ref, 2 * pp, vfull[:, pp * LANE:(pp + 1) * LANE], lo)
    _store_value_heads(vgo_ref, 0, vg_ref[...].reshape(rows, LANE), lo)


def _cachekv_call(caches, layer, wl):
    bsz, _, past, _ = caches[0].shape
    n = bsz * past
    cache_spec = pl.BlockSpec((bsz, None, past, LANE), lambda i: (0, layer, 0, 0))
    return pl.pallas_call(
        _cachekv_body,
        grid=(1,),
        in_specs=[cache_spec] * 4 + [_wspec(wl["wkp"], layer), _wspec(wl["wv"], layer)],
        out_specs=[_const_spec((MLA_HEADS, n, LANE)), _const_spec((MLA_HEADS, n, LANE)),
                   _const_spec((1, n, LANE)), _const_spec((GQA_KV_HEADS, n, LANE))],
        out_shape=[jax.ShapeDtypeStruct((MLA_HEADS, n, LANE), BF16),
                   jax.ShapeDtypeStruct((MLA_HEADS, n, LANE), BF16),
                   jax.ShapeDtypeStruct((1, n, LANE), BF16),
                   jax.ShapeDtypeStruct((GQA_KV_HEADS, n, LANE), BF16)],
        compiler_params=pltpu.CompilerParams(
            dimension_semantics=("arbitrary",), vmem_limit_bytes=VMEM_LIMIT),
        name="cache_kv",
    )(*caches, wl["wkp"], wl["wv"])


def _normalise_pair(acc_a, acc_b, lo):
    num = jnp.where(lo, acc_a, acc_b)
    den = pltpu.roll(jnp.where(lo, acc_b, acc_a), MLA_V, axis=1)
    return num / den


def _attn_body(group, seq_len, shared_kv, exp_scale, q_ref, k_ref, v_ref, o_ref):
    lo = lax.broadcasted_iota(jnp.int32, (CTX_Q_ROWS, LANE), 1) < MLA_V
    nt_dims = (((1,), (1,)), ((), ()))
    units = [(g, pr, qb, e) for g in range(group) for pr in range(N_PAIRS)
             for qb in range(seq_len // CTX_Q_ROWS) for e in range(2)]

    def unit_rows(u):
        g, _, qb, _ = u
        q0 = g * seq_len + qb * CTX_Q_ROWS
        return slice(g * seq_len, (g + 1) * seq_len), slice(q0, q0 + CTX_Q_ROWS)

    scores, row_max, accs = {}, {}, {}
    for i in range(len(units) + 2):
        if i < len(units):
            _, pr, _, e = units[i]
            rows, qrows = unit_rows(units[i])
            q = q_ref[2 * pr + e, qrows, :]
            k = k_ref[0 if shared_kv else 2 * pr + e, rows, :]
            scores[i] = lax.dot_general(q, k, nt_dims, preferred_element_type=F32)
        if 0 <= i - 1 < len(units):
            row_max[i - 1] = jnp.max(scores[i - 1], axis=-1, keepdims=True)
        if 0 <= i - 2 < len(units):
            u = i - 2
            _, pr, _, e = units[u]
            rows, qrows = unit_rows(units[u])
            v = v_ref[e if shared_kv else 2 * pr + e, rows, :]
            p = jnp.exp2((scores.pop(u) - row_max.pop(u)) * exp_scale)
            accs[u] = _dot(p.astype(BF16), v)
            if e == 1:
                o_ref[pr, qrows, :] = _normalise_pair(accs.pop(u - 1), accs.pop(u), lo).astype(BF16)


def _attn_call(q, k, v, batch, seq_len, group, scale, shared_kv, name):
    n = q.shape[1]
    rows = group * seq_len
    exp_scale = scale * math.log2(math.e)

    def spec(heads):
        return pl.BlockSpec((heads, rows, LANE), lambda b: (0, b, 0))

    return pl.pallas_call(
        functools.partial(_attn_body, group, seq_len, shared_kv, exp_scale),
        grid=(batch // group,),
        in_specs=[spec(q.shape[0]), spec(k.shape[0]), spec(v.shape[0])],
        out_specs=pl.BlockSpec((N_PAIRS, rows, LANE), lambda b: (0, b, 0)),
        out_shape=jax.ShapeDtypeStruct((N_PAIRS, n, LANE), BF16),
        compiler_params=pltpu.CompilerParams(
            dimension_semantics=("arbitrary",), vmem_limit_bytes=VMEM_LIMIT),
        name=name,
    )(q, k, v)


def _attn_pipe_body(shared_kv, exp_scale, tq, pairs, q_ref, k_ref, v_ref, kc_ref, vc_ref, o_ref,
                    s0, s1, mr0, mr1, ac0, ac1):
    seq = k_ref.shape[1]
    past = kc_ref.shape[1]
    nq = seq // tq
    n_units = pairs * nq
    n_new = seq // KEY_TILE
    n_tiles = n_new + past // KEY_TILE
    s_scr = (s0, s1)
    mrun_scr = (mr0, mr1)
    acc_scr = (ac0, ac1)
    nt_dims = (((1,), (1,)), ((), ()))
    lo = lax.broadcasted_iota(jnp.int32, (tq, LANE), 1) < MLA_V

    def pair_rows(t):
        j = t % nq
        return t // nq, pl.ds(pl.multiple_of(j * tq, tq), tq)

    def kv_tile(new_ref, cache_ref, head, blk):
        if blk < n_new:
            return new_ref[head, blk * KEY_TILE:(blk + 1) * KEY_TILE, :]
        blk -= n_new
        return cache_ref[head, blk * KEY_TILE:(blk + 1) * KEY_TILE, :]

    def finish_prev(e_prev, t_prev):
        if e_prev == 1:
            pair, rows = pair_rows(t_prev)
            o_ref[pair, rows, :] = _normalise_pair(ac0[...], ac1[...], lo).astype(BF16)

    def region(t_scores, e_scores, t_cur, e_cur, t_prev):
        if t_prev is not None:
            finish_prev(1 - e_cur, t_prev)
        if e_scores is not None:
            pair, rows = pair_rows(t_scores)
            q = q_ref[2 * pair + e_scores, rows, :]
            k_head = 0 if shared_kv else 2 * pair + e_scores
        if e_cur is not None:
            v_head = e_cur if shared_kv else 2 * (t_cur // nq) + e_cur
            m = jnp.max(mrun_scr[e_cur][...], axis=-1, keepdims=True)
            m_b = jnp.broadcast_to(m, (tq, LANE))
        acc = None
        for blk in range(n_tiles):
            cols = slice(blk * KEY_TILE, (blk + 1) * KEY_TILE)
            if e_scores is not None:
                s_t = lax.dot_general(q, kv_tile(k_ref, kc_ref, k_head, blk), nt_dims,
                                      preferred_element_type=F32)
                s_scr[e_scores][:, cols] = s_t
                m_t = jnp.maximum(s_t[:, 0:LANE], s_t[:, LANE:2 * LANE])
                if blk > 0:
                    m_t = jnp.maximum(m_t, mrun_scr[e_scores][...])
                mrun_scr[e_scores][...] = m_t
            if e_cur is not None:
                p_parts = []
                for hh in range(KEY_TILE // LANE):
                    c0 = blk * KEY_TILE + hh * LANE
                    s_h = s_scr[e_cur][:, c0:c0 + LANE]
                    p_parts.append(jnp.exp2((s_h - m_b) * exp_scale).astype(BF16))
                p_t = jnp.concatenate(p_parts, axis=1)
                part = _dot(p_t, kv_tile(v_ref, vc_ref, v_head, blk))
                acc = part if acc is None else acc + part
        if e_cur is not None:
            acc_scr[e_cur][...] = acc

    region(0, 0, None, None, None)
    region(0, 1, 0, 0, None)
    region(1, 0, 0, 1, 0)

    def body(t, carry):
        @pl.when(t > 0)
        def _():
            region(t, 1, t, 0, t - 1)

        @pl.when(t < n_units)
        def _():
            region(t + 1, 0, t, 1, t)

        return carry

    lax.fori_loop(1, n_units - 1, body, 0)
    region(n_units - 1, 1, n_units - 1, 0, n_units - 2)
    region(None, None, n_units - 1, 1, n_units - 1)
    finish_prev(1, n_units - 1)


def _attn_pipe_call(q, k, v, kc, vc, batch, seq_len, tq, scale, shared_kv, name, pairs=LAT_PAIRS):
    n = q.shape[1]
    past = kc.shape[1] // batch
    exp_scale = scale * math.log2(math.e)

    def spec(a, rows):
        heads = a.shape[0]
        if heads < 2 * N_PAIRS:
            return pl.BlockSpec((heads, rows, LANE), lambda b, p: (0, b, 0))
        return pl.BlockSpec((2 * pairs, rows, LANE), lambda b, p: (p, b, 0))

    total = seq_len + past
    return pl.pallas_call(
        functools.partial(_attn_pipe_body, shared_kv, exp_scale, tq, pairs),
        grid=(batch, N_PAIRS // pairs),
        in_specs=[spec(q, seq_len), spec(k, seq_len), spec(v, seq_len), spec(kc, past), spec(vc, past)],
        out_specs=pl.BlockSpec((pairs, seq_len, LANE), lambda b, p: (p, b, 0)),
        out_shape=jax.ShapeDtypeStruct((N_PAIRS, n, LANE), BF16),
        scratch_shapes=[pltpu.VMEM((tq, total), F32), pltpu.VMEM((tq, total), F32)]
        + [pltpu.VMEM((tq, LANE), F32) for _ in range(4)],
        compiler_params=pltpu.CompilerParams(
            dimension_semantics=("arbitrary", "arbitrary"), vmem_limit_bytes=VMEM_LIMIT),
        name=name,
    )(q, k, v, kc, vc)


def _seq_pos(tm, seq_len):
    i = pl.program_id(0)
    row = lax.broadcasted_iota(jnp.int32, (tm, 1), 0) + i * tm
    return jnp.bitwise_and(row, seq_len - 1)


def _merge_body(seq_len, x_ref, mod_ref, gmix_ref, wg_ref, a_ref, b_ref, pc_ref, pl_ref, pr_ref,
                wpool_ref, pscale_ref, wa_ref, wb_ref, wc_ref, wo_ref, o_ref, merged_scr, ab_scr):
    x = x_ref[...]
    tm = x.shape[0]
    mod = mod_ref[0]
    sh1 = mod[:, 0:D_MODEL]
    sc1 = mod[:, D_MODEL:2 * D_MODEL]
    g1 = mod[:, 2 * D_MODEL:3 * D_MODEL]
    h = (_rms(x, gmix_ref[...]) * (1.0 + sc1) + sh1).astype(BF16)

    pos = _seq_pos(tm, seq_len)
    ext = jnp.concatenate([pl_ref[...], pc_ref[...], pr_ref[...]], axis=0)
    pscale = pscale_ref[...]

    def pool_group(g):
        w = POOL_WINDOWS[g]
        eg = ext[:, g * POOL_GROUP:(g + 1) * POOL_GROUP]
        centre = eg[HALO:HALO + tm]
        wsum = centre
        for dlt in range(-(w // 2), w - w // 2):
            if dlt == 0:
                continue
            valid = pos >= -dlt if dlt < 0 else pos < seq_len - dlt
            wsum = wsum + jnp.where(valid, eg[HALO + dlt:HALO + dlt + tm], 0.0)
        cnt = jnp.minimum(pos + (w - w // 2), seq_len) - jnp.maximum(pos - w // 2, 0)
        return (wsum / cnt.astype(F32) - centre).astype(BF16)

    def pool_pair_map(p, pooled_a, pooled_b):
        both = jnp.concatenate([pooled_a, pooled_b], axis=1)
        return _dot(both, wpool_ref[p]) * pscale[:, 2 * p * POOL_GROUP:2 * (p + 1) * POOL_GROUP]

    def gated(branch, src, w_ref, j):
        c0 = branch * D_MODEL + j * FF_CHUNK
        gate = jax.nn.sigmoid(_dot(h, wg_ref[:, c0:c0 + FF_CHUNK]))
        return gate * _dot(src, w_ref[:, j * FF_CHUNK:(j + 1) * FF_CHUNK])

    a_out = jnp.concatenate([a_ref[p] for p in range(N_PAIRS)], axis=1)
    b_out = jnp.concatenate([b_ref[p] for p in range(N_PAIRS)], axis=1)
    n_chunks = D_MODEL // FF_CHUNK
    pooled = []
    c_parts = []
    for j in range(n_chunks):
        cols = slice(j * FF_CHUNK, (j + 1) * FF_CHUNK)
        for g in range(j * len(POOL_WINDOWS) // n_chunks, (j + 1) * len(POOL_WINDOWS) // n_chunks):
            pooled.append(pool_group(g))
            if g % 2 == 1:
                c_parts.append(pool_pair_map(g // 2, pooled[g - 1], pooled[g]))
        ab_scr[:, cols] = gated(0, a_out, wa_ref, j) + gated(1, b_out, wb_ref, j)
    c_out = jnp.concatenate(c_parts, axis=-1).astype(BF16)
    for j in range(n_chunks):
        cols = slice(j * FF_CHUNK, (j + 1) * FF_CHUNK)
        merged_scr[:, cols] = (ab_scr[:, cols] + gated(2, c_out, wc_ref, j)).astype(BF16)
    o_ref[...] = x + g1 * _dot(merged_scr[...], wo_ref[...])


def _halo_specs(tm, width, n):
    blocks = tm // HALO
    last = n // HALO - 1
    left = pl.BlockSpec((HALO, width), lambda i: (jnp.maximum(i * blocks - 1, 0), 0))
    right = pl.BlockSpec((HALO, width), lambda i: (jnp.minimum((i + 1) * blocks, last), 0))
    return left, right


def _mod_spec(layer, mod_base, per_seq, latent):
    if latent:
        return pl.BlockSpec((None, 1, 1, 6 * D_MODEL), lambda i: (layer, mod_base + i // per_seq, 0, 0))
    return pl.BlockSpec((None, 1, 1, 6 * D_MODEL), lambda i: (layer, mod_base, 0, 0))


def _merge_call(x, mod4, layer, mod_base, seq_len, tm, latent, a_out, b_out, pool_in, wl):
    n = x.shape[0]
    per_seq = seq_len // tm if latent else 1

    def tile(i):
        return (i, 0)

    left, right = _halo_specs(tm, POOL_WIDTH, n)
    consts = [wl["wpool"], wl["pscale"], wl["wa"], wl["wb"], wl["wc"], wl["wo"]]
    in_specs = [pl.BlockSpec((tm, D_MODEL), tile), _mod_spec(layer, mod_base, per_seq, latent),
                _wspec(wl["gmix"], layer), _wspec(wl["wg"], layer),
                pl.BlockSpec((N_PAIRS, tm, LANE), lambda i: (0, i, 0)),
                pl.BlockSpec((N_PAIRS, tm, LANE), lambda i: (0, i, 0)),
                pl.BlockSpec((tm, POOL_WIDTH), tile), left, right]
    in_specs += [_wspec(a, layer) for a in consts]
    return pl.pallas_call(
        functools.partial(_merge_body, seq_len),
        grid=(n // tm,),
        in_specs=in_specs,
        out_specs=pl.BlockSpec((tm, D_MODEL), tile),
        out_shape=jax.ShapeDtypeStruct((n, D_MODEL), F32),
        scratch_shapes=[pltpu.VMEM((tm, D_MODEL), BF16), pltpu.VMEM((tm, D_MODEL), F32)],
        compiler_params=pltpu.CompilerParams(
            dimension_semantics=("arbitrary",), vmem_limit_bytes=VMEM_LIMIT),
        name="merge_lat" if latent else "merge_ctx",
    )(x, mod4, wl["gmix"], wl["wg"], a_out, b_out, pool_in, pool_in, pool_in, *consts)


def _ffn_body(seq_len, final, x_ref, xl_ref, xr_ref, mod_ref, gffn_ref, wug_ref, wuv_ref, cw_ref, cb_ref,
              wd_ref, gfin_ref, o_ref, act_scr):
    x = x_ref[...]
    tm = x.shape[0]
    mod = mod_ref[0]
    sh2 = mod[:, 3 * D_MODEL:4 * D_MODEL]
    sc2 = mod[:, 4 * D_MODEL:5 * D_MODEL]
    g2 = mod[:, 5 * D_MODEL:6 * D_MODEL]
    xe = jnp.concatenate([xl_ref[...], x, xr_ref[...]], axis=0)
    h2e = _rms(xe, gffn_ref[...]) * (1.0 + sc2) + sh2
    h2 = h2e[HALO:HALO + tm].astype(BF16)
    h2e = h2e.astype(BF16)

    pos = _seq_pos(tm, seq_len)
    has_prev = pos >= 1
    has_next = pos <= seq_len - 2
    cw = cw_ref[...]
    cb = cb_ref[...]
    for j in range(D_FF // FF_CHUNK):
        cols = slice(j * FF_CHUNK, (j + 1) * FF_CHUNK)
        ge = _dot(h2e, wug_ref[:, cols])
        g = (jnp.where(has_prev, ge[HALO - 1:HALO - 1 + tm], 0.0) * cw[0:1, cols]
             + ge[HALO:HALO + tm] * cw[1:2, cols]
             + jnp.where(has_next, ge[HALO + 1:HALO + 1 + tm], 0.0) * cw[2:3, cols]
             + cb[:, cols])
        val = _dot(h2, wuv_ref[:, cols])
        act_scr[:, cols] = (g * jax.nn.sigmoid(g) * val).astype(BF16)
    y = x + g2 * _dot(act_scr[...], wd_ref[...])
    if final:
        y = _rms(y, gfin_ref[...])
    o_ref[...] = y


def _ffn_call(x, mod4, layer, mod_base, seq_len, tm, latent, final, wl, g_final):
    n = x.shape[0]
    per_seq = seq_len // tm if latent else 1

    def tile(i):
        return (i, 0)

    left, right = _halo_specs(tm, D_MODEL, n)
    consts = [wl["gffn"], wl["wu"], wl["wu"], wl["cw"], wl["cb"], wl["wd"], g_final]
    in_specs = [pl.BlockSpec((tm, D_MODEL), tile), left, right, _mod_spec(layer, mod_base, per_seq, latent),
                _wspec(wl["gffn"], layer), _wspec(wl["wu"], layer, cols=D_FF, col_block=0),
                _wspec(wl["wu"], layer, cols=D_FF, col_block=1), _wspec(wl["cw"], layer),
                _wspec(wl["cb"], layer), _wspec(wl["wd"], layer), _const_spec(g_final.shape)]
    return pl.pallas_call(
        functools.partial(_ffn_body, seq_len, final),
        grid=(n // tm,),
        in_specs=in_specs,
        out_specs=pl.BlockSpec((tm, D_MODEL), tile),
        out_shape=jax.ShapeDtypeStruct((n, D_MODEL), F32),
        scratch_shapes=[pltpu.VMEM((tm, D_FF), BF16)],
        compiler_params=pltpu.CompilerParams(
            dimension_semantics=("arbitrary",), vmem_limit_bytes=VMEM_LIMIT),
        name="ffn_lat" if latent else "ffn_ctx",
    )(x, x, x, mod4, *consts)


def _rot_partner(a):
    d = a.shape[-1]
    return jnp.flip(a.reshape(a.shape[:-1] + (2, 2, d // 4)), axis=-2).reshape(a.shape)


def _prep_weights(g_norm_mix, w_in, g_q_a, w_q_b, g_kv_a, w_kv_b, g_q_gqa, g_k_gqa, w_pool, pool_scale,
                  w_br_a, w_br_b, w_br_c, w_out, g_norm_ffn, w_up, conv_w, conv_b, w_down):
    depth = w_in.shape[0]
    hd = GQA_HEAD_DIM
    grp = GQA_HEADS // GQA_KV_HEADS

    def lane_pad(a, before, width=LANE):
        cfg = [(0, 0)] * (a.ndim - 1) + [(before, width - before - a.shape[-1])]
        return jnp.pad(a, cfg)

    def pair_heads(a):
        lead = a.shape[:-1]
        return a.reshape(lead + (GQA_KV_HEADS, grp, hd)).swapaxes(-3, -2).reshape(lead + (GQA_HEADS * hd,))

    w1 = jnp.concatenate(
        [w_in[:, :, _OFF_QA:_OFF_KR], pair_heads(w_in[:, :, _OFF_GQ:_OFF_GK]), w_in[:, :, _OFF_GK:_OFF_GATE],
         lane_pad(w_in[:, :, _OFF_KR:_OFF_GQ], MLA_NOPE)], axis=-1).astype(BF16)
    assert w1.shape[-1] == _W1_COLS

    qb = w_q_b.reshape(depth, Q_LORA, MLA_HEADS, MLA_NOPE + MLA_ROPE)
    wqb = lane_pad(qb, 0).reshape(depth, Q_LORA, MLA_HEADS * LANE).astype(BF16)
    wqbsw = lane_pad(_rot_partner(qb[..., MLA_NOPE:]), MLA_NOPE).reshape(
        depth, Q_LORA, MLA_HEADS * LANE).astype(BF16)

    kvb = w_kv_b.reshape(depth, KV_LORA, MLA_HEADS, MLA_NOPE + MLA_V)
    wk = lane_pad(kvb[..., :MLA_NOPE], 0).reshape(depth, KV_LORA, MLA_HEADS * LANE).astype(BF16)
    wv = kvb[..., MLA_NOPE:].reshape(depth, KV_LORA, MLA_HEADS * MLA_V).astype(BF16)

    pk = np.zeros((LANE, MLA_HEADS * LANE), np.float32)
    for hh in range(MLA_HEADS):
        for t in range(MLA_ROPE):
            pk[MLA_NOPE + t, hh * LANE + MLA_NOPE + t] = 1.0
    wkp = jnp.concatenate([wk, jnp.broadcast_to(jnp.asarray(pk, BF16), (depth,) + pk.shape)], axis=1)

    wp = w_pool.reshape(depth, len(POOL_WINDOWS) // 2, 2, POOL_GROUP, POOL_GROUP)
    zero = jnp.zeros_like(wp[:, :, 0])
    wpool2 = jnp.concatenate([jnp.concatenate([wp[:, :, 0], zero], axis=-1),
                              jnp.concatenate([zero, wp[:, :, 1]], axis=-1)], axis=-2).astype(BF16)

    row = lambda a: a[:, None, :]
    two = lambda a: jnp.concatenate([a, a], axis=-1)
    wb = w_br_b.reshape(depth, GQA_KV_HEADS, grp, hd, D_MODEL).swapaxes(1, 2).reshape(depth, GQA_HEADS * hd, D_MODEL)
    return dict(
        gmix=row(g_norm_mix), w1=w1, gqa=row(g_q_a), wqb=wqb, wqbsw=wqbsw, gkva=row(g_kv_a), wkp=wkp,
        wv=wv, gq2=row(two(g_q_gqa)), gk2=row(two(g_k_gqa)),
        wg=w_in[:, :, _OFF_GATE:].astype(BF16), wpool=wpool2, pscale=row(pool_scale),
        wa=w_br_a.astype(BF16), wb=wb.astype(BF16), wc=w_br_c.astype(BF16), wo=w_out.astype(BF16),
        gffn=row(g_norm_ffn), wu=w_up.astype(BF16), cw=conv_w, cb=row(conv_b), wd=w_down.astype(BF16),
    )


def _layer(x, mod4, layer, mod_base, batch, seq_len, latent, final, wl, g_final, tabs, caches):
    tm = TOKEN_TILE
    tq = Q_TILE
    assert tm % seq_len == 0 or seq_len % tm == 0
    pj = _inproj_call(x, mod4, layer, mod_base, seq_len, tm, latent, wl, tabs)
    if caches is not None:
        kc_m, vc_m, kc_g, vc_g = _cachekv_call(caches, layer, wl)
    mla_scale = (MLA_NOPE + MLA_ROPE) ** -0.5
    n = x.shape[0]
    k_g = pj["k_gqa"].reshape(1, n, LANE)
    if latent:
        a_out = _attn_pipe_call(pj["q_mla"], pj["k_mla"], pj["v_mla"], kc_m, vc_m, batch, seq_len, tq,
                                mla_scale, False, "attn_mla_lat")
        b_out = _attn_pipe_call(pj["q_gqa"], k_g, pj["v_gqa"], kc_g, vc_g, batch, seq_len, tq, 1.0, True,
                                "attn_gqa_lat")
    else:
        a_out = _attn_call(pj["q_mla"], pj["k_mla"], pj["v_mla"], batch, seq_len, CTX_GROUP,
                           mla_scale, False, "attn_mla_ctx")
        b_out = _attn_call(pj["q_gqa"], k_g, pj["v_gqa"], batch, seq_len, CTX_GROUP, 1.0, True,
                           "attn_gqa_ctx")
    x1 = _merge_call(x, mod4, layer, mod_base, seq_len, tm, latent, a_out, b_out, pj["pool"], wl)
    x2 = _ffn_call(x1, mod4, layer, mod_base, seq_len, tm, latent, final, wl, g_final)
    return x2, pj


def kernel(x_prompt, x_sample, c, cache_mla_ckv, cache_mla_krope, cache_gqa_k, cache_gqa_v, c_ctx, w_ada, b_ada, g_norm_mix, w_in, g_q_a, w_q_b, g_kv_a, w_kv_b, g_q_gqa, g_k_gqa, w_pool, pool_scale, w_br_a, w_br_b, w_br_c, w_out, g_norm_ffn, w_up, conv_w, conv_b, w_down, g_final):
    depth = w_in.shape[0]
    bc, tc, _ = x_prompt.shape
    bl, tl, _ = x_sample.shape
    past = cache_mla_ckv.shape[2]
    mod_rows = 8
    assert 1 + bl <= mod_rows and tc & (tc - 1) == 0 and tl & (tl - 1) == 0

    cin = jnp.concatenate([c_ctx[None, :], c, jnp.zeros((mod_rows - 1 - bl, D_MODEL), F32)], axis=0)
    mod4 = _ada_call(cin, w_ada, b_ada).reshape(depth, mod_rows, 1, 6 * D_MODEL)
    tabs = _rope_tables(tl)
    g_fin = g_final.reshape(1, D_MODEL)
    wl = _prep_weights(g_norm_mix, w_in, g_q_a, w_q_b, g_kv_a, w_kv_b, g_q_gqa, g_k_gqa, w_pool,
                       pool_scale, w_br_a, w_br_b, w_br_c, w_out, g_norm_ffn, w_up, conv_w, conv_b, w_down)
    kr_pad = jnp.pad(cache_mla_krope, ((0, 0), (0, 0), (0, 0), (MLA_NOPE, LANE - MLA_NOPE - MLA_ROPE)))
    caches = (cache_mla_ckv, kr_pad, cache_gqa_k.reshape(bl, depth, past, LANE),
              cache_gqa_v.reshape(bl, depth, past, LANE))

    xc = x_prompt.reshape(bc * tc, D_MODEL)
    xl = x_sample.reshape(bl * tl, D_MODEL)
    st_ckv, st_kr, st_k, st_v = [], [], [], []
    for l in range(depth):
        final = l == depth - 1
        xc, pj = _layer(xc, mod4, l, 0, bc, tc, False, final, wl, g_fin, None, None)
        st_ckv.append(pj["ckv_out"].reshape(bc, tc, KV_LORA))
        st_kr.append(pj["kr_out"].reshape(bc, tc, MLA_ROPE))
        st_k.append(pj["kg_out"].reshape(bc, tc, GQA_KV_HEADS, GQA_HEAD_DIM))
        st_v.append(pj["vg_out"].reshape(bc, tc, GQA_KV_HEADS, GQA_HEAD_DIM))
        xl, _ = _layer(xl, mod4, l, 1, bl, tl, True, final, wl, g_fin, tabs, caches)

    return (xc.reshape(bc, tc, D_MODEL), xl.reshape(bl, tl, D_MODEL),
            jnp.stack(st_ckv, axis=1), jnp.stack(st_kr, axis=1),
            jnp.stack(st_k, axis=1), jnp.stack(st_v, axis=1))
```

```python
import functools
import math

import numpy as np
import jax
import jax.numpy as jnp
from jax import lax
from jax.experimental import pallas as pl
from jax.experimental.pallas import tpu as pltpu

D_MODEL = 1024
GRID_W = 64
RMS_EPS = 1e-6
ROPE_THETA = 10000.0
MLA_HEADS = 8
MLA_NOPE = 64
MLA_ROPE = 32
MLA_V = 64
Q_LORA = 256
KV_LORA = 128
GQA_HEADS = 8
GQA_KV_HEADS = 2
GQA_HEAD_DIM = 64
POOL_WINDOWS = (2, 4, 8, 16)
POOL_GROUP = 128
POOL_WIDTH = POOL_GROUP * len(POOL_WINDOWS)
D_FF = 2816
N_BRANCH = 3

LANE = 128
HALO = 8
FF_CHUNK = 256
KEY_TILE = 256
CTX_GROUP = 8
CTX_Q_ROWS = 256
TOKEN_TILE = 512
Q_TILE = 512
LAT_PAIRS = 2
N_PAIRS = MLA_HEADS // 2
VMEM_LIMIT = 56 * 1024 * 1024

BF16 = jnp.bfloat16
F32 = jnp.float32

_OFF_QA = 0
_OFF_CKV = _OFF_QA + Q_LORA
_OFF_KR = _OFF_CKV + KV_LORA
_OFF_GQ = _OFF_KR + MLA_ROPE
_OFF_GK = _OFF_GQ + GQA_HEADS * GQA_HEAD_DIM
_OFF_GV = _OFF_GK + GQA_KV_HEADS * GQA_HEAD_DIM
_OFF_POOL = _OFF_GV + GQA_KV_HEADS * GQA_HEAD_DIM
_OFF_GATE = _OFF_POOL + POOL_WIDTH

_W1_QA = 0
_W1_CKV = 256
_W1_GQ = 384
_W1_GK = 896
_W1_GV = 1024
_W1_POOL = 1152
_W1_KR = 1664
_W1_COLS = 1792


def _const_spec(shape):
    nd = len(shape)
    return pl.BlockSpec(shape, lambda *_: (0,) * nd, pipeline_mode=pl.Buffered(1))


def _wspec(a, layer, cols=None, col_block=0):
    shape = a.shape[1:] if cols is None else a.shape[1:-1] + (cols,)
    idx = (layer,) + (0,) * (len(shape) - 1) + (col_block,)
    return pl.BlockSpec((None,) + shape, lambda *_: idx, pipeline_mode=pl.Buffered(1))


def _rms(x, g):
    ms = jnp.mean(x * x, axis=-1, keepdims=True)
    return x * lax.rsqrt(ms + RMS_EPS) * g


def _dot(a, b):
    return jnp.dot(a, b, preferred_element_type=F32)


def _rope_head_tables(n_tokens, d):
    dim_axis = d // 2
    t = np.arange(n_tokens)
    row = (t // GRID_W).astype(np.float64)
    col = (t % GRID_W).astype(np.float64)
    freqs = ROPE_THETA ** (-np.arange(0, dim_axis, 2, dtype=np.float64) / dim_axis)
    ar = row[:, None] * freqs[None, :]
    ac = col[:, None] * freqs[None, :]
    cos = np.concatenate([np.cos(ar), np.cos(ar), np.cos(ac), np.cos(ac)], axis=1)
    sin = np.concatenate([-np.sin(ar), np.sin(ar), -np.sin(ac), np.sin(ac)], axis=1)
    return cos, sin


def _rope_tables(n_tokens):
    cg, sg = _rope_head_tables(n_tokens, GQA_HEAD_DIM)
    cg = np.concatenate([cg, cg], axis=1)
    sg = np.concatenate([sg, sg], axis=1)
    cm32, sm32 = _rope_head_tables(n_tokens, MLA_ROPE)
    pad = LANE - MLA_NOPE - MLA_ROPE
    cm = np.concatenate([np.ones((n_tokens, MLA_NOPE)), cm32, np.zeros((n_tokens, pad))], axis=1)
    sm = np.concatenate([np.zeros((n_tokens, MLA_NOPE)), sm32, np.zeros((n_tokens, pad))], axis=1)
    return tuple(jnp.asarray(a, F32) for a in (cg, sg, cm, sm))


def _ada_body(c_ref, w_ref, b_ref, o_ref):
    c = c_ref[...]
    s = (c * jax.nn.sigmoid(c)).astype(BF16)
    o_ref[0] = _dot(s, w_ref[0].astype(BF16)) + b_ref[0]


def _ada_call(cin, w_ada, b_ada):
    depth, d, n = w_ada.shape
    cols = 2048
    rows = cin.shape[0]
    return pl.pallas_call(
        _ada_body,
        grid=(depth, n // cols),
        in_specs=[
            pl.BlockSpec((rows, d), lambda l, j: (0, 0)),
            pl.BlockSpec((1, d, cols), lambda l, j: (l, 0, j)),
            pl.BlockSpec((1, 1, cols), lambda l, j: (l, 0, j)),
        ],
        out_specs=pl.BlockSpec((1, rows, cols), lambda l, j: (l, 0, j)),
        out_shape=jax.ShapeDtypeStruct((depth, rows, n), F32),
        compiler_params=pltpu.CompilerParams(
            dimension_semantics=("arbitrary", "arbitrary"), vmem_limit_bytes=VMEM_LIMIT),
        name="ada_mod",
    )(cin, w_ada, b_ada.reshape(depth, 1, n))


def _two_head_rsqrt(xb, lo):
    x2 = xb * xb
    s_lo = jnp.sum(jnp.where(lo, x2, 0.0), axis=-1, keepdims=True)
    s_hi = jnp.sum(jnp.where(lo, 0.0, x2), axis=-1, keepdims=True)
    ms = jnp.where(lo, s_lo, s_hi) * (1.0 / GQA_HEAD_DIM)
    return lax.rsqrt(ms + RMS_EPS)


def _rot_partner_lanes(xb, lane, quarter):
    first = jnp.bitwise_and(lane, quarter) == 0
    return jnp.where(first, pltpu.roll(xb, LANE - quarter, axis=1), pltpu.roll(xb, quarter, axis=1))


def _store_value_heads(v_ref, first, blk, lo):
    v_ref[first] = jnp.where(lo, blk, 1.0).astype(BF16)
    v_ref[first + 1] = jnp.where(lo, 1.0, blk).astype(BF16)


def _inproj_body(latent, names, *refs):
    r = dict(zip(names, refs))
    x = r["x"][...]
    tm = x.shape[0]
    mod = r["mod"][0]
    sh1 = mod[:, 0:D_MODEL]
    sc1 = mod[:, D_MODEL:2 * D_MODEL]
    h = (_rms(x, r["gmix"][...]) * (1.0 + sc1) + sh1).astype(BF16)
    z = _dot(h, r["w1"][...])

    lane = lax.broadcasted_iota(jnp.int32, (tm, LANE), 1)
    lo = lane < GQA_HEAD_DIM

    if latent:
        cg = r["cg"][...]
        sg = r["sg"][...]
        cm = r["cm"][...]
        sm = r["sm"][...]

    qn = _rms(z[:, _W1_QA:_W1_QA + Q_LORA], r["gqa"][...]).astype(BF16)
    qm = _dot(qn, r["wqb"][...])
    if latent:
        qms = _dot(qn, r["wqbsw"][...])
    for hh in range(MLA_HEADS):
        blk = qm[:, hh * LANE:(hh + 1) * LANE]
        if latent:
            blk = blk * cm + qms[:, hh * LANE:(hh + 1) * LANE] * sm
        r["q_mla"][hh] = blk.astype(BF16)

    ckv = _rms(z[:, _W1_CKV:_W1_CKV + KV_LORA], r["gkva"][...])
    kr = z[:, _W1_KR:_W1_KR + LANE]
    if latent:
        kr = kr * cm + _rot_partner_lanes(kr, lane, MLA_ROPE // 4) * sm
    else:
        r["ckv_out"][...] = ckv
        r["kr_out"][...] = kr[:, MLA_NOPE:MLA_NOPE + MLA_ROPE]
    ckv_b = ckv.astype(BF16)
    kfull = _dot(jnp.concatenate([ckv_b, kr.astype(BF16)], axis=1), r["wkp"][...])
    for hh in range(MLA_HEADS):
        r["k_mla"][hh] = kfull[:, hh * LANE:(hh + 1) * LANE].astype(BF16)
    vfull = _dot(ckv_b, r["wv"][...])
    for pp in range(N_PAIRS):
        _store_value_heads(r["v_mla"], 2 * pp, vfull[:, pp * LANE:(pp + 1) * LANE], lo)

    gq2 = r["gq2"][...]
    q_scale = GQA_HEAD_DIM ** -0.5
    for j in range(N_PAIRS):
        xb = z[:, _W1_GQ + j * LANE:_W1_GQ + (j + 1) * LANE]
        rs = _two_head_rsqrt(xb, lo)
        y = xb * rs * gq2
        if latent:
            y = y * cg + _rot_partner_lanes(y, lane, GQA_HEAD_DIM // 4) * sg
        y = y * q_scale
        r["q_gqa"][2 * j] = jnp.where(lo, y, 0.0).astype(BF16)
        r["q_gqa"][2 * j + 1] = jnp.where(lo, 0.0, y).astype(BF16)
    kb = z[:, _W1_GK:_W1_GK + LANE]
    rs = _two_head_rsqrt(kb, lo)
    kg = kb * rs * r["gk2"][...]
    if latent:
        kg = kg * cg + _rot_partner_lanes(kg, lane, GQA_HEAD_DIM // 4) * sg
    vg = z[:, _W1_GV:_W1_GV + LANE]
    if not latent:
        r["kg_out"][...] = kg
        r["vg_out"][...] = vg
    r["k_gqa"][...] = kg.astype(BF16)
    _store_value_heads(r["v_gqa"], 0, vg, lo)

    r["pool"][...] = z[:, _W1_POOL:_W1_POOL + POOL_WIDTH]


def _inproj_call(x, mod4, layer, mod_base, seq_len, tm, latent, wl, tabs):
    n = x.shape[0]
    nt = n // tm
    per_seq = seq_len // tm if latent else 1

    def tile(i):
        return (i, 0)

    def mod_idx(i):
        if latent:
            return (layer, mod_base + i // per_seq, 0, 0)
        return (layer, mod_base, 0, 0)

    names = ["x", "mod", "w1", "gmix", "gqa", "wqb", "gkva", "wkp", "wv", "gq2", "gk2"]
    args = [x, mod4, wl["w1"]] + [wl[nm] for nm in names[3:]]
    specs = [pl.BlockSpec((tm, D_MODEL), tile), pl.BlockSpec((None, 1, 1, 6 * D_MODEL), mod_idx),
             _wspec(wl["w1"], layer)]
    specs += [_wspec(a, layer) for a in args[3:]]
    if latent:
        names.append("wqbsw")
        args.append(wl["wqbsw"])
        specs.append(_wspec(wl["wqbsw"], layer))
        for nm, a in zip(("cg", "sg", "cm", "sm"), tabs):
            names.append(nm)
            args.append(a)
            specs.append(pl.BlockSpec((tm, LANE), lambda i: (i % per_seq, 0)))

    head_spec = pl.BlockSpec((MLA_HEADS, tm, LANE), lambda i: (0, i, 0))
    kvh_spec = pl.BlockSpec((GQA_KV_HEADS, tm, LANE), lambda i: (0, i, 0))
    out_names = ["q_mla", "k_mla", "v_mla", "q_gqa", "k_gqa", "v_gqa", "pool"]
    out_shapes = [
        jax.ShapeDtypeStruct((MLA_HEADS, n, LANE), BF16),
        jax.ShapeDtypeStruct((MLA_HEADS, n, LANE), BF16),
        jax.ShapeDtypeStruct((MLA_HEADS, n, LANE), BF16),
        jax.ShapeDtypeStruct((GQA_HEADS, n, LANE), BF16),
        jax.ShapeDtypeStruct((n, LANE), BF16),
        jax.ShapeDtypeStruct((GQA_KV_HEADS, n, LANE), BF16),
        jax.ShapeDtypeStruct((n, POOL_WIDTH), F32),
    ]
    out_specs = [head_spec, head_spec, head_spec, head_spec,
                 pl.BlockSpec((tm, LANE), tile), kvh_spec,
                 pl.BlockSpec((tm, POOL_WIDTH), tile)]
    if not latent:
        out_names += ["ckv_out", "kr_out", "kg_out", "vg_out"]
        out_shapes += [jax.ShapeDtypeStruct((n, KV_LORA), F32), jax.ShapeDtypeStruct((n, MLA_ROPE), F32),
                       jax.ShapeDtypeStruct((n, LANE), F32), jax.ShapeDtypeStruct((n, LANE), F32)]
        out_specs += [pl.BlockSpec((tm, KV_LORA), tile), pl.BlockSpec((tm, MLA_ROPE), tile),
                      pl.BlockSpec((tm, LANE), tile), pl.BlockSpec((tm, LANE), tile)]

    outs = pl.pallas_call(
        functools.partial(_inproj_body, latent, names + out_names),
        grid=(nt,),
        in_specs=specs,
        out_specs=out_specs,
        out_shape=out_shapes,
        compiler_params=pltpu.CompilerParams(
            dimension_semantics=("arbitrary",), vmem_limit_bytes=VMEM_LIMIT),
        name="inproj_lat" if latent else "inproj_ctx",
    )(*args)
    return dict(zip(out_names, outs))


def _cachekv_body(ckv_ref, kr_ref, kg_ref, vg_ref, wkp_ref, wv_ref, k_ref, v_ref, kgo_ref, vgo_ref):
    rows = ckv_ref.shape[0] * ckv_ref.shape[1]
    ckv_b = ckv_ref[...].reshape(rows, LANE).astype(BF16)
    lo = lax.broadcasted_iota(jnp.int32, (rows, LANE), 1) < MLA_V
    kr_b = kr_ref[...].reshape(rows, LANE).astype(BF16)
    kfull = _dot(jnp.concatenate([ckv_b, kr_b], axis=1), wkp_ref[...])
    kgo_ref[0] = kg_ref[...].reshape(rows, LANE).astype(BF16)
    for hh in range(MLA_HEADS):
        k_ref[hh] = kfull[:, hh * LANE:(hh + 1) * LANE].astype(BF16)
    vfull = _dot(ckv_b, wv_ref[...])
    for pp in range(N_PAIRS):
        _store_value_heads(v_ref, 2 * pp, vfull[:, pp * LANE:(pp + 1) * LANE], lo)
    _store_value_heads(vgo_ref, 0, vg_ref[...].reshape(rows, LANE), lo)


def _cachekv_call(caches, layer, wl):
    bsz, _, past, _ = caches[0].shape
    n = bsz * past
    cache_spec = pl.BlockSpec((bsz, None, past, LANE), lambda i: (0, layer, 0, 0))
    return pl.pallas_call(
        _cachekv_body,
        grid=(1,),
        in_specs=[cache_spec] * 4 + [_wspec(wl["wkp"], layer), _wspec(wl["wv"], layer)],
        out_specs=[_const_spec((MLA_HEADS, n, LANE)), _const_spec((MLA_HEADS, n, LANE)),
                   _const_spec((1, n, LANE)), _const_spec((GQA_KV_HEADS, n, LANE))],
        out_shape=[jax.ShapeDtypeStruct((MLA_HEADS, n, LANE), BF16),
                   jax.ShapeDtypeStruct((MLA_HEADS, n, LANE), BF16),
                   jax.ShapeDtypeStruct((1, n, LANE), BF16),
                   jax.ShapeDtypeStruct((GQA_KV_HEADS, n, LANE), BF16)],
        compiler_params=pltpu.CompilerParams(
            dimension_semantics=("arbitrary",), vmem_limit_bytes=VMEM_LIMIT),
        name="cache_kv",
    )(*caches, wl["wkp"], wl["wv"])


def _normalise_pair(acc_a, acc_b, lo):
    num = jnp.where(lo, acc_a, acc_b)
    den = pltpu.roll(jnp.where(lo, acc_b, acc_a), MLA_V, axis=1)
    return num / den


def _attn_body(group, seq_len, shared_kv, exp_scale, q_ref, k_ref, v_ref, o_ref):
    lo = lax.broadcasted_iota(jnp.int32, (CTX_Q_ROWS, LANE), 1) < MLA_V
    nt_dims = (((1,), (1,)), ((), ()))
    units = [(pr, qb, e) for pr in range(N_PAIRS) for qb in range(seq_len // CTX_Q_ROWS) for e in range(2)]

    def one_batch_row(g, carry):
        base = pl.multiple_of(g * seq_len, seq_len)
        rows = pl.ds(base, seq_len)

        scores, row_max, accs = {}, {}, {}
        for i in range(len(units) + 2):
            if i < len(units):
                pr, qb, e = units[i]
                qrows = pl.ds(base + qb * CTX_Q_ROWS, CTX_Q_ROWS)
                q = q_ref[2 * pr + e, qrows, :]
                k = k_ref[0 if shared_kv else 2 * pr + e, rows, :]
                scores[i] = lax.dot_general(q, k, nt_dims, preferred_element_type=F32)
            if 0 <= i - 1 < len(units):
                row_max[i - 1] = jnp.max(scores[i - 1], axis=-1, keepdims=True)
            if 0 <= i - 2 < len(units):
                u = i - 2
                pr, qb, e = units[u]
                qrows = pl.ds(base + qb * CTX_Q_ROWS, CTX_Q_ROWS)
                v = v_ref[e if shared_kv else 2 * pr + e, rows, :]
                p = jnp.exp2((scores.pop(u) - row_max.pop(u)) * exp_scale)
                accs[u] = _dot(p.astype(BF16), v)
                if e == 1:
                    o_ref[pr, qrows, :] = _normalise_pair(accs.pop(u - 1), accs.pop(u), lo).astype(BF16)
        return carry

    lax.fori_loop(0, group, one_batch_row, 0)


def _attn_call(q, k, v, batch, seq_len, group, scale, shared_kv, name):
    n = q.shape[1]
    rows = group * seq_len
    exp_scale = scale * math.log2(math.e)

    def spec(heads):
        return pl.BlockSpec((heads, rows, LANE), lambda b: (0, b, 0))

    return pl.pallas_call(
        functools.partial(_attn_body, group, seq_len, shared_kv, exp_scale),
        grid=(batch // group,),
        in_specs=[spec(q.shape[0]), spec(k.shape[0]), spec(v.shape[0])],
        out_specs=pl.BlockSpec((N_PAIRS, rows, LANE), lambda b: (0, b, 0)),
        out_shape=jax.ShapeDtypeStruct((N_PAIRS, n, LANE), BF16),
        compiler_params=pltpu.CompilerParams(
            dimension_semantics=("arbitrary",), vmem_limit_bytes=VMEM_LIMIT),
        name=name,
    )(q, k, v)


def _attn_pipe_body(shared_kv, exp_scale, tq, pairs, q_ref, k_ref, v_ref, kc_ref, vc_ref, o_ref,
                    s0, s1, mr0, mr1, ac0, ac1):
    seq = k_ref.shape[1]
    past = kc_ref.shape[1]
    nq = seq // tq
    n_units = pairs * nq
    n_new = seq // KEY_TILE
    n_tiles = n_new + past // KEY_TILE
    s_scr = (s0, s1)
    mrun_scr = (mr0, mr1)
    acc_scr = (ac0, ac1)
    nt_dims = (((1,), (1,)), ((), ()))
    lo = lax.broadcasted_iota(jnp.int32, (tq, LANE), 1) < MLA_V

    def pair_rows(t):
        j = t % nq
        return t // nq, pl.ds(pl.multiple_of(j * tq, tq), tq)

    def kv_tile(new_ref, cache_ref, head, blk):
        if blk < n_new:
            return new_ref[head, blk * KEY_TILE:(blk + 1) * KEY_TILE, :]
        blk -= n_new
        return cache_ref[head, blk * KEY_TILE:(blk + 1) * KEY_TILE, :]

    def finish_prev(e_prev, t_prev):
        if e_prev == 1:
            pair, rows = pair_rows(t_prev)
            o_ref[pair, rows, :] = _normalise_pair(ac0[...], ac1[...], lo).astype(BF16)

    def region(t_scores, e_scores, t_cur, e_cur, t_prev):
        if t_prev is not None:
            finish_prev(1 - e_cur, t_prev)
        if e_scores is not None:
            pair, rows = pair_rows(t_scores)
            q = q_ref[2 * pair + e_scores, rows, :]
            k_head = 0 if shared_kv else 2 * pair + e_scores
        if e_cur is not None:
            v_head = e_cur if shared_kv else 2 * (t_cur // nq) + e_cur
            m = jnp.max(mrun_scr[e_cur][...], axis=-1, keepdims=True)
            m_b = jnp.broadcast_to(m, (tq, LANE))
        acc = None
        for blk in range(n_tiles):
            cols = slice(blk * KEY_TILE, (blk + 1) * KEY_TILE)
            if e_scores is not None:
                s_t = lax.dot_general(q, kv_tile(k_ref, kc_ref, k_head, blk), nt_dims,
                                      preferred_element_type=F32)
                s_scr[e_scores][:, cols] = s_t
                m_t = jnp.maximum(s_t[:, 0:LANE], s_t[:, LANE:2 * LANE])
                if blk > 0:
                    m_t = jnp.maximum(m_t, mrun_scr[e_scores][...])
                mrun_scr[e_scores][...] = m_t
            if e_cur is not None:
                p_parts = []
                for hh in range(KEY_TILE // LANE):
                    c0 = blk * KEY_TILE + hh * LANE
                    s_h = s_scr[e_cur][:, c0:c0 + LANE]
                    p_parts.append(jnp.exp2((s_h - m_b) * exp_scale).astype(BF16))
                p_t = jnp.concatenate(p_parts, axis=1)
                part = _dot(p_t, kv_tile(v_ref, vc_ref, v_head, blk))
                acc = part if acc is None else acc + part
        if e_cur is not None:
            acc_scr[e_cur][...] = acc

    region(0, 0, None, None, None)
    region(0, 1, 0, 0, None)
    region(1, 0, 0, 1, 0)

    def body(t, carry):
        @pl.when(t > 0)
        def _():
            region(t, 1, t, 0, t - 1)

        @pl.when(t < n_units)
        def _():
            region(t + 1, 0, t, 1, t)

        return carry

    lax.fori_loop(1, n_units - 1, body, 0)
    region(n_units - 1, 1, n_units - 1, 0, n_units - 2)
    region(None, None, n_units - 1, 1, n_units - 1)
    finish_prev(1, n_units - 1)


def _attn_pipe_call(q, k, v, kc, vc, batch, seq_len, tq, scale, shared_kv, name, pairs=LAT_PAIRS):
    n = q.shape[1]
    past = kc.shape[1] // batch
    exp_scale = scale * math.log2(math.e)

    def spec(a, rows):
        heads = a.shape[0]
        if heads < 2 * N_PAIRS:
            return pl.BlockSpec((heads, rows, LANE), lambda b, p: (0, b, 0))
        return pl.BlockSpec((2 * pairs, rows, LANE), lambda b, p: (p, b, 0))

    total = seq_len + past
    return pl.pallas_call(
        functools.partial(_attn_pipe_body, shared_kv, exp_scale, tq, pairs),
        grid=(batch, N_PAIRS // pairs),
        in_specs=[spec(q, seq_len), spec(k, seq_len), spec(v, seq_len), spec(kc, past), spec(vc, past)],
        out_specs=pl.BlockSpec((pairs, seq_len, LANE), lambda b, p: (p, b, 0)),
        out_shape=jax.ShapeDtypeStruct((N_PAIRS, n, LANE), BF16),
        scratch_shapes=[pltpu.VMEM((tq, total), F32), pltpu.VMEM((tq, total), F32)]
        + [pltpu.VMEM((tq, LANE), F32) for _ in range(4)],
        compiler_params=pltpu.CompilerParams(
            dimension_semantics=("arbitrary", "arbitrary"), vmem_limit_bytes=VMEM_LIMIT),
        name=name,
    )(q, k, v, kc, vc)


def _seq_pos(tm, seq_len):
    i = pl.program_id(0)
    row = lax.broadcasted_iota(jnp.int32, (tm, 1), 0) + i * tm
    return jnp.bitwise_and(row, seq_len - 1)


def _merge_body(seq_len, x_ref, mod_ref, gmix_ref, wg_ref, a_ref, b_ref, pc_ref, pl_ref, pr_ref,
                wpool_ref, pscale_ref, wa_ref, wb_ref, wc_ref, wo_ref, o_ref, merged_scr, ab_scr):
    x = x_ref[...]
    tm = x.shape[0]
    mod = mod_ref[0]
    sh1 = mod[:, 0:D_MODEL]
    sc1 = mod[:, D_MODEL:2 * D_MODEL]
    g1 = mod[:, 2 * D_MODEL:3 * D_MODEL]
    h = (_rms(x, gmix_ref[...]) * (1.0 + sc1) + sh1).astype(BF16)

    pos = _seq_pos(tm, seq_len)
    ext = jnp.concatenate([pl_ref[...], pc_ref[...], pr_ref[...]], axis=0)
    pscale = pscale_ref[...]

    def pool_group(g):
        w = POOL_WINDOWS[g]
        eg = ext[:, g * POOL_GROUP:(g + 1) * POOL_GROUP]
        centre = eg[HALO:HALO + tm]
        wsum = centre
        for dlt in range(-(w // 2), w - w // 2):
            if dlt == 0:
                continue
            valid = pos >= -dlt if dlt < 0 else pos < seq_len - dlt
            wsum = wsum + jnp.where(valid, eg[HALO + dlt:HALO + dlt + tm], 0.0)
        cnt = jnp.minimum(pos + (w - w // 2), seq_len) - jnp.maximum(pos - w // 2, 0)
        return (wsum / cnt.astype(F32) - centre).astype(BF16)

    def pool_pair_map(p, pooled_a, pooled_b):
        both = jnp.concatenate([pooled_a, pooled_b], axis=1)
        return _dot(both, wpool_ref[p]) * pscale[:, 2 * p * POOL_GROUP:2 * (p + 1) * POOL_GROUP]

    def gated(branch, src, w_ref, j):
        c0 = branch * D_MODEL + j * FF_CHUNK
        gate = jax.nn.sigmoid(_dot(h, wg_ref[:, c0:c0 + FF_CHUNK]))
        return gate * _dot(src, w_ref[:, j * FF_CHUNK:(j + 1) * FF_CHUNK])

    a_out = jnp.concatenate([a_ref[p] for p in range(N_PAIRS)], axis=1)
    b_out = jnp.concatenate([b_ref[p] for p in range(N_PAIRS)], axis=1)
    n_chunks = D_MODEL // FF_CHUNK
    pooled = []
    c_parts = []
    for j in range(n_chunks):
        cols = slice(j * FF_CHUNK, (j + 1) * FF_CHUNK)
        for g in range(j * len(POOL_WINDOWS) // n_chunks, (j + 1) * len(POOL_WINDOWS) // n_chunks):
            pooled.append(pool_group(g))
            if g % 2 == 1:
                c_parts.append(pool_pair_map(g // 2, pooled[g - 1], pooled[g]))
        ab_scr[:, cols] = gated(0, a_out, wa_ref, j) + gated(1, b_out, wb_ref, j)
    c_out = jnp.concatenate(c_parts, axis=-1).astype(BF16)
    for j in range(n_chunks):
        cols = slice(j * FF_CHUNK, (j + 1) * FF_CHUNK)
        merged_scr[:, cols] = (ab_scr[:, cols] + gated(2, c_out, wc_ref, j)).astype(BF16)
    o_ref[...] = x + g1 * _dot(merged_scr[...], wo_ref[...])


def _halo_specs(tm, width, n):
    blocks = tm // HALO
    last = n // HALO - 1
    left = pl.BlockSpec((HALO, width), lambda i: (jnp.maximum(i * blocks - 1, 0), 0))
    right = pl.BlockSpec((HALO, width), lambda i: (jnp.minimum((i + 1) * blocks, last), 0))
    return left, right


def _mod_spec(layer, mod_base, per_seq, latent):
    if latent:
        return pl.BlockSpec((None, 1, 1, 6 * D_MODEL), lambda i: (layer, mod_base + i // per_seq, 0, 0))
    return pl.BlockSpec((None, 1, 1, 6 * D_MODEL), lambda i: (layer, mod_base, 0, 0))


def _merge_call(x, mod4, layer, mod_base, seq_len, tm, latent, a_out, b_out, pool_in, wl):
    n = x.shape[0]
    per_seq = seq_len // tm if latent else 1

    def tile(i):
        return (i, 0)

    left, right = _halo_specs(tm, POOL_WIDTH, n)
    consts = [wl["wpool"], wl["pscale"], wl["wa"], wl["wb"], wl["wc"], wl["wo"]]
    in_specs = [pl.BlockSpec((tm, D_MODEL), tile), _mod_spec(layer, mod_base, per_seq, latent),
                _wspec(wl["gmix"], layer), _wspec(wl["wg"], layer),
                pl.BlockSpec((N_PAIRS, tm, LANE), lambda i: (0, i, 0)),
                pl.BlockSpec((N_PAIRS, tm, LANE), lambda i: (0, i, 0)),
                pl.BlockSpec((tm, POOL_WIDTH), tile), left, right]
    in_specs += [_wspec(a, layer) for a in consts]
    return pl.pallas_call(
        functools.partial(_merge_body, seq_len),
        grid=(n // tm,),
        in_specs=in_specs,
        out_specs=pl.BlockSpec((tm, D_MODEL), tile),
        out_shape=jax.ShapeDtypeStruct((n, D_MODEL), F32),
        scratch_shapes=[pltpu.VMEM((tm, D_MODEL), BF16), pltpu.VMEM((tm, D_MODEL), F32)],
        compiler_params=pltpu.CompilerParams(
            dimension_semantics=("arbitrary",), vmem_limit_bytes=VMEM_LIMIT),
        name="merge_lat" if latent else "merge_ctx",
    )(x, mod4, wl["gmix"], wl["wg"], a_out, b_out, pool_in, pool_in, pool_in, *consts)


def _ffn_body(seq_len, final, x_ref, xl_ref, xr_ref, mod_ref, gffn_ref, wug_ref, wuv_ref, cw_ref, cb_ref,
              wd_ref, gfin_ref, o_ref, act_scr):
    x = x_ref[...]
    tm = x.shape[0]
    mod = mod_ref[0]
    sh2 = mod[:, 3 * D_MODEL:4 * D_MODEL]
    sc2 = mod[:, 4 * D_MODEL:5 * D_MODEL]
    g2 = mod[:, 5 * D_MODEL:6 * D_MODEL]
    xe = jnp.concatenate([xl_ref[...], x, xr_ref[...]], axis=0)
    h2e = _rms(xe, gffn_ref[...]) * (1.0 + sc2) + sh2
    h2 = h2e[HALO:HALO + tm].astype(BF16)
    h2e = h2e.astype(BF16)

    pos = _seq_pos(tm, seq_len)
    has_prev = pos >= 1
    has_next = pos <= seq_len - 2
    cw = cw_ref[...]
    cb = cb_ref[...]
    for j in range(D_FF // FF_CHUNK):
        cols = slice(j * FF_CHUNK, (j + 1) * FF_CHUNK)
        ge = _dot(h2e, wug_ref[:, cols])
        g = (jnp.where(has_prev, ge[HALO - 1:HALO - 1 + tm], 0.0) * cw[0:1, cols]
             + ge[HALO:HALO + tm] * cw[1:2, cols]
             + jnp.where(has_next, ge[HALO + 1:HALO + 1 + tm], 0.0) * cw[2:3, cols]
             + cb[:, cols])
        val = _dot(h2, wuv_ref[:, cols])
        act_scr[:, cols] = (g * jax.nn.sigmoid(g) * val).astype(BF16)
    y = x + g2 * _dot(act_scr[...], wd_ref[...])
    if final:
        y = _rms(y, gfin_ref[...])
    o_ref[...] = y


def _ffn_call(x, mod4, layer, mod_base, seq_len, tm, latent, final, wl, g_final):
    n = x.shape[0]
    per_seq = seq_len // tm if latent else 1

    def tile(i):
        return (i, 0)

    left, right = _halo_specs(tm, D_MODEL, n)
    consts = [wl["gffn"], wl["wu"], wl["wu"], wl["cw"], wl["cb"], wl["wd"], g_final]
    in_specs = [pl.BlockSpec((tm, D_MODEL), tile), left, right, _mod_spec(layer, mod_base, per_seq, latent),
                _wspec(wl["gffn"], layer), _wspec(wl["wu"], layer, cols=D_FF, col_block=0),
                _wspec(wl["wu"], layer, cols=D_FF, col_block=1), _wspec(wl["cw"], layer),
                _wspec(wl["cb"], layer), _wspec(wl["wd"], layer), _const_spec(g_final.shape)]
    return pl.pallas_call(
        functools.partial(_ffn_body, seq_len, final),
        grid=(n // tm,),
        in_specs=in_specs,
        out_specs=pl.BlockSpec((tm, D_MODEL), tile),
        out_shape=jax.ShapeDtypeStruct((n, D_MODEL), F32),
        scratch_shapes=[pltpu.VMEM((tm, D_FF), BF16)],
        compiler_params=pltpu.CompilerParams(
            dimension_semantics=("arbitrary",), vmem_limit_bytes=VMEM_LIMIT),
        name="ffn_lat" if latent else "ffn_ctx",
    )(x, x, x, mod4, *consts)


def _rot_partner(a):
    d = a.shape[-1]
    return jnp.flip(a.reshape(a.shape[:-1] + (2, 2, d // 4)), axis=-2).reshape(a.shape)


def _prep_weights(g_norm_mix, w_in, g_q_a, w_q_b, g_kv_a, w_kv_b, g_q_gqa, g_k_gqa, w_pool, pool_scale,
                  w_br_a, w_br_b, w_br_c, w_out, g_norm_ffn, w_up, conv_w, conv_b, w_down):
    depth = w_in.shape[0]
    hd = GQA_HEAD_DIM
    grp = GQA_HEADS // GQA_KV_HEADS

    def lane_pad(a, before, width=LANE):
        cfg = [(0, 0)] * (a.ndim - 1) + [(before, width - before - a.shape[-1])]
        return jnp.pad(a, cfg)

    def pair_heads(a):
        lead = a.shape[:-1]
        return a.reshape(lead + (GQA_KV_HEADS, grp, hd)).swapaxes(-3, -2).reshape(lead + (GQA_HEADS * hd,))

    w1 = jnp.concatenate(
        [w_in[:, :, _OFF_QA:_OFF_KR], pair_heads(w_in[:, :, _OFF_GQ:_OFF_GK]), w_in[:, :, _OFF_GK:_OFF_GATE],
         lane_pad(w_in[:, :, _OFF_KR:_OFF_GQ], MLA_NOPE)], axis=-1).astype(BF16)
    assert w1.shape[-1] == _W1_COLS

    qb = w_q_b.reshape(depth, Q_LORA, MLA_HEADS, MLA_NOPE + MLA_ROPE)
    wqb = lane_pad(qb, 0).reshape(depth, Q_LORA, MLA_HEADS * LANE).astype(BF16)
    wqbsw = lane_pad(_rot_partner(qb[..., MLA_NOPE:]), MLA_NOPE).reshape(
        depth, Q_LORA, MLA_HEADS * LANE).astype(BF16)

    kvb = w_kv_b.reshape(depth, KV_LORA, MLA_HEADS, MLA_NOPE + MLA_V)
    wk = lane_pad(kvb[..., :MLA_NOPE], 0).reshape(depth, KV_LORA, MLA_HEADS * LANE).astype(BF16)
    wv = kvb[..., MLA_NOPE:].reshape(depth, KV_LORA, MLA_HEADS * MLA_V).astype(BF16)

    pk = np.zeros((LANE, MLA_HEADS * LANE), np.float32)
    for hh in range(MLA_HEADS):
        for t in range(MLA_ROPE):
            pk[MLA_NOPE + t, hh * LANE + MLA_NOPE + t] = 1.0
    wkp = jnp.concatenate([wk, jnp.broadcast_to(jnp.asarray(pk, BF16), (depth,) + pk.shape)], axis=1)

    wp = w_pool.reshape(depth, len(POOL_WINDOWS) // 2, 2, POOL_GROUP, POOL_GROUP)
    zero = jnp.zeros_like(wp[:, :, 0])
    wpool2 = jnp.concatenate([jnp.concatenate([wp[:, :, 0], zero], axis=-1),
                              jnp.concatenate([zero, wp[:, :, 1]], axis=-1)], axis=-2).astype(BF16)

    row = lambda a: a[:, None, :]
    two = lambda a: jnp.concatenate([a, a], axis=-1)
    wb = w_br_b.reshape(depth, GQA_KV_HEADS, grp, hd, D_MODEL).swapaxes(1, 2).reshape(depth, GQA_HEADS * hd, D_MODEL)
    return dict(
        gmix=row(g_norm_mix), w1=w1, gqa=row(g_q_a), wqb=wqb, wqbsw=wqbsw, gkva=row(g_kv_a), wkp=wkp,
        wv=wv, gq2=row(two(g_q_gqa)), gk2=row(two(g_k_gqa)),
        wg=w_in[:, :, _OFF_GATE:].astype(BF16), wpool=wpool2, pscale=row(pool_scale),
        wa=w_br_a.astype(BF16), wb=wb.astype(BF16), wc=w_br_c.astype(BF16), wo=w_out.astype(BF16),
        gffn=row(g_norm_ffn), wu=w_up.astype(BF16), cw=conv_w, cb=row(conv_b), wd=w_down.astype(BF16),
    )


def _layer(x, mod4, layer, mod_base, batch, seq_len, latent, final, wl, g_final, tabs, caches):
    tm = TOKEN_TILE
    tq = Q_TILE
    assert tm % seq_len == 0 or seq_len % tm == 0
    pj = _inproj_call(x, mod4, layer, mod_base, seq_len, tm, latent, wl, tabs)
    if caches is not None:
        kc_m, vc_m, kc_g, vc_g = _cachekv_call(caches, layer, wl)
    mla_scale = (MLA_NOPE + MLA_ROPE) ** -0.5
    n = x.shape[0]
    k_g = pj["k_gqa"].reshape(1, n, LANE)
    if latent:
        a_out = _attn_pipe_call(pj["q_mla"], pj["k_mla"], pj["v_mla"], kc_m, vc_m, batch, seq_len, tq,
                                mla_scale, False, "attn_mla_lat")
        b_out = _attn_pipe_call(pj["q_gqa"], k_g, pj["v_gqa"], kc_g, vc_g, batch, seq_len, tq, 1.0, True,
                                "attn_gqa_lat")
    else:
        a_out = _attn_call(pj["q_mla"], pj["k_mla"], pj["v_mla"], batch, seq_len, CTX_GROUP,
                           mla_scale, False, "attn_mla_ctx")
        b_out = _attn_call(pj["q_gqa"], k_g, pj["v_gqa"], batch, seq_len, CTX_GROUP, 1.0, True,
                           "attn_gqa_ctx")
    x1 = _merge_call(x, mod4, layer, mod_base, seq_len, tm, latent, a_out, b_out, pj["pool"], wl)
    x2 = _ffn_call(x1, mod4, layer, mod_base, seq_len, tm, latent, final, wl, g_final)
    return x2, pj


def kernel(x_prompt, x_sample, c, cache_mla_ckv, cache_mla_krope, cache_gqa_k, cache_gqa_v, c_ctx, w_ada, b_ada, g_norm_mix, w_in, g_q_a, w_q_b, g_kv_a, w_kv_b, g_q_gqa, g_k_gqa, w_pool, pool_scale, w_br_a, w_br_b, w_br_c, w_out, g_norm_ffn, w_up, conv_w, conv_b, w_down, g_final):
    depth = w_in.shape[0]
    bc, tc, _ = x_prompt.shape
    bl, tl, _ = x_sample.shape
    past = cache_mla_ckv.shape[2]
    mod_rows = 8
    assert 1 + bl <= mod_rows and tc & (tc - 1) == 0 and tl & (tl - 1) == 0

    cin = jnp.concatenate([c_ctx[None, :], c, jnp.zeros((mod_rows - 1 - bl, D_MODEL), F32)], axis=0)
    mod4 = _ada_call(cin, w_ada, b_ada).reshape(depth, mod_rows, 1, 6 * D_MODEL)
    tabs = _rope_tables(tl)
    g_fin = g_final.reshape(1, D_MODEL)
    wl = _prep_weights(g_norm_mix, w_in, g_q_a, w_q_b, g_kv_a, w_kv_b, g_q_gqa, g_k_gqa, w_pool,
                       pool_scale, w_br_a, w_br_b, w_br_c, w_out, g_norm_ffn, w_up, conv_w, conv_b, w_down)
    kr_pad = jnp.pad(cache_mla_krope, ((0, 0), (0, 0), (0, 0), (MLA_NOPE, LANE - MLA_NOPE - MLA_ROPE)))
    caches = (cache_mla_ckv, kr_pad, cache_gqa_k.reshape(bl, depth, past, LANE),
              cache_gqa_v.reshape(bl, depth, past, LANE))

    xc = x_prompt.reshape(bc * tc, D_MODEL)
    xl = x_sample.reshape(bl * tl, D_MODEL)
    st_ckv, st_kr, st_k, st_v = [], [], [], []
    for l in range(depth):
        final = l == depth - 1
        xc, pj = _layer(xc, mod4, l, 0, bc, tc, False, final, wl, g_fin, None, None)
        st_ckv.append(pj["ckv_out"].reshape(bc, tc, KV_LORA))
        st_kr.append(pj["kr_out"].reshape(bc, tc, MLA_ROPE))
        st_k.append(pj["kg_out"].reshape(bc, tc, GQA_KV_HEADS, GQA_HEAD_DIM))
        st_v.append(pj["vg_out"].reshape(bc, tc, GQA_KV_HEADS, GQA_HEAD_DIM))
        xl, _ = _layer(xl, mod4, l, 1, bl, tl, True, final, wl, g_fin, tabs, caches)

    return (xc.reshape(bc, tc, D_MODEL), xl.reshape(bl, tl, D_MODEL),
            jnp.stack(st_ckv, axis=1), jnp.stack(st_kr, axis=1),
            jnp.stack(st_k, axis=1), jnp.stack(st_v, axis=1))
```

```python
import functools
import math

import numpy as np
import jax
import jax.numpy as jnp
from jax import lax
from jax.experimental import pallas as pl
from jax.experimental.pallas import tpu as pltpu

D_MODEL = 1024
GRID_W = 64
RMS_EPS = 1e-6
ROPE_THETA = 10000.0
MLA_HEADS = 8
MLA_NOPE = 64
MLA_ROPE = 32
MLA_V = 64
Q_LORA = 256
KV_LORA = 128
GQA_HEADS = 8
GQA_KV_HEADS = 2
GQA_HEAD_DIM = 64
POOL_WINDOWS = (2, 4, 8, 16)
POOL_GROUP = 128
POOL_WIDTH = POOL_GROUP * len(POOL_WINDOWS)
D_FF = 2816

LANE = 128
HALO = 8
FF_CHUNK = 256
KEY_TILE = 256
CTX_GROUP = 8
CTX_Q_ROWS = 256
TOKEN_TILE = 512
Q_TILE = 512
LAT_PAIRS = 2
N_PAIRS = MLA_HEADS // 2
VMEM_LIMIT = 56 * 1024 * 1024

BF16 = jnp.bfloat16
F32 = jnp.float32

_OFF_QA = 0
_OFF_CKV = _OFF_QA + Q_LORA
_OFF_KR = _OFF_CKV + KV_LORA
_OFF_GQ = _OFF_KR + MLA_ROPE
_OFF_GK = _OFF_GQ + GQA_HEADS * GQA_HEAD_DIM
_OFF_GV = _OFF_GK + GQA_KV_HEADS * GQA_HEAD_DIM
_OFF_POOL = _OFF_GV + GQA_KV_HEADS * GQA_HEAD_DIM
_OFF_GATE = _OFF_POOL + POOL_WIDTH

_W1_QA = 0
_W1_CKV = 256
_W1_GQ = 384
_W1_GK = 896
_W1_GV = 1024
_W1_POOL = 1152
_W1_KR = 1664
_W1_COLS = 1792


def _const_spec(shape):
    nd = len(shape)
    return pl.BlockSpec(shape, lambda *_: (0,) * nd, pipeline_mode=pl.Buffered(1))


def _wspec(a, layer, cols=None, col_block=0):
    shape = a.shape[1:] if cols is None else a.shape[1:-1] + (cols,)
    idx = (layer,) + (0,) * (len(shape) - 1) + (col_block,)
    return pl.BlockSpec((None,) + shape, lambda *_: idx, pipeline_mode=pl.Buffered(1))


def _rms(x, g):
    ms = jnp.mean(x * x, axis=-1, keepdims=True)
    return x * lax.rsqrt(ms + RMS_EPS) * g


def _dot(a, b):
    return jnp.dot(a, b, preferred_element_type=F32)


def _rope_head_tables(n_tokens, d):
    dim_axis = d // 2
    t = np.arange(n_tokens)
    row = (t // GRID_W).astype(np.float64)
    col = (t % GRID_W).astype(np.float64)
    freqs = ROPE_THETA ** (-np.arange(0, dim_axis, 2, dtype=np.float64) / dim_axis)
    ar = row[:, None] * freqs[None, :]
    ac = col[:, None] * freqs[None, :]
    cos = np.concatenate([np.cos(ar), np.cos(ar), np.cos(ac), np.cos(ac)], axis=1)
    sin = np.concatenate([-np.sin(ar), np.sin(ar), -np.sin(ac), np.sin(ac)], axis=1)
    return cos, sin


def _rope_tables(n_tokens):
    cg, sg = _rope_head_tables(n_tokens, GQA_HEAD_DIM)
    cg = np.concatenate([cg, cg], axis=1)
    sg = np.concatenate([sg, sg], axis=1)
    cm32, sm32 = _rope_head_tables(n_tokens, MLA_ROPE)
    pad = LANE - MLA_NOPE - MLA_ROPE
    cm = np.concatenate([np.ones((n_tokens, MLA_NOPE)), cm32, np.zeros((n_tokens, pad))], axis=1)
    sm = np.concatenate([np.zeros((n_tokens, MLA_NOPE)), sm32, np.zeros((n_tokens, pad))], axis=1)
    return tuple(jnp.asarray(a, F32) for a in (cg, sg, cm, sm))


def _ada_body(c_ref, w_ref, b_ref, o_ref):
    c = c_ref[...]
    s = (c * jax.nn.sigmoid(c)).astype(BF16)
    o_ref[0] = _dot(s, w_ref[0].astype(BF16)) + b_ref[0]


def _ada_call(cin, w_ada, b_ada):
    depth, d, n = w_ada.shape
    cols = 2048
    rows = cin.shape[0]
    return pl.pallas_call(
        _ada_body,
        grid=(depth, n // cols),
        in_specs=[
            pl.BlockSpec((rows, d), lambda l, j: (0, 0)),
            pl.BlockSpec((1, d, cols), lambda l, j: (l, 0, j)),
            pl.BlockSpec((1, 1, cols), lambda l, j: (l, 0, j)),
        ],
        out_specs=pl.BlockSpec((1, rows, cols), lambda l, j: (l, 0, j)),
        out_shape=jax.ShapeDtypeStruct((depth, rows, n), F32),
        compiler_params=pltpu.CompilerParams(
            dimension_semantics=("arbitrary", "arbitrary"), vmem_limit_bytes=VMEM_LIMIT),
        name="ada_mod",
    )(cin, w_ada, b_ada.reshape(depth, 1, n))


def _two_head_rsqrt(xb, lo):
    x2 = xb * xb
    s_lo = jnp.sum(jnp.where(lo, x2, 0.0), axis=-1, keepdims=True)
    s_hi = jnp.sum(jnp.where(lo, 0.0, x2), axis=-1, keepdims=True)
    ms = jnp.where(lo, s_lo, s_hi) * (1.0 / GQA_HEAD_DIM)
    return lax.rsqrt(ms + RMS_EPS)


def _rot_partner_lanes(xb, lane, quarter):
    first = jnp.bitwise_and(lane, quarter) == 0
    return jnp.where(first, pltpu.roll(xb, LANE - quarter, axis=1), pltpu.roll(xb, quarter, axis=1))


def _store_value_heads(v_ref, first, blk, lo):
    v_ref[first] = jnp.where(lo, blk, 1.0).astype(BF16)
    v_ref[first + 1] = jnp.where(lo, 1.0, blk).astype(BF16)


def _inproj_body(latent, names, *refs):
    r = dict(zip(names, refs))
    x = r["x"][...]
    tm = x.shape[0]
    mod = r["mod"][0]
    sh1 = mod[:, 0:D_MODEL]
    sc1 = mod[:, D_MODEL:2 * D_MODEL]
    h = (_rms(x, r["gmix"][...]) * (1.0 + sc1) + sh1).astype(BF16)
    z = _dot(h, r["w1"][...])

    lane = lax.broadcasted_iota(jnp.int32, (tm, LANE), 1)
    lo = lane < GQA_HEAD_DIM

    if latent:
        cg = r["cg"][...]
        sg = r["sg"][...]
        cm = r["cm"][...]
        sm = r["sm"][...]

    qn = _rms(z[:, _W1_QA:_W1_QA + Q_LORA], r["gqa"][...]).astype(BF16)
    qm = _dot(qn, r["wqb"][...])
    if latent:
        qms = _dot(qn, r["wqbsw"][...])
    for hh in range(MLA_HEADS):
        blk = qm[:, hh * LANE:(hh + 1) * LANE]
        if latent:
            blk = blk * cm + qms[:, hh * LANE:(hh + 1) * LANE] * sm
        r["q_mla"][hh] = blk.astype(BF16)

    ckv = _rms(z[:, _W1_CKV:_W1_CKV + KV_LORA], r["gkva"][...])
    kr = z[:, _W1_KR:_W1_KR + LANE]
    if latent:
        kr = kr * cm + _rot_partner_lanes(kr, lane, MLA_ROPE // 4) * sm
    else:
        r["ckv_out"][...] = ckv
        r["kr_out"][...] = kr[:, MLA_NOPE:MLA_NOPE + MLA_ROPE]
    ckv_b = ckv.astype(BF16)
    kfull = _dot(jnp.concatenate([ckv_b, kr.astype(BF16)], axis=1), r["wkp"][...])
    for hh in range(MLA_HEADS):
        r["k_mla"][hh] = kfull[:, hh * LANE:(hh + 1) * LANE].astype(BF16)
    vfull = _dot(ckv_b, r["wv"][...])
    for pp in range(N_PAIRS):
        _store_value_heads(r["v_mla"], 2 * pp, vfull[:, pp * LANE:(pp + 1) * LANE], lo)

    gq2 = r["gq2"][...]
    q_scale = GQA_HEAD_DIM ** -0.5
    for j in range(N_PAIRS):
        xb = z[:, _W1_GQ + j * LANE:_W1_GQ + (j + 1) * LANE]
        rs = _two_head_rsqrt(xb, lo)
        y = xb * rs * gq2
        if latent:
            y = y * cg + _rot_partner_lanes(y, lane, GQA_HEAD_DIM // 4) * sg
        y = y * q_scale
        r["q_gqa"][2 * j] = jnp.where(lo, y, 0.0).astype(BF16)
        r["q_gqa"][2 * j + 1] = jnp.where(lo, 0.0, y).astype(BF16)
    kb = z[:, _W1_GK:_W1_GK + LANE]
    rs = _two_head_rsqrt(kb, lo)
    kg = kb * rs * r["gk2"][...]
    if latent:
        kg = kg * cg + _rot_partner_lanes(kg, lane, GQA_HEAD_DIM // 4) * sg
    vg = z[:, _W1_GV:_W1_GV + LANE]
    if not latent:
        r["kg_out"][...] = kg
        r["vg_out"][...] = vg
    r["k_gqa"][...] = kg.astype(BF16)
    _store_value_heads(r["v_gqa"], 0, vg, lo)

    r["pool"][...] = z[:, _W1_POOL:_W1_POOL + POOL_WIDTH]


def _inproj_call(x, mod4, layer, mod_base, seq_len, tm, latent, wl, tabs):
    n = x.shape[0]
    nt = n // tm
    per_seq = seq_len // tm if latent else 1

    def tile(i):
        return (i, 0)

    def mod_idx(i):
        if latent:
            return (layer, mod_base + i // per_seq, 0, 0)
        return (layer, mod_base, 0, 0)

    names = ["x", "mod", "w1", "gmix", "gqa", "wqb", "gkva", "wkp", "wv", "gq2", "gk2"]
    args = [x, mod4, wl["w1"]] + [wl[nm] for nm in names[3:]]
    specs = [pl.BlockSpec((tm, D_MODEL), tile), pl.BlockSpec((None, 1, 1, 6 * D_MODEL), mod_idx),
             _wspec(wl["w1"], layer)]
    specs += [_wspec(a, layer) for a in args[3:]]
    if latent:
        names.append("wqbsw")
        args.append(wl["wqbsw"])
        specs.append(_wspec(wl["wqbsw"], layer))
        for nm, a in zip(("cg", "sg", "cm", "sm"), tabs):
            names.append(nm)
            args.append(a)
            specs.append(pl.BlockSpec((tm, LANE), lambda i: (i % per_seq, 0)))

    head_spec = pl.BlockSpec((MLA_HEADS, tm, LANE), lambda i: (0, i, 0))
    kvh_spec = pl.BlockSpec((GQA_KV_HEADS, tm, LANE), lambda i: (0, i, 0))
    out_names = ["q_mla", "k_mla", "v_mla", "q_gqa", "k_gqa", "v_gqa", "pool"]
    out_shapes = [
        jax.ShapeDtypeStruct((MLA_HEADS, n, LANE), BF16),
        jax.ShapeDtypeStruct((MLA_HEADS, n, LANE), BF16),
        jax.ShapeDtypeStruct((MLA_HEADS, n, LANE), BF16),
        jax.ShapeDtypeStruct((GQA_HEADS, n, LANE), BF16),
        jax.ShapeDtypeStruct((n, LANE), BF16),
        jax.ShapeDtypeStruct((GQA_KV_HEADS, n, LANE), BF16),
        jax.ShapeDtypeStruct((n, POOL_WIDTH), F32),
    ]
    out_specs = [head_spec, head_spec, head_spec, head_spec,
                 pl.BlockSpec((tm, LANE), tile), kvh_spec,
                 pl.BlockSpec((tm, POOL_WIDTH), tile)]
    if not latent:
        out_names += ["ckv_out", "kr_out", "kg_out", "vg_out"]
        out_shapes += [jax.ShapeDtypeStruct((n, KV_LORA), F32), jax.ShapeDtypeStruct((n, MLA_ROPE), F32),
                       jax.ShapeDtypeStruct((n, LANE), F32), jax.ShapeDtypeStruct((n, LANE), F32)]
        out_specs += [pl.BlockSpec((tm, KV_LORA), tile), pl.BlockSpec((tm, MLA_ROPE), tile),
                      pl.BlockSpec((tm, LANE), tile), pl.BlockSpec((tm, LANE), tile)]

    outs = pl.pallas_call(
        functools.partial(_inproj_body, latent, names + out_names),
        grid=(nt,),
        in_specs=specs,
        out_specs=out_specs,
        out_shape=out_shapes,
        compiler_params=pltpu.CompilerParams(
            dimension_semantics=("arbitrary",), vmem_limit_bytes=VMEM_LIMIT),
        name="inproj_lat" if latent else "inproj_ctx",
    )(*args)
    return dict(zip(out_names, outs))


def _cachekv_body(ckv_ref, kr_ref, kg_ref, vg_ref, wkp_ref, wv_ref, k_ref, v_ref, kgo_ref, vgo_ref):
    rows = ckv_ref.shape[0] * ckv_ref.shape[1]
    ckv_b = ckv_ref[...].reshape(rows, LANE).astype(BF16)
    lo = lax.broadcasted_iota(jnp.int32, (rows, LANE), 1) < MLA_V
    kr_b = kr_ref[...].reshape(rows, LANE).astype(BF16)
    kfull = _dot(jnp.concatenate([ckv_b, kr_b], axis=1), wkp_ref[...])
    kgo_ref[0] = kg_ref[...].reshape(rows, LANE).astype(BF16)
    for hh in range(MLA_HEADS):
        k_ref[hh] = kfull[:, hh * LANE:(hh + 1) * LANE].astype(BF16)
    vfull = _dot(ckv_b, wv_ref[...])
    for pp in range(N_PAIRS):
        _store_value_heads(v_ref, 2 * pp, vfull[:, pp * LANE:(pp + 1) * LANE], lo)
    _store_value_heads(vgo_ref, 0, vg_ref[...].reshape(rows, LANE), lo)


def _cachekv_call(caches, layer, wl):
    bsz, _, past, _ = caches[0].shape
    n = bsz * past
    cache_spec = pl.BlockSpec((bsz, None, past, LANE), lambda i: (0, layer, 0, 0))
    return pl.pallas_call(
        _cachekv_body,
        grid=(1,),
        in_specs=[cache_spec] * 4 + [_wspec(wl["wkp"], layer), _wspec(wl["wv"], layer)],
        out_specs=[_const_spec((MLA_HEADS, n, LANE)), _const_spec((MLA_HEADS, n, LANE)),
                   _const_spec((1, n, LANE)), _const_spec((GQA_KV_HEADS, n, LANE))],
        out_shape=[jax.ShapeDtypeStruct((MLA_HEADS, n, LANE), BF16),
                   jax.ShapeDtypeStruct((MLA_HEADS, n, LANE), BF16),
                   jax.ShapeDtypeStruct((1, n, LANE), BF16),
                   jax.ShapeDtypeStruct((GQA_KV_HEADS, n, LANE), BF16)],
        compiler_params=pltpu.CompilerParams(
            dimension_semantics=("arbitrary",), vmem_limit_bytes=VMEM_LIMIT),
        name="cache_kv",
    )(*caches, wl["wkp"], wl["wv"])


def _normalise_pair(acc_a, acc_b, lo):
    num = jnp.where(lo, acc_a, acc_b)
    den = pltpu.roll(jnp.where(lo, acc_b, acc_a), MLA_V, axis=1)
    return num / den


def _attn_body(group, seq_len, shared_kv, exp_scale, q_ref, k_ref, v_ref, o_ref):
    lo = lax.broadcasted_iota(jnp.int32, (CTX_Q_ROWS, LANE), 1) < MLA_V
    nt_dims = (((1,), (1,)), ((), ()))
    units = [(pr, qb, e) for pr in range(N_PAIRS) for qb in range(seq_len // CTX_Q_ROWS) for e in range(2)]

    def one_batch_row(g, carry):
        base = pl.multiple_of(g * seq_len, seq_len)
        rows = pl.ds(base, seq_len)

        scores, row_max, accs = {}, {}, {}
        for i in range(len(units) + 2):
            if i < len(units):
                pr, qb, e = units[i]
                qrows = pl.ds(base + qb * CTX_Q_ROWS, CTX_Q_ROWS)
                q = q_ref[2 * pr + e, qrows, :]
                k = k_ref[0 if shared_kv else 2 * pr + e, rows, :]
                scores[i] = lax.dot_general(q, k, nt_dims, preferred_element_type=F32)
            if 0 <= i - 1 < len(units):
                row_max[i - 1] = jnp.max(scores[i - 1], axis=-1, keepdims=True)
            if 0 <= i - 2 < len(units):
                u = i - 2
                pr, qb, e = units[u]
                qrows = pl.ds(base + qb * CTX_Q_ROWS, CTX_Q_ROWS)
                v = v_ref[e if shared_kv else 2 * pr + e, rows, :]
                p = jnp.exp2((scores.pop(u) - row_max.pop(u)) * exp_scale)
                accs[u] = _dot(p.astype(BF16), v)
                if e == 1:
                    o_ref[pr, qrows, :] = _normalise_pair(accs.pop(u - 1), accs.pop(u), lo).astype(BF16)
        return carry

    lax.fori_loop(0, group, one_batch_row, 0)


def _attn_call(q, k, v, batch, seq_len, group, scale, shared_kv, name):
    n = q.shape[1]
    rows = group * seq_len
    exp_scale = scale * math.log2(math.e)

    def spec(heads):
        return pl.BlockSpec((heads, rows, LANE), lambda b: (0, b, 0))

    return pl.pallas_call(
        functools.partial(_attn_body, group, seq_len, shared_kv, exp_scale),
        grid=(batch // group,),
        in_specs=[spec(q.shape[0]), spec(k.shape[0]), spec(v.shape[0])],
        out_specs=pl.BlockSpec((N_PAIRS, rows, LANE), lambda b: (0, b, 0)),
        out_shape=jax.ShapeDtypeStruct((N_PAIRS, n, LANE), BF16),
        compiler_params=pltpu.CompilerParams(
            dimension_semantics=("arbitrary",), vmem_limit_bytes=VMEM_LIMIT),
        name=name,
    )(q, k, v)


def _attn_pipe_body(shared_kv, exp_scale, tq, pairs, q_ref, k_ref, v_ref, kc_ref, vc_ref, o_ref,
                    s0, s1, mr0, mr1, ac0, ac1):
    seq = k_ref.shape[1]
    past = kc_ref.shape[1]
    nq = seq // tq
    n_units = pairs * nq
    n_new = seq // KEY_TILE
    n_tiles = n_new + past // KEY_TILE
    s_scr = (s0, s1)
    mrun_scr = (mr0, mr1)
    acc_scr = (ac0, ac1)
    nt_dims = (((1,), (1,)), ((), ()))
    lo = lax.broadcasted_iota(jnp.int32, (tq, LANE), 1) < MLA_V

    def pair_rows(t):
        j = t % nq
        return t // nq, pl.ds(pl.multiple_of(j * tq, tq), tq)

    def kv_tile(new_ref, cache_ref, head, blk):
        if blk < n_new:
            return new_ref[head, blk * KEY_TILE:(blk + 1) * KEY_TILE, :]
        blk -= n_new
        return cache_ref[head, blk * KEY_TILE:(blk + 1) * KEY_TILE, :]

    def finish_prev(e_prev, t_prev):
        if e_prev == 1:
            pair, rows = pair_rows(t_prev)
            o_ref[pair, rows, :] = _normalise_pair(ac0[...], ac1[...], lo).astype(BF16)

    def region(t_scores, e_scores, t_cur, e_cur, t_prev):
        if t_prev is not None:
            finish_prev(1 - e_cur, t_prev)
        if e_scores is not None:
            pair, rows = pair_rows(t_scores)
            q = q_ref[2 * pair + e_scores, rows, :]
            k_head = 0 if shared_kv else 2 * pair + e_scores
        if e_cur is not None:
            v_head = e_cur if shared_kv else 2 * (t_cur // nq) + e_cur
            m = jnp.max(mrun_scr[e_cur][...], axis=-1, keepdims=True)
            m_b = jnp.broadcast_to(m, (tq, LANE))
        acc = None
        for blk in range(n_tiles):
            cols = slice(blk * KEY_TILE, (blk + 1) * KEY_TILE)
            if e_scores is not None:
                s_t = lax.dot_general(q, kv_tile(k_ref, kc_ref, k_head, blk), nt_dims,
                                      preferred_element_type=F32)
                s_scr[e_scores][:, cols] = s_t
                m_t = jnp.maximum(s_t[:, 0:LANE], s_t[:, LANE:2 * LANE])
                if blk > 0:
                    m_t = jnp.maximum(m_t, mrun_scr[e_scores][...])
                mrun_scr[e_scores][...] = m_t
            if e_cur is not None:
                p_parts = []
                for hh in range(KEY_TILE // LANE):
                    c0 = blk * KEY_TILE + hh * LANE
                    s_h = s_scr[e_cur][:, c0:c0 + LANE]
                    p_parts.append(jnp.exp2((s_h - m_b) * exp_scale).astype(BF16))
                p_t = jnp.concatenate(p_parts, axis=1)
                part = _dot(p_t, kv_tile(v_ref, vc_ref, v_head, blk))
                acc = part if acc is None else acc + part
        if e_cur is not None:
            acc_scr[e_cur][...] = acc

    region(0, 0, None, None, None)
    region(0, 1, 0, 0, None)
    region(1, 0, 0, 1, 0)

    def body(t, carry):
        @pl.when(t > 0)
        def _():
            region(t, 1, t, 0, t - 1)

        @pl.when(t < n_units)
        def _():
            region(t + 1, 0, t, 1, t)

        return carry

    lax.fori_loop(1, n_units - 1, body, 0)
    region(n_units - 1, 1, n_units - 1, 0, n_units - 2)
    region(None, None, n_units - 1, 1, n_units - 1)
    finish_prev(1, n_units - 1)


def _attn_pipe_call(q, k, v, kc, vc, batch, seq_len, tq, scale, shared_kv, name, pairs=LAT_PAIRS):
    n = q.shape[1]
    past = kc.shape[1] // batch
    exp_scale = scale * math.log2(math.e)

    def spec(a, rows):
        heads = a.shape[0]
        if heads < 2 * N_PAIRS:
            return pl.BlockSpec((heads, rows, LANE), lambda b, p: (0, b, 0))
        return pl.BlockSpec((2 * pairs, rows, LANE), lambda b, p: (p, b, 0))

    total = seq_len + past
    return pl.pallas_call(
        functools.partial(_attn_pipe_body, shared_kv, exp_scale, tq, pairs),
        grid=(batch, N_PAIRS // pairs),
        in_specs=[spec(q, seq_len), spec(k, seq_len), spec(v, seq_len), spec(kc, past), spec(vc, past)],
        out_specs=pl.BlockSpec((pairs, seq_len, LANE), lambda b, p: (p, b, 0)),
        out_shape=jax.ShapeDtypeStruct((N_PAIRS, n, LANE), BF16),
        scratch_shapes=[pltpu.VMEM((tq, total), F32), pltpu.VMEM((tq, total), F32)]
        + [pltpu.VMEM((tq, LANE), F32) for _ in range(4)],
        compiler_params=pltpu.CompilerParams(
            dimension_semantics=("arbitrary", "arbitrary"), vmem_limit_bytes=VMEM_LIMIT),
        name=name,
    )(q, k, v, kc, vc)


def _seq_pos(tm, seq_len):
    i = pl.program_id(0)
    row = lax.broadcasted_iota(jnp.int32, (tm, 1), 0) + i * tm
    return jnp.bitwise_and(row, seq_len - 1)


def _merge_body(seq_len, x_ref, mod_ref, gmix_ref, wg_ref, a_ref, b_ref, pc_ref, pl_ref, pr_ref,
                wpool_ref, pscale_ref, wa_ref, wb_ref, wc_ref, wo_ref, o_ref, merged_scr, ab_scr):
    x = x_ref[...]
    tm = x.shape[0]
    mod = mod_ref[0]
    sh1 = mod[:, 0:D_MODEL]
    sc1 = mod[:, D_MODEL:2 * D_MODEL]
    g1 = mod[:, 2 * D_MODEL:3 * D_MODEL]
    h = (_rms(x, gmix_ref[...]) * (1.0 + sc1) + sh1).astype(BF16)

    pos = _seq_pos(tm, seq_len)
    pscale = pscale_ref[...]
    centre_rows = pc_ref[...]
    if seq_len >= tm:
        start = (pl.program_id(0) * tm) % seq_len
        left = jnp.where(start == 0, 0.0, pl_ref[...])
        right = jnp.where(start + tm == seq_len, 0.0, pr_ref[...])
        blocks = [jnp.concatenate([left, centre_rows, right], axis=0)]
    else:
        zeros = jnp.zeros((HALO, POOL_WIDTH), F32)
        blocks = [jnp.concatenate([zeros, centre_rows[s * seq_len:(s + 1) * seq_len], zeros], axis=0)
                  for s in range(tm // seq_len)]

    def pool_group(g):
        w = POOL_WINDOWS[g]
        gcols = slice(g * POOL_GROUP, (g + 1) * POOL_GROUP)
        sums = []
        for blk in blocks:
            rows = blk.shape[0] - 2 * HALO
            eg = blk[:, gcols]
            wsum = eg[HALO:HALO + rows]
            for dlt in range(-(w // 2), w - w // 2):
                if dlt != 0:
                    wsum = wsum + eg[HALO + dlt:HALO + dlt + rows]
            sums.append(wsum)
        wsum = sums[0] if len(sums) == 1 else jnp.concatenate(sums, axis=0)
        cnt = jnp.minimum(pos + (w - w // 2), seq_len) - jnp.maximum(pos - w // 2, 0)
        return (wsum / cnt.astype(F32) - centre_rows[:, gcols]).astype(BF16)

    def pool_pair_map(p, pooled_a, pooled_b):
        both = jnp.concatenate([pooled_a, pooled_b], axis=1)
        return _dot(both, wpool_ref[p]) * pscale[:, 2 * p * POOL_GROUP:2 * (p + 1) * POOL_GROUP]

    def gated(branch, src, w_ref, j):
        c0 = branch * D_MODEL + j * FF_CHUNK
        gate = jax.nn.sigmoid(_dot(h, wg_ref[:, c0:c0 + FF_CHUNK]))
        return gate * _dot(src, w_ref[:, j * FF_CHUNK:(j + 1) * FF_CHUNK])

    a_out = jnp.concatenate([a_ref[p] for p in range(N_PAIRS)], axis=1)
    b_out = jnp.concatenate([b_ref[p] for p in range(N_PAIRS)], axis=1)
    n_chunks = D_MODEL // FF_CHUNK
    pooled = []
    c_parts = []
    for j in range(n_chunks):
        cols = slice(j * FF_CHUNK, (j + 1) * FF_CHUNK)
        for g in range(j * len(POOL_WINDOWS) // n_chunks, (j + 1) * len(POOL_WINDOWS) // n_chunks):
            pooled.append(pool_group(g))
            if g % 2 == 1:
                c_parts.append(pool_pair_map(g // 2, pooled[g - 1], pooled[g]))
        ab_scr[:, cols] = gated(0, a_out, wa_ref, j) + gated(1, b_out, wb_ref, j)
    c_out = jnp.concatenate(c_parts, axis=-1).astype(BF16)
    for j in range(n_chunks):
        cols = slice(j * FF_CHUNK, (j + 1) * FF_CHUNK)
        merged_scr[:, cols] = (ab_scr[:, cols] + gated(2, c_out, wc_ref, j)).astype(BF16)
    o_ref[...] = x + g1 * _dot(merged_scr[...], wo_ref[...])


def _halo_specs(tm, width, n):
    blocks = tm // HALO
    last = n // HALO - 1
    left = pl.BlockSpec((HALO, width), lambda i: (jnp.maximum(i * blocks - 1, 0), 0))
    right = pl.BlockSpec((HALO, width), lambda i: (jnp.minimum((i + 1) * blocks, last), 0))
    return left, right


def _mod_spec(layer, mod_base, per_seq, latent):
    if latent:
        return pl.BlockSpec((None, 1, 1, 6 * D_MODEL), lambda i: (layer, mod_base + i // per_seq, 0, 0))
    return pl.BlockSpec((None, 1, 1, 6 * D_MODEL), lambda i: (layer, mod_base, 0, 0))


def _merge_call(x, mod4, layer, mod_base, seq_len, tm, latent, a_out, b_out, pool_in, wl):
    n = x.shape[0]
    per_seq = seq_len // tm if latent else 1

    def tile(i):
        return (i, 0)

    left, right = _halo_specs(tm, POOL_WIDTH, n)
    consts = [wl["wpool"], wl["pscale"], wl["wa"], wl["wb"], wl["wc"], wl["wo"]]
    in_specs = [pl.BlockSpec((tm, D_MODEL), tile), _mod_spec(layer, mod_base, per_seq, latent),
                _wspec(wl["gmix"], layer), _wspec(wl["wg"], layer),
                pl.BlockSpec((N_PAIRS, tm, LANE), lambda i: (0, i, 0)),
                pl.BlockSpec((N_PAIRS, tm, LANE), lambda i: (0, i, 0)),
                pl.BlockSpec((tm, POOL_WIDTH), tile), left, right]
    in_specs += [_wspec(a, layer) for a in consts]
    return pl.pallas_call(
        functools.partial(_merge_body, seq_len),
        grid=(n // tm,),
        in_specs=in_specs,
        out_specs=pl.BlockSpec((tm, D_MODEL), tile),
        out_shape=jax.ShapeDtypeStruct((n, D_MODEL), F32),
        scratch_shapes=[pltpu.VMEM((tm, D_MODEL), BF16), pltpu.VMEM((tm, D_MODEL), F32)],
        compiler_params=pltpu.CompilerParams(
            dimension_semantics=("arbitrary",), vmem_limit_bytes=VMEM_LIMIT),
        name="merge_lat" if latent else "merge_ctx",
    )(x, mod4, wl["gmix"], wl["wg"], a_out, b_out, pool_in, pool_in, pool_in, *consts)


def _ffn_body(seq_len, final, x_ref, xl_ref, xr_ref, mod_ref, gffn_ref, wug_ref, wuv_ref, cw_ref, cb_ref,
              wd_ref, gfin_ref, o_ref, act_scr):
    x = x_ref[...]
    tm = x.shape[0]
    mod = mod_ref[0]
    sh2 = mod[:, 3 * D_MODEL:4 * D_MODEL]
    sc2 = mod[:, 4 * D_MODEL:5 * D_MODEL]
    g2 = mod[:, 5 * D_MODEL:6 * D_MODEL]
    xe = jnp.concatenate([xl_ref[...], x, xr_ref[...]], axis=0)
    h2e = _rms(xe, gffn_ref[...]) * (1.0 + sc2) + sh2
    h2 = h2e[HALO:HALO + tm].astype(BF16)
    h2e = h2e.astype(BF16)

    pos = _seq_pos(tm, seq_len)
    has_prev = pos >= 1
    has_next = pos <= seq_len - 2
    cw = cw_ref[...]
    cb = cb_ref[...]
    for j in range(D_FF // FF_CHUNK):
        cols = slice(j * FF_CHUNK, (j + 1) * FF_CHUNK)
        ge = _dot(h2e, wug_ref[:, cols])
        g = (jnp.where(has_prev, ge[HALO - 1:HALO - 1 + tm], 0.0) * cw[0:1, cols]
             + ge[HALO:HALO + tm] * cw[1:2, cols]
             + jnp.where(has_next, ge[HALO + 1:HALO + 1 + tm], 0.0) * cw[2:3, cols]
             + cb[:, cols])
        val = _dot(h2, wuv_ref[:, cols])
        act_scr[:, cols] = (g * jax.nn.sigmoid(g) * val).astype(BF16)
    y = x + g2 * _dot(act_scr[...], wd_ref[...])
    if final:
        y = _rms(y, gfin_ref[...])
    o_ref[...] = y


def _ffn_call(x, mod4, layer, mod_base, seq_len, tm, latent, final, wl, g_final):
    n = x.shape[0]
    per_seq = seq_len // tm if latent else 1

    def tile(i):
        return (i, 0)

    left, right = _halo_specs(tm, D_MODEL, n)
    consts = [wl["gffn"], wl["wu"], wl["wu"], wl["cw"], wl["cb"], wl["wd"], g_final]
    in_specs = [pl.BlockSpec((tm, D_MODEL), tile), left, right, _mod_spec(layer, mod_base, per_seq, latent),
                _wspec(wl["gffn"], layer), _wspec(wl["wu"], layer, cols=D_FF, col_block=0),
                _wspec(wl["wu"], layer, cols=D_FF, col_block=1), _wspec(wl["cw"], layer),
                _wspec(wl["cb"], layer), _wspec(wl["wd"], layer), _const_spec(g_final.shape)]
    return pl.pallas_call(
        functools.partial(_ffn_body, seq_len, final),
        grid=(n // tm,),
        in_specs=in_specs,
        out_specs=pl.BlockSpec((tm, D_MODEL), tile),
        out_shape=jax.ShapeDtypeStruct((n, D_MODEL), F32),
        scratch_shapes=[pltpu.VMEM((tm, D_FF), BF16)],
        compiler_params=pltpu.CompilerParams(
            dimension_semantics=("arbitrary",), vmem_limit_bytes=VMEM_LIMIT),
        name="ffn_lat" if latent else "ffn_ctx",
    )(x, x, x, mod4, *consts)


def _rot_partner(a):
    d = a.shape[-1]
    return jnp.flip(a.reshape(a.shape[:-1] + (2, 2, d // 4)), axis=-2).reshape(a.shape)


def _prep_weights(g_norm_mix, w_in, g_q_a, w_q_b, g_kv_a, w_kv_b, g_q_gqa, g_k_gqa, w_pool, pool_scale,
                  w_br_a, w_br_b, w_br_c, w_out, g_norm_ffn, w_up, conv_w, conv_b, w_down):
    depth = w_in.shape[0]
    hd = GQA_HEAD_DIM
    grp = GQA_HEADS // GQA_KV_HEADS

    def lane_pad(a, before, width=LANE):
        cfg = [(0, 0)] * (a.ndim - 1) + [(before, width - before - a.shape[-1])]
        return jnp.pad(a, cfg)

    def pair_heads(a):
        lead = a.shape[:-1]
        return a.reshape(lead + (GQA_KV_HEADS, grp, hd)).swapaxes(-3, -2).reshape(lead + (GQA_HEADS * hd,))

    w1 = jnp.concatenate(
        [w_in[:, :, _OFF_QA:_OFF_KR], pair_heads(w_in[:, :, _OFF_GQ:_OFF_GK]), w_in[:, :, _OFF_GK:_OFF_GATE],
         lane_pad(w_in[:, :, _OFF_KR:_OFF_GQ], MLA_NOPE)], axis=-1).astype(BF16)
    assert w1.shape[-1] == _W1_COLS

    qb = w_q_b.reshape(depth, Q_LORA, MLA_HEADS, MLA_NOPE + MLA_ROPE)
    wqb = lane_pad(qb, 0).reshape(depth, Q_LORA, MLA_HEADS * LANE).astype(BF16)
    wqbsw = lane_pad(_rot_partner(qb[..., MLA_NOPE:]), MLA_NOPE).reshape(
        depth, Q_LORA, MLA_HEADS * LANE).astype(BF16)

    kvb = w_kv_b.reshape(depth, KV_LORA, MLA_HEADS, MLA_NOPE + MLA_V)
    wk = lane_pad(kvb[..., :MLA_NOPE], 0).reshape(depth, KV_LORA, MLA_HEADS * LANE).astype(BF16)
    wv = kvb[..., MLA_NOPE:].reshape(depth, KV_LORA, MLA_HEADS * MLA_V).astype(BF16)

    pk = np.zeros((LANE, MLA_HEADS * LANE), np.float32)
    for hh in range(MLA_HEADS):
        for t in range(MLA_ROPE):
            pk[MLA_NOPE + t, hh * LANE + MLA_NOPE + t] = 1.0
    wkp = jnp.concatenate([wk, jnp.broadcast_to(jnp.asarray(pk, BF16), (depth,) + pk.shape)], axis=1)

    wp = w_pool.reshape(depth, len(POOL_WINDOWS) // 2, 2, POOL_GROUP, POOL_GROUP)
    zero = jnp.zeros_like(wp[:, :, 0])
    wpool2 = jnp.concatenate([jnp.concatenate([wp[:, :, 0], zero], axis=-1),
                              jnp.concatenate([zero, wp[:, :, 1]], axis=-1)], axis=-2).astype(BF16)

    row = lambda a: a[:, None, :]
    two = lambda a: jnp.concatenate([a, a], axis=-1)
    wb = w_br_b.reshape(depth, GQA_KV_HEADS, grp, hd, D_MODEL).swapaxes(1, 2).reshape(depth, GQA_HEADS * hd, D_MODEL)
    return dict(
        gmix=row(g_norm_mix), w1=w1, gqa=row(g_q_a), wqb=wqb, wqbsw=wqbsw, gkva=row(g_kv_a), wkp=wkp,
        wv=wv, gq2=row(two(g_q_gqa)), gk2=row(two(g_k_gqa)),
        wg=w_in[:, :, _OFF_GATE:].astype(BF16), wpool=wpool2, pscale=row(pool_scale),
        wa=w_br_a.astype(BF16), wb=wb.astype(BF16), wc=w_br_c.astype(BF16), wo=w_out.astype(BF16),
        gffn=row(g_norm_ffn), wu=w_up.astype(BF16), cw=conv_w, cb=row(conv_b), wd=w_down.astype(BF16),
    )


def _layer(x, mod4, layer, mod_base, batch, seq_len, latent, final, wl, g_final, tabs, caches):
    tm = TOKEN_TILE
    tq = Q_TILE
    assert tm % seq_len == 0 or seq_len % tm == 0
    pj = _inproj_call(x, mod4, layer, mod_base, seq_len, tm, latent, wl, tabs)
    if caches is not None:
        kc_m, vc_m, kc_g, vc_g = _cachekv_call(caches, layer, wl)
    mla_scale = (MLA_NOPE + MLA_ROPE) ** -0.5
    n = x.shape[0]
    k_g = pj["k_gqa"].reshape(1, n, LANE)
    if latent:
        a_out = _attn_pipe_call(pj["q_mla"], pj["k_mla"], pj["v_mla"], kc_m, vc_m, batch, seq_len, tq,
                                mla_scale, False, "attn_mla_lat")
        b_out = _attn_pipe_call(pj["q_gqa"], k_g, pj["v_gqa"], kc_g, vc_g, batch, seq_len, tq, 1.0, True,
                                "attn_gqa_lat")
    else:
        a_out = _attn_call(pj["q_mla"], pj["k_mla"], pj["v_mla"], batch, seq_len, CTX_GROUP,
                           mla_scale, False, "attn_mla_ctx")
        b_out = _attn_call(pj["q_gqa"], k_g, pj["v_gqa"], batch, seq_len, CTX_GROUP, 1.0, True,
                           "attn_gqa_ctx")
    x1 = _merge_call(x, mod4, layer, mod_base, seq_len, tm, latent, a_out, b_out, pj["pool"], wl)
    x2 = _ffn_call(x1, mod4, layer, mod_base, seq_len, tm, latent, final, wl, g_final)
    return x2, pj


def kernel(x_prompt, x_sample, c, cache_mla_ckv, cache_mla_krope, cache_gqa_k, cache_gqa_v, c_ctx, w_ada, b_ada, g_norm_mix, w_in, g_q_a, w_q_b, g_kv_a, w_kv_b, g_q_gqa, g_k_gqa, w_pool, pool_scale, w_br_a, w_br_b, w_br_c, w_out, g_norm_ffn, w_up, conv_w, conv_b, w_down, g_final):
    depth = w_in.shape[0]
    bc, tc, _ = x_prompt.shape
    bl, tl, _ = x_sample.shape
    past = cache_mla_ckv.shape[2]
    mod_rows = 8
    assert 1 + bl <= mod_rows and tc & (tc - 1) == 0 and tl & (tl - 1) == 0

    cin = jnp.concatenate([c_ctx[None, :], c, jnp.zeros((mod_rows - 1 - bl, D_MODEL), F32)], axis=0)
    mod4 = _ada_call(cin, w_ada, b_ada).reshape(depth, mod_rows, 1, 6 * D_MODEL)
    tabs = _rope_tables(tl)
    g_fin = g_final.reshape(1, D_MODEL)
    wl = _prep_weights(g_norm_mix, w_in, g_q_a, w_q_b, g_kv_a, w_kv_b, g_q_gqa, g_k_gqa, w_pool,
                       pool_scale, w_br_a, w_br_b, w_br_c, w_out, g_norm_ffn, w_up, conv_w, conv_b, w_down)
    kr_pad = jnp.pad(cache_mla_krope, ((0, 0), (0, 0), (0, 0), (MLA_NOPE, LANE - MLA_NOPE - MLA_ROPE)))
    caches = (cache_mla_ckv, kr_pad, cache_gqa_k.reshape(bl, depth, past, LANE),
              cache_gqa_v.reshape(bl, depth, past, LANE))

    xc = x_prompt.reshape(bc * tc, D_MODEL)
    xl = x_sample.reshape(bl * tl, D_MODEL)
    st_ckv, st_kr, st_k, st_v = [], [], [], []
    for l in range(depth):
        final = l == depth - 1
        xc, pj = _layer(xc, mod4, l, 0, bc, tc, False, final, wl, g_fin, None, None)
        st_ckv.append(pj["ckv_out"].reshape(bc, tc, KV_LORA))
        st_kr.append(pj["kr_out"].reshape(bc, tc, MLA_ROPE))
        st_k.append(pj["kg_out"].reshape(bc, tc, GQA_KV_HEADS, GQA_HEAD_DIM))
        st_v.append(pj["vg_out"].reshape(bc, tc, GQA_KV_HEADS, GQA_HEAD_DIM))
        xl, _ = _layer(xl, mod4, l, 1, bl, tl, True, final, wl, g_fin, tabs, caches)

    return (xc.reshape(bc, tc, D_MODEL), xl.reshape(bl, tl, D_MODEL),
            jnp.stack(st_ckv, axis=1), jnp.stack(st_kr, axis=1),
            jnp.stack(st_k, axis=1), jnp.stack(st_v, axis=1))
```

```python
import functools
import math

import numpy as np
import jax
import jax.numpy as jnp
from jax import lax
from jax.experimental import pallas as pl
from jax.experimental.pallas import tpu as pltpu

D_MODEL = 1024
GRID_W = 64
RMS_EPS = 1e-6
ROPE_THETA = 10000.0
MLA_HEADS = 8
MLA_NOPE = 64
MLA_ROPE = 32
MLA_V = 64
Q_LORA = 256
KV_LORA = 128
GQA_HEADS = 8
GQA_KV_HEADS = 2
GQA_HEAD_DIM = 64
POOL_WINDOWS = (2, 4, 8, 16)
POOL_GROUP = 128
POOL_WIDTH = POOL_GROUP * len(POOL_WINDOWS)
D_FF = 2816

LANE = 128
HALO = 8
FF_CHUNK = 256
KEY_TILE = 256
CTX_GROUP = 8
CTX_Q_ROWS = 256
TOKEN_TILE = 512
Q_TILE = 512
LAT_PAIRS = 2
N_PAIRS = MLA_HEADS // 2
VMEM_LIMIT = 56 * 1024 * 1024

BF16 = jnp.bfloat16
F32 = jnp.float32

_OFF_QA = 0
_OFF_CKV = _OFF_QA + Q_LORA
_OFF_KR = _OFF_CKV + KV_LORA
_OFF_GQ = _OFF_KR + MLA_ROPE
_OFF_GK = _OFF_GQ + GQA_HEADS * GQA_HEAD_DIM
_OFF_GV = _OFF_GK + GQA_KV_HEADS * GQA_HEAD_DIM
_OFF_POOL = _OFF_GV + GQA_KV_HEADS * GQA_HEAD_DIM
_OFF_GATE = _OFF_POOL + POOL_WIDTH

_W1_QA = 0
_W1_CKV = 256
_W1_GQ = 384
_W1_GK = 896
_W1_GV = 1024
_W1_POOL = 1152
_W1_KR = 1664
_W1_COLS = 1792


def _const_spec(shape):
    nd = len(shape)
    return pl.BlockSpec(shape, lambda *_: (0,) * nd, pipeline_mode=pl.Buffered(1))


def _wspec(a, layer, cols=None, col_block=0):
    shape = a.shape[1:] if cols is None else a.shape[1:-1] + (cols,)
    idx = (layer,) + (0,) * (len(shape) - 1) + (col_block,)
    return pl.BlockSpec((None,) + shape, lambda *_: idx, pipeline_mode=pl.Buffered(1))


def _rms(x, g):
    ms = jnp.mean(x * x, axis=-1, keepdims=True)
    return x * lax.rsqrt(ms + RMS_EPS) * g


def _dot(a, b):
    return jnp.dot(a, b, preferred_element_type=F32)


def _rope_head_tables(n_tokens, d):
    dim_axis = d // 2
    t = np.arange(n_tokens)
    row = (t // GRID_W).astype(np.float64)
    col = (t % GRID_W).astype(np.float64)
    freqs = ROPE_THETA ** (-np.arange(0, dim_axis, 2, dtype=np.float64) / dim_axis)
    ar = row[:, None] * freqs[None, :]
    ac = col[:, None] * freqs[None, :]
    cos = np.concatenate([np.cos(ar), np.cos(ar), np.cos(ac), np.cos(ac)], axis=1)
    sin = np.concatenate([-np.sin(ar), np.sin(ar), -np.sin(ac), np.sin(ac)], axis=1)
    return cos, sin


def _rope_tables(n_tokens):
    cg, sg = _rope_head_tables(n_tokens, GQA_HEAD_DIM)
    cg = np.concatenate([cg, cg], axis=1)
    sg = np.concatenate([sg, sg], axis=1)
    cm32, sm32 = _rope_head_tables(n_tokens, MLA_ROPE)
    pad = LANE - MLA_NOPE - MLA_ROPE
    cm = np.concatenate([np.ones((n_tokens, MLA_NOPE)), cm32, np.zeros((n_tokens, pad))], axis=1)
    sm = np.concatenate([np.zeros((n_tokens, MLA_NOPE)), sm32, np.zeros((n_tokens, pad))], axis=1)
    return tuple(jnp.asarray(a, F32) for a in (cg, sg, cm, sm))


def _ada_body(c_ref, w_ref, b_ref, o_ref):
    c = c_ref[...]
    s = (c * jax.nn.sigmoid(c)).astype(BF16)
    o_ref[0] = _dot(s, w_ref[0].astype(BF16)) + b_ref[0]


def _ada_call(cin, w_ada, b_ada):
    depth, d, n = w_ada.shape
    cols = 2048
    rows = cin.shape[0]
    return pl.pallas_call(
        _ada_body,
        grid=(depth, n // cols),
        in_specs=[
            pl.BlockSpec((rows, d), lambda l, j: (0, 0)),
            pl.BlockSpec((1, d, cols), lambda l, j: (l, 0, j)),
            pl.BlockSpec((1, 1, cols), lambda l, j: (l, 0, j)),
        ],
        out_specs=pl.BlockSpec((1, rows, cols), lambda l, j: (l, 0, j)),
        out_shape=jax.ShapeDtypeStruct((depth, rows, n), F32),
        compiler_params=pltpu.CompilerParams(
            dimension_semantics=("arbitrary", "arbitrary"), vmem_limit_bytes=VMEM_LIMIT),
        name="ada_mod",
    )(cin, w_ada, b_ada.reshape(depth, 1, n))


def _two_head_rsqrt(xb, lo):
    x2 = xb * xb
    s_lo = jnp.sum(jnp.where(lo, x2, 0.0), axis=-1, keepdims=True)
    s_hi = jnp.sum(jnp.where(lo, 0.0, x2), axis=-1, keepdims=True)
    ms = jnp.where(lo, s_lo, s_hi) * (1.0 / GQA_HEAD_DIM)
    return lax.rsqrt(ms + RMS_EPS)


def _rot_partner_lanes(xb, lane, quarter):
    first = jnp.bitwise_and(lane, quarter) == 0
    return jnp.where(first, pltpu.roll(xb, LANE - quarter, axis=1), pltpu.roll(xb, quarter, axis=1))


def _store_value_heads(v_ref, first, blk, lo):
    v_ref[first] = jnp.where(lo, blk, 1.0).astype(BF16)
    v_ref[first + 1] = jnp.where(lo, 1.0, blk).astype(BF16)


def _inproj_body(latent, names, *refs):
    r = dict(zip(names, refs))
    x = r["x"][...]
    tm = x.shape[0]
    mod = r["mod"][0]
    sh1 = mod[:, 0:D_MODEL]
    sc1 = mod[:, D_MODEL:2 * D_MODEL]
    h = (_rms(x, r["gmix"][...]) * (1.0 + sc1) + sh1).astype(BF16)
    z = _dot(h, r["w1"][...])

    lane = lax.broadcasted_iota(jnp.int32, (tm, LANE), 1)
    lo = lane < GQA_HEAD_DIM

    if latent:
        cg = r["cg"][...]
        sg = r["sg"][...]
        cm = r["cm"][...]
        sm = r["sm"][...]

    qn = _rms(z[:, _W1_QA:_W1_QA + Q_LORA], r["gqa"][...]).astype(BF16)
    qm = _dot(qn, r["wqb"][...])
    if latent:
        qms = _dot(qn, r["wqbsw"][...])
    for hh in range(MLA_HEADS):
        blk = qm[:, hh * LANE:(hh + 1) * LANE]
        if latent:
            blk = blk * cm + qms[:, hh * LANE:(hh + 1) * LANE] * sm
        r["q_mla"][hh] = blk.astype(BF16)

    ckv = _rms(z[:, _W1_CKV:_W1_CKV + KV_LORA], r["gkva"][...])
    kr = z[:, _W1_KR:_W1_KR + LANE]
    if latent:
        kr = kr * cm + _rot_partner_lanes(kr, lane, MLA_ROPE // 4) * sm
    else:
        r["ckv_out"][...] = ckv
        r["kr_out"][...] = kr[:, MLA_NOPE:MLA_NOPE + MLA_ROPE]
    ckv_b = ckv.astype(BF16)
    kfull = _dot(jnp.concatenate([ckv_b, kr.astype(BF16)], axis=1), r["wkp"][...])
    for hh in range(MLA_HEADS):
        r["k_mla"][hh] = kfull[:, hh * LANE:(hh + 1) * LANE].astype(BF16)
    vfull = _dot(ckv_b, r["wv"][...])
    for pp in range(N_PAIRS):
        _store_value_heads(r["v_mla"], 2 * pp, vfull[:, pp * LANE:(pp + 1) * LANE], lo)

    gq2 = r["gq2"][...]
    q_scale = GQA_HEAD_DIM ** -0.5
    for j in range(N_PAIRS):
        xb = z[:, _W1_GQ + j * LANE:_W1_GQ + (j + 1) * LANE]
        rs = _two_head_rsqrt(xb, lo)
        y = xb * rs * gq2
        if latent:
            y = y * cg + _rot_partner_lanes(y, lane, GQA_HEAD_DIM // 4) * sg
        y = y * q_scale
        r["q_gqa"][2 * j] = jnp.where(lo, y, 0.0).astype(BF16)
        r["q_gqa"][2 * j + 1] = jnp.where(lo, 0.0, y).astype(BF16)
    kb = z[:, _W1_GK:_W1_GK + LANE]
    rs = _two_head_rsqrt(kb, lo)
    kg = kb * rs * r["gk2"][...]
    if latent:
        kg = kg * cg + _rot_partner_lanes(kg, lane, GQA_HEAD_DIM // 4) * sg
    vg = z[:, _W1_GV:_W1_GV + LANE]
    if not latent:
        r["kg_out"][...] = kg
        r["vg_out"][...] = vg
    r["k_gqa"][...] = kg.astype(BF16)
    _store_value_heads(r["v_gqa"], 0, vg, lo)

    r["pool"][...] = z[:, _W1_POOL:_W1_POOL + POOL_WIDTH]


def _inproj_call(x, mod4, layer, mod_base, seq_len, tm, latent, wl, tabs):
    n = x.shape[0]
    nt = n // tm
    per_seq = seq_len // tm if latent else 1

    def tile(i):
        return (i, 0)

    def mod_idx(i):
        if latent:
            return (layer, mod_base + i // per_seq, 0, 0)
        return (layer, mod_base, 0, 0)

    names = ["x", "mod", "w1", "gmix", "gqa", "wqb", "gkva", "wkp", "wv", "gq2", "gk2"]
    args = [x, mod4, wl["w1"]] + [wl[nm] for nm in names[3:]]
    specs = [pl.BlockSpec((tm, D_MODEL), tile), pl.BlockSpec((None, 1, 1, 6 * D_MODEL), mod_idx),
             _wspec(wl["w1"], layer)]
    specs += [_wspec(a, layer) for a in args[3:]]
    if latent:
        names.append("wqbsw")
        args.append(wl["wqbsw"])
        specs.append(_wspec(wl["wqbsw"], layer))
        for nm, a in zip(("cg", "sg", "cm", "sm"), tabs):
            names.append(nm)
            args.append(a)
            specs.append(pl.BlockSpec((tm, LANE), lambda i: (i % per_seq, 0)))

    head_spec = pl.BlockSpec((MLA_HEADS, tm, LANE), lambda i: (0, i, 0))
    kvh_spec = pl.BlockSpec((GQA_KV_HEADS, tm, LANE), lambda i: (0, i, 0))
    out_names = ["q_mla", "k_mla", "v_mla", "q_gqa", "k_gqa", "v_gqa", "pool"]
    out_shapes = [
        jax.ShapeDtypeStruct((MLA_HEADS, n, LANE), BF16),
        jax.ShapeDtypeStruct((MLA_HEADS, n, LANE), BF16),
        jax.ShapeDtypeStruct((MLA_HEADS, n, LANE), BF16),
        jax.ShapeDtypeStruct((GQA_HEADS, n, LANE), BF16),
        jax.ShapeDtypeStruct((n, LANE), BF16),
        jax.ShapeDtypeStruct((GQA_KV_HEADS, n, LANE), BF16),
        jax.ShapeDtypeStruct((n, POOL_WIDTH), F32),
    ]
    out_specs = [head_spec, head_spec, head_spec, head_spec,
                 pl.BlockSpec((tm, LANE), tile), kvh_spec,
                 pl.BlockSpec((tm, POOL_WIDTH), tile)]
    if not latent:
        out_names += ["ckv_out", "kr_out", "kg_out", "vg_out"]
        out_shapes += [jax.ShapeDtypeStruct((n, KV_LORA), F32), jax.ShapeDtypeStruct((n, MLA_ROPE), F32),
                       jax.ShapeDtypeStruct((n, LANE), F32), jax.ShapeDtypeStruct((n, LANE), F32)]
        out_specs += [pl.BlockSpec((tm, KV_LORA), tile), pl.BlockSpec((tm, MLA_ROPE), tile),
                      pl.BlockSpec((tm, LANE), tile), pl.BlockSpec((tm, LANE), tile)]

    outs = pl.pallas_call(
        functools.partial(_inproj_body, latent, names + out_names),
        grid=(nt,),
        in_specs=specs,
        out_specs=out_specs,
        out_shape=out_shapes,
        compiler_params=pltpu.CompilerParams(
            dimension_semantics=("arbitrary",), vmem_limit_bytes=VMEM_LIMIT),
        name="inproj_lat" if latent else "inproj_ctx",
    )(*args)
    return dict(zip(out_names, outs))


def _cachekv_body(ckv_ref, kr_ref, kg_ref, vg_ref, wkp_ref, wv_ref, k_ref, v_ref, kgo_ref, vgo_ref):
    rows = ckv_ref.shape[0] * ckv_ref.shape[1]
    ckv_b = ckv_ref[...].reshape(rows, LANE).astype(BF16)
    lo = lax.broadcasted_iota(jnp.int32, (rows, LANE), 1) < MLA_V
    kr_b = kr_ref[...].reshape(rows, LANE).astype(BF16)
    kfull = _dot(jnp.concatenate([ckv_b, kr_b], axis=1), wkp_ref[...])
    kgo_ref[0] = kg_ref[...].reshape(rows, LANE).astype(BF16)
    for hh in range(MLA_HEADS):
        k_ref[hh] = kfull[:, hh * LANE:(hh + 1) * LANE].astype(BF16)
    vfull = _dot(ckv_b, wv_ref[...])
    for pp in range(N_PAIRS):
        _store_value_heads(v_ref, 2 * pp, vfull[:, pp * LANE:(pp + 1) * LANE], lo)
    _store_value_heads(vgo_ref, 0, vg_ref[...].reshape(rows, LANE), lo)


def _cachekv_call(caches, layer, wl):
    bsz, _, past, _ = caches[0].shape
    n = bsz * past
    cache_spec = pl.BlockSpec((bsz, None, past, LANE), lambda i: (0, layer, 0, 0))
    return pl.pallas_call(
        _cachekv_body,
        grid=(1,),
        in_specs=[cache_spec] * 4 + [_wspec(wl["wkp"], layer), _wspec(wl["wv"], layer)],
        out_specs=[_const_spec((MLA_HEADS, n, LANE)), _const_spec((MLA_HEADS, n, LANE)),
                   _const_spec((1, n, LANE)), _const_spec((GQA_KV_HEADS, n, LANE))],
        out_shape=[jax.ShapeDtypeStruct((MLA_HEADS, n, LANE), BF16),
                   jax.ShapeDtypeStruct((MLA_HEADS, n, LANE), BF16),
                   jax.ShapeDtypeStruct((1, n, LANE), BF16),
                   jax.ShapeDtypeStruct((GQA_KV_HEADS, n, LANE), BF16)],
        compiler_params=pltpu.CompilerParams(
            dimension_semantics=("arbitrary",), vmem_limit_bytes=VMEM_LIMIT),
        name="cache_kv",
    )(*caches, wl["wkp"], wl["wv"])


def _normalise_pair(acc_a, acc_b, lo):
    num = jnp.where(lo, acc_a, acc_b)
    den = pltpu.roll(jnp.where(lo, acc_b, acc_a), MLA_V, axis=1)
    return num / den


def _attn_body(group, seq_len, shared_kv, exp_scale, q_ref, k_ref, v_ref, o_ref):
    lo = lax.broadcasted_iota(jnp.int32, (CTX_Q_ROWS, LANE), 1) < MLA_V
    nt_dims = (((1,), (1,)), ((), ()))
    units = [(pr, qb, e) for pr in range(N_PAIRS) for qb in range(seq_len // CTX_Q_ROWS) for e in range(2)]

    def one_batch_row(g, carry):
        base = pl.multiple_of(g * seq_len, seq_len)
        rows = pl.ds(base, seq_len)

        scores, row_max, accs = {}, {}, {}
        for i in range(len(units) + 2):
            if i < len(units):
                pr, qb, e = units[i]
                qrows = pl.ds(base + qb * CTX_Q_ROWS, CTX_Q_ROWS)
                q = q_ref[2 * pr + e, qrows, :]
                k = k_ref[0 if shared_kv else 2 * pr + e, rows, :]
                scores[i] = lax.dot_general(q, k, nt_dims, preferred_element_type=F32)
            if 0 <= i - 1 < len(units):
                row_max[i - 1] = jnp.max(scores[i - 1], axis=-1, keepdims=True)
            if 0 <= i - 2 < len(units):
                u = i - 2
                pr, qb, e = units[u]
                qrows = pl.ds(base + qb * CTX_Q_ROWS, CTX_Q_ROWS)
                v = v_ref[e if shared_kv else 2 * pr + e, rows, :]
                p = jnp.exp2((scores.pop(u) - row_max.pop(u)) * exp_scale)
                accs[u] = _dot(p.astype(BF16), v)
                if e == 1:
                    o_ref[pr, qrows, :] = _normalise_pair(accs.pop(u - 1), accs.pop(u), lo).astype(BF16)
        return carry

    lax.fori_loop(0, group, one_batch_row, 0)


def _attn_call(q, k, v, batch, seq_len, group, scale, shared_kv, name):
    n = q.shape[1]
    rows = group * seq_len
    exp_scale = scale * math.log2(math.e)

    def spec(heads):
        return pl.BlockSpec((heads, rows, LANE), lambda b: (0, b, 0))

    return pl.pallas_call(
        functools.partial(_attn_body, group, seq_len, shared_kv, exp_scale),
        grid=(batch // group,),
        in_specs=[spec(q.shape[0]), spec(k.shape[0]), spec(v.shape[0])],
        out_specs=pl.BlockSpec((N_PAIRS, rows, LANE), lambda b: (0, b, 0)),
        out_shape=jax.ShapeDtypeStruct((N_PAIRS, n, LANE), BF16),
        compiler_params=pltpu.CompilerParams(
            dimension_semantics=("arbitrary",), vmem_limit_bytes=VMEM_LIMIT),
        name=name,
    )(q, k, v)


def _attn_pipe_body(shared_kv, exp_scale, tq, pairs, q_ref, k_ref, v_ref, kc_ref, vc_ref, o_ref,
                    s0, s1, mr0, mr1, ac0, ac1):
    seq = k_ref.shape[1]
    past = kc_ref.shape[1]
    nq = seq // tq
    n_units = pairs * nq
    n_new = seq // KEY_TILE
    n_tiles = n_new + past // KEY_TILE
    s_scr = (s0, s1)
    mrun_scr = (mr0, mr1)
    acc_scr = (ac0, ac1)
    nt_dims = (((1,), (1,)), ((), ()))
    lo = lax.broadcasted_iota(jnp.int32, (tq, LANE), 1) < MLA_V

    def pair_rows(t):
        j = t % nq
        return t // nq, pl.ds(pl.multiple_of(j * tq, tq), tq)

    def kv_tile(new_ref, cache_ref, head, blk):
        if blk < n_new:
            return new_ref[head, blk * KEY_TILE:(blk + 1) * KEY_TILE, :]
        blk -= n_new
        return cache_ref[head, blk * KEY_TILE:(blk + 1) * KEY_TILE, :]

    def finish_prev(e_prev, t_prev):
        if e_prev == 1:
            pair, rows = pair_rows(t_prev)
            o_ref[pair, rows, :] = _normalise_pair(ac0[...], ac1[...], lo).astype(BF16)

    def region(t_scores, e_scores, t_cur, e_cur, t_prev):
        if t_prev is not None:
            finish_prev(1 - e_cur, t_prev)
        if e_scores is not None:
            pair, rows = pair_rows(t_scores)
            q = q_ref[2 * pair + e_scores, rows, :]
            k_head = 0 if shared_kv else 2 * pair + e_scores
        if e_cur is not None:
            v_head = e_cur if shared_kv else 2 * (t_cur // nq) + e_cur
            m = jnp.max(mrun_scr[e_cur][...], axis=-1, keepdims=True)
            m_b = jnp.broadcast_to(m, (tq, LANE))
        acc = None
        for blk in range(n_tiles):
            cols = slice(blk * KEY_TILE, (blk + 1) * KEY_TILE)
            if e_scores is not None:
                s_t = lax.dot_general(q, kv_tile(k_ref, kc_ref, k_head, blk), nt_dims,
                                      preferred_element_type=F32)
                s_scr[e_scores][:, cols] = s_t
                m_t = jnp.maximum(s_t[:, 0:LANE], s_t[:, LANE:2 * LANE])
                if blk > 0:
                    m_t = jnp.maximum(m_t, mrun_scr[e_scores][...])
                mrun_scr[e_scores][...] = m_t
            if e_cur is not None:
                p_parts = []
                for hh in range(KEY_TILE // LANE):
                    c0 = blk * KEY_TILE + hh * LANE
                    s_h = s_scr[e_cur][:, c0:c0 + LANE]
                    p_parts.append(jnp.exp2((s_h - m_b) * exp_scale).astype(BF16))
                p_t = jnp.concatenate(p_parts, axis=1)
                part = _dot(p_t, kv_tile(v_ref, vc_ref, v_head, blk))
                acc = part if acc is None else acc + part
        if e_cur is not None:
            acc_scr[e_cur][...] = acc

    region(0, 0, None, None, None)
    region(0, 1, 0, 0, None)
    region(1, 0, 0, 1, 0)

    def body(t, carry):
        region(t, 1, t, 0, t - 1)
        region(t + 1, 0, t, 1, t)
        return carry

    lax.fori_loop(1, n_units - 1, body, 0)
    region(n_units - 1, 1, n_units - 1, 0, n_units - 2)
    region(None, None, n_units - 1, 1, n_units - 1)
    finish_prev(1, n_units - 1)


def _attn_pipe_call(q, k, v, kc, vc, batch, seq_len, tq, scale, shared_kv, name, pairs=LAT_PAIRS):
    n = q.shape[1]
    past = kc.shape[1] // batch
    exp_scale = scale * math.log2(math.e)

    def spec(a, rows):
        heads = a.shape[0]
        if heads < 2 * N_PAIRS:
            return pl.BlockSpec((heads, rows, LANE), lambda b, p: (0, b, 0))
        return pl.BlockSpec((2 * pairs, rows, LANE), lambda b, p: (p, b, 0))

    total = seq_len + past
    return pl.pallas_call(
        functools.partial(_attn_pipe_body, shared_kv, exp_scale, tq, pairs),
        grid=(batch, N_PAIRS // pairs),
        in_specs=[spec(q, seq_len), spec(k, seq_len), spec(v, seq_len), spec(kc, past), spec(vc, past)],
        out_specs=pl.BlockSpec((pairs, seq_len, LANE), lambda b, p: (p, b, 0)),
        out_shape=jax.ShapeDtypeStruct((N_PAIRS, n, LANE), BF16),
        scratch_shapes=[pltpu.VMEM((tq, total), F32), pltpu.VMEM((tq, total), F32)]
        + [pltpu.VMEM((tq, LANE), F32) for _ in range(4)],
        compiler_params=pltpu.CompilerParams(
            dimension_semantics=("arbitrary", "arbitrary"), vmem_limit_bytes=VMEM_LIMIT),
        name=name,
    )(q, k, v, kc, vc)


def _seq_pos(tm, seq_len):
    i = pl.program_id(0)
    row = lax.broadcasted_iota(jnp.int32, (tm, 1), 0) + i * tm
    return jnp.bitwise_and(row, seq_len - 1)


def _merge_body(seq_len, x_ref, mod_ref, gmix_ref, wg_ref, a_ref, b_ref, pc_ref, pl_ref, pr_ref,
                wpool_ref, pscale_ref, wa_ref, wb_ref, wc_ref, wo_ref, o_ref, merged_scr, ab_scr):
    x = x_ref[...]
    tm = x.shape[0]
    mod = mod_ref[0]
    sh1 = mod[:, 0:D_MODEL]
    sc1 = mod[:, D_MODEL:2 * D_MODEL]
    g1 = mod[:, 2 * D_MODEL:3 * D_MODEL]
    h = (_rms(x, gmix_ref[...]) * (1.0 + sc1) + sh1).astype(BF16)

    pos = _seq_pos(tm, seq_len)
    pscale = pscale_ref[...]
    centre_rows = pc_ref[...]
    if seq_len >= tm:
        start = (pl.program_id(0) * tm) % seq_len
        left = jnp.where(start == 0, 0.0, pl_ref[...])
        right = jnp.where(start + tm == seq_len, 0.0, pr_ref[...])
        blocks = [jnp.concatenate([left, centre_rows, right], axis=0)]
    else:
        zeros = jnp.zeros((HALO, POOL_WIDTH), F32)
        blocks = [jnp.concatenate([zeros, centre_rows[s * seq_len:(s + 1) * seq_len], zeros], axis=0)
                  for s in range(tm // seq_len)]

    def pool_group(g):
        w = POOL_WINDOWS[g]
        gcols = slice(g * POOL_GROUP, (g + 1) * POOL_GROUP)
        sums = []
        for blk in blocks:
            rows = blk.shape[0] - 2 * HALO
            eg = blk[:, gcols]
            wsum = eg[HALO:HALO + rows]
            for dlt in range(-(w // 2), w - w // 2):
                if dlt != 0:
                    wsum = wsum + eg[HALO + dlt:HALO + dlt + rows]
            sums.append(wsum)
        wsum = sums[0] if len(sums) == 1 else jnp.concatenate(sums, axis=0)
        cnt = jnp.minimum(pos + (w - w // 2), seq_len) - jnp.maximum(pos - w // 2, 0)
        return (wsum / cnt.astype(F32) - centre_rows[:, gcols]).astype(BF16)

    def pool_pair_map(p, pooled_a, pooled_b):
        both = jnp.concatenate([pooled_a, pooled_b], axis=1)
        return _dot(both, wpool_ref[p]) * pscale[:, 2 * p * POOL_GROUP:2 * (p + 1) * POOL_GROUP]

    def gated(branch, src, w_ref, j):
        c0 = branch * D_MODEL + j * FF_CHUNK
        gate = jax.nn.sigmoid(_dot(h, wg_ref[:, c0:c0 + FF_CHUNK]))
        return gate * _dot(src, w_ref[:, j * FF_CHUNK:(j + 1) * FF_CHUNK])

    a_out = jnp.concatenate([a_ref[p] for p in range(N_PAIRS)], axis=1)
    b_out = jnp.concatenate([b_ref[p] for p in range(N_PAIRS)], axis=1)
    n_chunks = D_MODEL // FF_CHUNK
    pooled = []
    c_parts = []
    for j in range(n_chunks):
        cols = slice(j * FF_CHUNK, (j + 1) * FF_CHUNK)
        for g in range(j * len(POOL_WINDOWS) // n_chunks, (j + 1) * len(POOL_WINDOWS) // n_chunks):
            pooled.append(pool_group(g))
            if g % 2 == 1:
                c_parts.append(pool_pair_map(g // 2, pooled[g - 1], pooled[g]))
        ab_scr[:, cols] = gated(0, a_out, wa_ref, j) + gated(1, b_out, wb_ref, j)
    c_out = jnp.concatenate(c_parts, axis=-1).astype(BF16)
    for j in range(n_chunks):
        cols = slice(j * FF_CHUNK, (j + 1) * FF_CHUNK)
        merged_scr[:, cols] = (ab_scr[:, cols] + gated(2, c_out, wc_ref, j)).astype(BF16)
    o_ref[...] = x + g1 * _dot(merged_scr[...], wo_ref[...])


def _halo_specs(tm, width, n):
    blocks = tm // HALO
    last = n // HALO - 1
    left = pl.BlockSpec((HALO, width), lambda i: (jnp.maximum(i * blocks - 1, 0), 0))
    right = pl.BlockSpec((HALO, width), lambda i: (jnp.minimum((i + 1) * blocks, last), 0))
    return left, right


def _mod_spec(layer, mod_base, per_seq, latent):
    if latent:
        return pl.BlockSpec((None, 1, 1, 6 * D_MODEL), lambda i: (layer, mod_base + i // per_seq, 0, 0))
    return pl.BlockSpec((None, 1, 1, 6 * D_MODEL), lambda i: (layer, mod_base, 0, 0))


def _merge_call(x, mod4, layer, mod_base, seq_len, tm, latent, a_out, b_out, pool_in, wl):
    n = x.shape[0]
    per_seq = seq_len // tm if latent else 1

    def tile(i):
        return (i, 0)

    left, right = _halo_specs(tm, POOL_WIDTH, n)
    consts = [wl["wpool"], wl["pscale"], wl["wa"], wl["wb"], wl["wc"], wl["wo"]]
    in_specs = [pl.BlockSpec((tm, D_MODEL), tile), _mod_spec(layer, mod_base, per_seq, latent),
                _wspec(wl["gmix"], layer), _wspec(wl["wg"], layer),
                pl.BlockSpec((N_PAIRS, tm, LANE), lambda i: (0, i, 0)),
                pl.BlockSpec((N_PAIRS, tm, LANE), lambda i: (0, i, 0)),
                pl.BlockSpec((tm, POOL_WIDTH), tile), left, right]
    in_specs += [_wspec(a, layer) for a in consts]
    return pl.pallas_call(
        functools.partial(_merge_body, seq_len),
        grid=(n // tm,),
        in_specs=in_specs,
        out_specs=pl.BlockSpec((tm, D_MODEL), tile),
        out_shape=jax.ShapeDtypeStruct((n, D_MODEL), F32),
        scratch_shapes=[pltpu.VMEM((tm, D_MODEL), BF16), pltpu.VMEM((tm, D_MODEL), F32)],
        compiler_params=pltpu.CompilerParams(
            dimension_semantics=("arbitrary",), vmem_limit_bytes=VMEM_LIMIT),
        name="merge_lat" if latent else "merge_ctx",
    )(x, mod4, wl["gmix"], wl["wg"], a_out, b_out, pool_in, pool_in, pool_in, *consts)


def _ffn_body(seq_len, final, x_ref, xl_ref, xr_ref, mod_ref, gffn_ref, wug_ref, wuv_ref, cw_ref, cb_ref,
              wd_ref, gfin_ref, o_ref, act_scr):
    x = x_ref[...]
    tm = x.shape[0]
    mod = mod_ref[0]
    sh2 = mod[:, 3 * D_MODEL:4 * D_MODEL]
    sc2 = mod[:, 4 * D_MODEL:5 * D_MODEL]
    g2 = mod[:, 5 * D_MODEL:6 * D_MODEL]
    xe = jnp.concatenate([xl_ref[...], x, xr_ref[...]], axis=0)
    h2e = _rms(xe, gffn_ref[...]) * (1.0 + sc2) + sh2
    h2 = h2e[HALO:HALO + tm].astype(BF16)
    h2e = h2e.astype(BF16)

    pos = _seq_pos(tm, seq_len)
    has_prev = pos >= 1
    has_next = pos <= seq_len - 2
    cw = cw_ref[...]
    cb = cb_ref[...]
    for j in range(D_FF // FF_CHUNK):
        cols = slice(j * FF_CHUNK, (j + 1) * FF_CHUNK)
        ge = _dot(h2e, wug_ref[:, cols])
        g = (jnp.where(has_prev, ge[HALO - 1:HALO - 1 + tm], 0.0) * cw[0:1, cols]
             + ge[HALO:HALO + tm] * cw[1:2, cols]
             + jnp.where(has_next, ge[HALO + 1:HALO + 1 + tm], 0.0) * cw[2:3, cols]
             + cb[:, cols])
        val = _dot(h2, wuv_ref[:, cols])
        act_scr[:, cols] = (g * jax.nn.sigmoid(g) * val).astype(BF16)
    y = x + g2 * _dot(act_scr[...], wd_ref[...])
    if final:
        y = _rms(y, gfin_ref[...])
    o_ref[...] = y


def _ffn_call(x, mod4, layer, mod_base, seq_len, tm, latent, final, wl, g_final):
    n = x.shape[0]
    per_seq = seq_len // tm if latent else 1

    def tile(i):
        return (i, 0)

    left, right = _halo_specs(tm, D_MODEL, n)
    consts = [wl["gffn"], wl["wu"], wl["wu"], wl["cw"], wl["cb"], wl["wd"], g_final]
    in_specs = [pl.BlockSpec((tm, D_MODEL), tile), left, right, _mod_spec(layer, mod_base, per_seq, latent),
                _wspec(wl["gffn"], layer), _wspec(wl["wu"], layer, cols=D_FF, col_block=0),
                _wspec(wl["wu"], layer, cols=D_FF, col_block=1), _wspec(wl["cw"], layer),
                _wspec(wl["cb"], layer), _wspec(wl["wd"], layer), _const_spec(g_final.shape)]
    return pl.pallas_call(
        functools.partial(_ffn_body, seq_len, final),
        grid=(n // tm,),
        in_specs=in_specs,
        out_specs=pl.BlockSpec((tm, D_MODEL), tile),
        out_shape=jax.ShapeDtypeStruct((n, D_MODEL), F32),
        scratch_shapes=[pltpu.VMEM((tm, D_FF), BF16)],
        compiler_params=pltpu.CompilerParams(
            dimension_semantics=("arbitrary",), vmem_limit_bytes=VMEM_LIMIT),
        name="ffn_lat" if latent else "ffn_ctx",
    )(x, x, x, mod4, *consts)


def _rot_partner(a):
    d = a.shape[-1]
    return jnp.flip(a.reshape(a.shape[:-1] + (2, 2, d // 4)), axis=-2).reshape(a.shape)


def _prep_weights(g_norm_mix, w_in, g_q_a, w_q_b, g_kv_a, w_kv_b, g_q_gqa, g_k_gqa, w_pool, pool_scale,
                  w_br_a, w_br_b, w_br_c, w_out, g_norm_ffn, w_up, conv_w, conv_b, w_down):
    depth = w_in.shape[0]
    hd = GQA_HEAD_DIM
    grp = GQA_HEADS // GQA_KV_HEADS

    def lane_pad(a, before, width=LANE):
        cfg = [(0, 0)] * (a.ndim - 1) + [(before, width - before - a.shape[-1])]
        return jnp.pad(a, cfg)

    def pair_heads(a):
        lead = a.shape[:-1]
        return a.reshape(lead + (GQA_KV_HEADS, grp, hd)).swapaxes(-3, -2).reshape(lead + (GQA_HEADS * hd,))

    w1 = jnp.concatenate(
        [w_in[:, :, _OFF_QA:_OFF_KR], pair_heads(w_in[:, :, _OFF_GQ:_OFF_GK]), w_in[:, :, _OFF_GK:_OFF_GATE],
         lane_pad(w_in[:, :, _OFF_KR:_OFF_GQ], MLA_NOPE)], axis=-1).astype(BF16)
    assert w1.shape[-1] == _W1_COLS

    qb = w_q_b.reshape(depth, Q_LORA, MLA_HEADS, MLA_NOPE + MLA_ROPE)
    wqb = lane_pad(qb, 0).reshape(depth, Q_LORA, MLA_HEADS * LANE).astype(BF16)
    wqbsw = lane_pad(_rot_partner(qb[..., MLA_NOPE:]), MLA_NOPE).reshape(
        depth, Q_LORA, MLA_HEADS * LANE).astype(BF16)

    kvb = w_kv_b.reshape(depth, KV_LORA, MLA_HEADS, MLA_NOPE + MLA_V)
    wk = lane_pad(kvb[..., :MLA_NOPE], 0).reshape(depth, KV_LORA, MLA_HEADS * LANE).astype(BF16)
    wv = kvb[..., MLA_NOPE:].reshape(depth, KV_LORA, MLA_HEADS * MLA_V).astype(BF16)

    pk = np.zeros((LANE, MLA_HEADS * LANE), np.float32)
    for hh in range(MLA_HEADS):
        for t in range(MLA_ROPE):
            pk[MLA_NOPE + t, hh * LANE + MLA_NOPE + t] = 1.0
    wkp = jnp.concatenate([wk, jnp.broadcast_to(jnp.asarray(pk, BF16), (depth,) + pk.shape)], axis=1)

    wp = w_pool.reshape(depth, len(POOL_WINDOWS) // 2, 2, POOL_GROUP, POOL_GROUP)
    zero = jnp.zeros_like(wp[:, :, 0])
    wpool2 = jnp.concatenate([jnp.concatenate([wp[:, :, 0], zero], axis=-1),
                              jnp.concatenate([zero, wp[:, :, 1]], axis=-1)], axis=-2).astype(BF16)

    row = lambda a: a[:, None, :]
    two = lambda a: jnp.concatenate([a, a], axis=-1)
    wb = w_br_b.reshape(depth, GQA_KV_HEADS, grp, hd, D_MODEL).swapaxes(1, 2).reshape(depth, GQA_HEADS * hd, D_MODEL)
    return dict(
        gmix=row(g_norm_mix), w1=w1, gqa=row(g_q_a), wqb=wqb, wqbsw=wqbsw, gkva=row(g_kv_a), wkp=wkp,
        wv=wv, gq2=row(two(g_q_gqa)), gk2=row(two(g_k_gqa)),
        wg=w_in[:, :, _OFF_GATE:].astype(BF16), wpool=wpool2, pscale=row(pool_scale),
        wa=w_br_a.astype(BF16), wb=wb.astype(BF16), wc=w_br_c.astype(BF16), wo=w_out.astype(BF16),
        gffn=row(g_norm_ffn), wu=w_up.astype(BF16), cw=conv_w, cb=row(conv_b), wd=w_down.astype(BF16),
    )


def _layer(x, mod4, layer, mod_base, batch, seq_len, latent, final, wl, g_final, tabs, caches):
    tm = TOKEN_TILE
    tq = Q_TILE
    assert tm % seq_len == 0 or seq_len % tm == 0
    pj = _inproj_call(x, mod4, layer, mod_base, seq_len, tm, latent, wl, tabs)
    if caches is not None:
        kc_m, vc_m, kc_g, vc_g = _cachekv_call(caches, layer, wl)
    mla_scale = (MLA_NOPE + MLA_ROPE) ** -0.5
    n = x.shape[0]
    k_g = pj["k_gqa"].reshape(1, n, LANE)
    if latent:
        a_out = _attn_pipe_call(pj["q_mla"], pj["k_mla"], pj["v_mla"], kc_m, vc_m, batch, seq_len, tq,
                                mla_scale, False, "attn_mla_lat")
        b_out = _attn_pipe_call(pj["q_gqa"], k_g, pj["v_gqa"], kc_g, vc_g, batch, seq_len, tq, 1.0, True,
                                "attn_gqa_lat")
    else:
        a_out = _attn_call(pj["q_mla"], pj["k_mla"], pj["v_mla"], batch, seq_len, CTX_GROUP,
                           mla_scale, False, "attn_mla_ctx")
        b_out = _attn_call(pj["q_gqa"], k_g, pj["v_gqa"], batch, seq_len, CTX_GROUP, 1.0, True,
                           "attn_gqa_ctx")
    x1 = _merge_call(x, mod4, layer, mod_base, seq_len, tm, latent, a_out, b_out, pj["pool"], wl)
    x2 = _ffn_call(x1, mod4, layer, mod_base, seq_len, tm, latent, final, wl, g_final)
    return x2, pj


def kernel(x_prompt, x_sample, c, cache_mla_ckv, cache_mla_krope, cache_gqa_k, cache_gqa_v, c_ctx, w_ada, b_ada, g_norm_mix, w_in, g_q_a, w_q_b, g_kv_a, w_kv_b, g_q_gqa, g_k_gqa, w_pool, pool_scale, w_br_a, w_br_b, w_br_c, w_out, g_norm_ffn, w_up, conv_w, conv_b, w_down, g_final):
    depth = w_in.shape[0]
    bc, tc, _ = x_prompt.shape
    bl, tl, _ = x_sample.shape
    past = cache_mla_ckv.shape[2]
    mod_rows = 8
    assert 1 + bl <= mod_rows and tc & (tc - 1) == 0 and tl & (tl - 1) == 0

    cin = jnp.concatenate([c_ctx[None, :], c, jnp.zeros((mod_rows - 1 - bl, D_MODEL), F32)], axis=0)
    mod4 = _ada_call(cin, w_ada, b_ada).reshape(depth, mod_rows, 1, 6 * D_MODEL)
    tabs = _rope_tables(tl)
    g_fin = g_final.reshape(1, D_MODEL)
    wl = _prep_weights(g_norm_mix, w_in, g_q_a, w_q_b, g_kv_a, w_kv_b, g_q_gqa, g_k_gqa, w_pool,
                       pool_scale, w_br_a, w_br_b, w_br_c, w_out, g_norm_ffn, w_up, conv_w, conv_b, w_down)
    kr_pad = jnp.pad(cache_mla_krope, ((0, 0), (0, 0), (0, 0), (MLA_NOPE, LANE - MLA_NOPE - MLA_ROPE)))
    caches = (cache_mla_ckv, kr_pad, cache_gqa_k.reshape(bl, depth, past, LANE),
              cache_gqa_v.reshape(bl, depth, past, LANE))

    xc = x_prompt.reshape(bc * tc, D_MODEL)
    xl = x_sample.reshape(bl * tl, D_MODEL)
    st_ckv, st_kr, st_k, st_v = [], [], [], []
    for l in range(depth):
        final = l == depth - 1
        xc, pj = _layer(xc, mod4, l, 0, bc, tc, False, final, wl, g_fin, None, None)
        st_ckv.append(pj["ckv_out"].reshape(bc, tc, KV_LORA))
        st_kr.append(pj["kr_out"].reshape(bc, tc, MLA_ROPE))
        st_k.append(pj["kg_out"].reshape(bc, tc, GQA_KV_HEADS, GQA_HEAD_DIM))
        st_v.append(pj["vg_out"].reshape(bc, tc, GQA_KV_HEADS, GQA_HEAD_DIM))
        xl, _ = _layer(xl, mod4, l, 1, bl, tl, True, final, wl, g_fin, tabs, caches)

    return (xc.reshape(bc, tc, D_MODEL), xl.reshape(bl, tl, D_MODEL),
            jnp.stack(st_ckv, axis=1), jnp.stack(st_kr, axis=1),
            jnp.stack(st_k, axis=1), jnp.stack(st_v, axis=1))
```

```python
import functools
import math

import numpy as np
import jax
import jax.numpy as jnp
from jax import lax
from jax.experimental import pallas as pl
from jax.experimental.pallas import tpu as pltpu

D_MODEL = 1024
GRID_W = 64
RMS_EPS = 1e-6
ROPE_THETA = 10000.0
MLA_HEADS = 8
MLA_NOPE = 64
MLA_ROPE = 32
MLA_V = 64
Q_LORA = 256
KV_LORA = 128
GQA_HEADS = 8
GQA_KV_HEADS = 2
GQA_HEAD_DIM = 64
POOL_WINDOWS = (2, 4, 8, 16)
POOL_GROUP = 128
POOL_WIDTH = POOL_GROUP * len(POOL_WINDOWS)
D_FF = 2816

LANE = 128
HALO = 8
FF_CHUNK = 256
KEY_TILE = 256
CTX_GROUP = 8
CTX_Q_ROWS = 256
TOKEN_TILE = 512
Q_TILE = 512
LAT_PAIRS = 2
N_PAIRS = MLA_HEADS // 2
VMEM_LIMIT = 56 * 1024 * 1024

BF16 = jnp.bfloat16
F32 = jnp.float32

_OFF_QA = 0
_OFF_CKV = _OFF_QA + Q_LORA
_OFF_KR = _OFF_CKV + KV_LORA
_OFF_GQ = _OFF_KR + MLA_ROPE
_OFF_GK = _OFF_GQ + GQA_HEADS * GQA_HEAD_DIM
_OFF_GV = _OFF_GK + GQA_KV_HEADS * GQA_HEAD_DIM
_OFF_POOL = _OFF_GV + GQA_KV_HEADS * GQA_HEAD_DIM
_OFF_GATE = _OFF_POOL + POOL_WIDTH

_W1_QA = 0
_W1_CKV = 256
_W1_GQ = 384
_W1_GK = 896
_W1_GV = 1024
_W1_POOL = 1152
_W1_KR = 1664
_W1_COLS = 1792


def _const_spec(shape):
    nd = len(shape)
    return pl.BlockSpec(shape, lambda *_: (0,) * nd, pipeline_mode=pl.Buffered(1))


def _wspec(a, layer, cols=None, col_block=0):
    shape = a.shape[1:] if cols is None else a.shape[1:-1] + (cols,)
    idx = (layer,) + (0,) * (len(shape) - 1) + (col_block,)
    return pl.BlockSpec((None,) + shape, lambda *_: idx, pipeline_mode=pl.Buffered(1))


def _rms(x, g):
    ms = jnp.mean(x * x, axis=-1, keepdims=True)
    return x * lax.rsqrt(ms + RMS_EPS) * g


def _dot(a, b):
    return jnp.dot(a, b, preferred_element_type=F32)


def _rope_head_tables(n_tokens, d):
    dim_axis = d // 2
    t = np.arange(n_tokens)
    row = (t // GRID_W).astype(np.float64)
    col = (t % GRID_W).astype(np.float64)
    freqs = ROPE_THETA ** (-np.arange(0, dim_axis, 2, dtype=np.float64) / dim_axis)
    ar = row[:, None] * freqs[None, :]
    ac = col[:, None] * freqs[None, :]
    cos = np.concatenate([np.cos(ar), np.cos(ar), np.cos(ac), np.cos(ac)], axis=1)
    sin = np.concatenate([-np.sin(ar), np.sin(ar), -np.sin(ac), np.sin(ac)], axis=1)
    return cos, sin


def _rope_tables(n_tokens):
    cg, sg = _rope_head_tables(n_tokens, GQA_HEAD_DIM)
    cg = np.concatenate([cg, cg], axis=1)
    sg = np.concatenate([sg, sg], axis=1)
    cm32, sm32 = _rope_head_tables(n_tokens, MLA_ROPE)
    pad = LANE - MLA_NOPE - MLA_ROPE
    cm = np.concatenate([np.ones((n_tokens, MLA_NOPE)), cm32, np.zeros((n_tokens, pad))], axis=1)
    sm = np.concatenate([np.zeros((n_tokens, MLA_NOPE)), sm32, np.zeros((n_tokens, pad))], axis=1)
    return tuple(jnp.asarray(a, F32) for a in (cg, sg, cm, sm))


def _ada_body(c_ref, w_ref, b_ref, o_ref):
    c = c_ref[...]
    s = (c * jax.nn.sigmoid(c)).astype(BF16)
    o_ref[0] = _dot(s, w_ref[0].astype(BF16)) + b_ref[0]


def _ada_call(cin, w_ada, b_ada):
    depth, d, n = w_ada.shape
    cols = 2048
    rows = cin.shape[0]
    return pl.pallas_call(
        _ada_body,
        grid=(depth, n // cols),
        in_specs=[
            pl.BlockSpec((rows, d), lambda l, j: (0, 0)),
            pl.BlockSpec((1, d, cols), lambda l, j: (l, 0, j)),
            pl.BlockSpec((1, 1, cols), lambda l, j: (l, 0, j)),
        ],
        out_specs=pl.BlockSpec((1, rows, cols), lambda l, j: (l, 0, j)),
        out_shape=jax.ShapeDtypeStruct((depth, rows, n), F32),
        compiler_params=pltpu.CompilerParams(
            dimension_semantics=("arbitrary", "arbitrary"), vmem_limit_bytes=VMEM_LIMIT),
        name="ada_mod",
    )(cin, w_ada, b_ada.reshape(depth, 1, n))


def _two_head_rsqrt(xb, lo):
    x2 = xb * xb
    s_lo = jnp.sum(jnp.where(lo, x2, 0.0), axis=-1, keepdims=True)
    s_hi = jnp.sum(jnp.where(lo, 0.0, x2), axis=-1, keepdims=True)
    ms = jnp.where(lo, s_lo, s_hi) * (1.0 / GQA_HEAD_DIM)
    return lax.rsqrt(ms + RMS_EPS)


def _rot_partner_lanes(xb, lane, quarter):
    first = jnp.bitwise_and(lane, quarter) == 0
    return jnp.where(first, pltpu.roll(xb, LANE - quarter, axis=1), pltpu.roll(xb, quarter, axis=1))


def _store_value_heads(v_ref, first, blk, lo):
    v_ref[first] = jnp.where(lo, blk, 1.0).astype(BF16)
    v_ref[first + 1] = jnp.where(lo, 1.0, blk).astype(BF16)


def _inproj_body(latent, names, *refs):
    r = dict(zip(names, refs))
    x = r["x"][...]
    tm = x.shape[0]
    mod = r["mod"][0]
    sh1 = mod[:, 0:D_MODEL]
    sc1 = mod[:, D_MODEL:2 * D_MODEL]
    h = (_rms(x, r["gmix"][...]) * (1.0 + sc1) + sh1).astype(BF16)
    z = _dot(h, r["w1"][...])

    lane = lax.broadcasted_iota(jnp.int32, (tm, LANE), 1)
    lo = lane < GQA_HEAD_DIM

    if latent:
        cg = r["cg"][...]
        sg = r["sg"][...]
        cm = r["cm"][...]
        sm = r["sm"][...]

    qn = _rms(z[:, _W1_QA:_W1_QA + Q_LORA], r["gqa"][...]).astype(BF16)
    qm = _dot(qn, r["wqb"][...])
    if latent:
        qms = _dot(qn, r["wqbsw"][...])
    for hh in range(MLA_HEADS):
        blk = qm[:, hh * LANE:(hh + 1) * LANE]
        if latent:
            blk = blk * cm + qms[:, hh * LANE:(hh + 1) * LANE] * sm
        r["q_mla"][hh] = blk.astype(BF16)

    ckv = _rms(z[:, _W1_CKV:_W1_CKV + KV_LORA], r["gkva"][...])
    kr = z[:, _W1_KR:_W1_KR + LANE]
    if latent:
        kr = kr * cm + _rot_partner_lanes(kr, lane, MLA_ROPE // 4) * sm
    else:
        r["ckv_out"][...] = ckv
        r["kr_out"][...] = kr[:, MLA_NOPE:MLA_NOPE + MLA_ROPE]
    ckv_b = ckv.astype(BF16)
    kfull = _dot(jnp.concatenate([ckv_b, kr.astype(BF16)], axis=1), r["wkp"][...])
    for hh in range(MLA_HEADS):
        r["k_mla"][hh] = kfull[:, hh * LANE:(hh + 1) * LANE].astype(BF16)
    vfull = _dot(ckv_b, r["wv"][...])
    for pp in range(N_PAIRS):
        _store_value_heads(r["v_mla"], 2 * pp, vfull[:, pp * LANE:(pp + 1) * LANE], lo)

    gq2 = r["gq2"][...]
    q_scale = GQA_HEAD_DIM ** -0.5
    for j in range(N_PAIRS):
        xb = z[:, _W1_GQ + j * LANE:_W1_GQ + (j + 1) * LANE]
        rs = _two_head_rsqrt(xb, lo)
        y = xb * rs * gq2
        if latent:
            y = y * cg + _rot_partner_lanes(y, lane, GQA_HEAD_DIM // 4) * sg
        y = y * q_scale
        r["q_gqa"][2 * j] = jnp.where(lo, y, 0.0).astype(BF16)
        r["q_gqa"][2 * j + 1] = jnp.where(lo, 0.0, y).astype(BF16)
    kb = z[:, _W1_GK:_W1_GK + LANE]
    rs = _two_head_rsqrt(kb, lo)
    kg = kb * rs * r["gk2"][...]
    if latent:
        kg = kg * cg + _rot_partner_lanes(kg, lane, GQA_HEAD_DIM // 4) * sg
    vg = z[:, _W1_GV:_W1_GV + LANE]
    if not latent:
        r["kg_out"][...] = kg
        r["vg_out"][...] = vg
    r["k_gqa"][...] = kg.astype(BF16)
    _store_value_heads(r["v_gqa"], 0, vg, lo)

    r["pool"][...] = z[:, _W1_POOL:_W1_POOL + POOL_WIDTH]


def _inproj_call(x, mod4, layer, mod_base, seq_len, tm, latent, wl, tabs):
    n = x.shape[0]
    nt = n // tm
    per_seq = seq_len // tm if latent else 1

    def tile(i):
        return (i, 0)

    def mod_idx(i):
        if latent:
            return (layer, mod_base + i // per_seq, 0, 0)
        return (layer, mod_base, 0, 0)

    names = ["x", "mod", "w1", "gmix", "gqa", "wqb", "gkva", "wkp", "wv", "gq2", "gk2"]
    args = [x, mod4, wl["w1"]] + [wl[nm] for nm in names[3:]]
    specs = [pl.BlockSpec((tm, D_MODEL), tile), pl.BlockSpec((None, 1, 1, 6 * D_MODEL), mod_idx),
             _wspec(wl["w1"], layer)]
    specs += [_wspec(a, layer) for a in args[3:]]
    if latent:
        names.append("wqbsw")
        args.append(wl["wqbsw"])
        specs.append(_wspec(wl["wqbsw"], layer))
        for nm, a in zip(("cg", "sg", "cm", "sm"), tabs):
            names.append(nm)
            args.append(a)
            specs.append(pl.BlockSpec((tm, LANE), lambda i: (i % per_seq, 0)))

    head_spec = pl.BlockSpec((MLA_HEADS, tm, LANE), lambda i: (0, i, 0))
    kvh_spec = pl.BlockSpec((GQA_KV_HEADS, tm, LANE), lambda i: (0, i, 0))
    out_names = ["q_mla", "k_mla", "v_mla", "q_gqa", "k_gqa", "v_gqa", "pool"]
    out_shapes = [
        jax.ShapeDtypeStruct((MLA_HEADS, n, LANE), BF16),
        jax.ShapeDtypeStruct((MLA_HEADS, n, LANE), BF16),
        jax.ShapeDtypeStruct((MLA_HEADS, n, LANE), BF16),
        jax.ShapeDtypeStruct((GQA_HEADS, n, LANE), BF16),
        jax.ShapeDtypeStruct((n, LANE), BF16),
        jax.ShapeDtypeStruct((GQA_KV_HEADS, n, LANE), BF16),
        jax.ShapeDtypeStruct((n, POOL_WIDTH), F32),
    ]
    out_specs = [head_spec, head_spec, head_spec, head_spec,
                 pl.BlockSpec((tm, LANE), tile), kvh_spec,
                 pl.BlockSpec((tm, POOL_WIDTH), tile)]
    if not latent:
        out_names += ["ckv_out", "kr_out", "kg_out", "vg_out"]
        out_shapes += [jax.ShapeDtypeStruct((n, KV_LORA), F32), jax.ShapeDtypeStruct((n, MLA_ROPE), F32),
                       jax.ShapeDtypeStruct((n, LANE), F32), jax.ShapeDtypeStruct((n, LANE), F32)]
        out_specs += [pl.BlockSpec((tm, KV_LORA), tile), pl.BlockSpec((tm, MLA_ROPE), tile),
                      pl.BlockSpec((tm, LANE), tile), pl.BlockSpec((tm, LANE), tile)]

    outs = pl.pallas_call(
        functools.partial(_inproj_body, latent, names + out_names),
        grid=(nt,),
        in_specs=specs,
        out_specs=out_specs,
        out_shape=out_shapes,
        compiler_params=pltpu.CompilerParams(
            dimension_semantics=("arbitrary",), vmem_limit_bytes=VMEM_LIMIT),
        name="inproj_lat" if latent else "inproj_ctx",
    )(*args)
    return dict(zip(out_names, outs))


def _cachekv_body(ckv_ref, kr_ref, kg_ref, vg_ref, wkp_ref, wv_ref, k_ref, v_ref, kgo_ref, vgo_ref):
    rows = ckv_ref.shape[0] * ckv_ref.shape[1]
    ckv_b = ckv_ref[...].reshape(rows, LANE).astype(BF16)
    lo = lax.broadcasted_iota(jnp.int32, (rows, LANE), 1) < MLA_V
    kr_b = kr_ref[...].reshape(rows, LANE).astype(BF16)
    kfull = _dot(jnp.concatenate([ckv_b, kr_b], axis=1), wkp_ref[...])
    kgo_ref[0] = kg_ref[...].reshape(rows, LANE).astype(BF16)
    for hh in range(MLA_HEADS):
        k_ref[hh] = kfull[:, hh * LANE:(hh + 1) * LANE].astype(BF16)
    vfull = _dot(ckv_b, wv_ref[...])
    for pp in range(N_PAIRS):
        _store_value_heads(v_ref, 2 * pp, vfull[:, pp * LANE:(pp + 1) * LANE], lo)
    _store_value_heads(vgo_ref, 0, vg_ref[...].reshape(rows, LANE), lo)


def _cachekv_call(caches, layer, wl):
    bsz, _, past, _ = caches[0].shape
    n = bsz * past
    cache_spec = pl.BlockSpec((bsz, None, past, LANE), lambda i: (0, layer, 0, 0))
    return pl.pallas_call(
        _cachekv_body,
        grid=(1,),
        in_specs=[cache_spec] * 4 + [_wspec(wl["wkp"], layer), _wspec(wl["wv"], layer)],
        out_specs=[_const_spec((MLA_HEADS, n, LANE)), _const_spec((MLA_HEADS, n, LANE)),
                   _const_spec((1, n, LANE)), _const_spec((GQA_KV_HEADS, n, LANE))],
        out_shape=[jax.ShapeDtypeStruct((MLA_HEADS, n, LANE), BF16),
                   jax.ShapeDtypeStruct((MLA_HEADS, n, LANE), BF16),
                   jax.ShapeDtypeStruct((1, n, LANE), BF16),
                   jax.ShapeDtypeStruct((GQA_KV_HEADS, n, LANE), BF16)],
        compiler_params=pltpu.CompilerParams(
            dimension_semantics=("arbitrary",), vmem_limit_bytes=VMEM_LIMIT),
        name="cache_kv",
    )(*caches, wl["wkp"], wl["wv"])


def _normalise_pair(acc_a, acc_b, lo):
    num = jnp.where(lo, acc_a, acc_b)
    den = pltpu.roll(jnp.where(lo, acc_b, acc_a), MLA_V, axis=1)
    return num / den


def _attn_body(group, seq_len, shared_kv, exp_scale, q_ref, k_ref, v_ref, o_ref):
    lo = lax.broadcasted_iota(jnp.int32, (CTX_Q_ROWS, LANE), 1) < MLA_V
    nt_dims = (((1,), (1,)), ((), ()))
    units = [(pr, qb, e) for pr in range(N_PAIRS) for qb in range(seq_len // CTX_Q_ROWS) for e in range(2)]

    def one_batch_row(g, carry):
        base = pl.multiple_of(g * seq_len, seq_len)
        rows = pl.ds(base, seq_len)

        scores, row_max, accs = {}, {}, {}
        for i in range(len(units) + 2):
            if i < len(units):
                pr, qb, e = units[i]
                qrows = pl.ds(base + qb * CTX_Q_ROWS, CTX_Q_ROWS)
                q = q_ref[2 * pr + e, qrows, :]
                k = k_ref[0 if shared_kv else 2 * pr + e, rows, :]
                scores[i] = lax.dot_general(q, k, nt_dims, preferred_element_type=F32)
            if 0 <= i - 1 < len(units):
                row_max[i - 1] = jnp.max(scores[i - 1], axis=-1, keepdims=True)
            if 0 <= i - 2 < len(units):
                u = i - 2
                pr, qb, e = units[u]
                qrows = pl.ds(base + qb * CTX_Q_ROWS, CTX_Q_ROWS)
                v = v_ref[e if shared_kv else 2 * pr + e, rows, :]
                p = jnp.exp2((scores.pop(u) - row_max.pop(u)) * exp_scale)
                accs[u] = _dot(p.astype(BF16), v)
                if e == 1:
                    o_ref[pr, qrows, :] = _normalise_pair(accs.pop(u - 1), accs.pop(u), lo).astype(BF16)
        return carry

    lax.fori_loop(0, group, one_batch_row, 0)


def _attn_call(q, k, v, batch, seq_len, group, scale, shared_kv, name):
    n = q.shape[1]
    rows = group * seq_len
    exp_scale = scale * math.log2(math.e)

    def spec(heads):
        return pl.BlockSpec((heads, rows, LANE), lambda b: (0, b, 0))

    return pl.pallas_call(
        functools.partial(_attn_body, group, seq_len, shared_kv, exp_scale),
        grid=(batch // group,),
        in_specs=[spec(q.shape[0]), spec(k.shape[0]), spec(v.shape[0])],
        out_specs=pl.BlockSpec((N_PAIRS, rows, LANE), lambda b: (0, b, 0)),
        out_shape=jax.ShapeDtypeStruct((N_PAIRS, n, LANE), BF16),
        compiler_params=pltpu.CompilerParams(
            dimension_semantics=("arbitrary",), vmem_limit_bytes=VMEM_LIMIT),
        name=name,
    )(q, k, v)


def _attn_pipe_body(shared_kv, exp_scale, tq, pairs, q_ref, k_ref, v_ref, kc_ref, vc_ref, o_ref,
                    s0, s1, mr0, mr1, ac0, ac1):
    seq = k_ref.shape[1]
    past = kc_ref.shape[1]
    nq = seq // tq
    n_units = pairs * nq
    n_new = seq // KEY_TILE
    n_tiles = n_new + past // KEY_TILE
    s_scr = (s0, s1)
    mrun_scr = (mr0, mr1)
    acc_scr = (ac0, ac1)
    nt_dims = (((1,), (1,)), ((), ()))
    lo = lax.broadcasted_iota(jnp.int32, (tq, LANE), 1) < MLA_V

    def pair_rows(t):
        j = t % nq
        return t // nq, pl.ds(pl.multiple_of(j * tq, tq), tq)

    def kv_tile(new_ref, cache_ref, head, blk):
        if blk < n_new:
            return new_ref[head, blk * KEY_TILE:(blk + 1) * KEY_TILE, :]
        blk -= n_new
        return cache_ref[head, blk * KEY_TILE:(blk + 1) * KEY_TILE, :]

    def finish_prev(e_prev, t_prev):
        if e_prev == 1:
            pair, rows = pair_rows(t_prev)
            o_ref[pair, rows, :] = _normalise_pair(ac0[...], ac1[...], lo).astype(BF16)

    def region(t_scores, e_scores, t_cur, e_cur, t_prev):
        if t_prev is not None:
            finish_prev(1 - e_cur, t_prev)
        if e_scores is not None:
            pair, rows = pair_rows(t_scores)
            q = q_ref[2 * pair + e_scores, rows, :]
            k_head = 0 if shared_kv else 2 * pair + e_scores
        if e_cur is not None:
            v_head = e_cur if shared_kv else 2 * (t_cur // nq) + e_cur
            m = jnp.max(mrun_scr[e_cur][...], axis=-1, keepdims=True)
            m_b = jnp.broadcast_to(m, (tq, LANE))
        acc = None
        for blk in range(n_tiles):
            cols = slice(blk * KEY_TILE, (blk + 1) * KEY_TILE)
            if e_scores is not None:
                s_t = lax.dot_general(q, kv_tile(k_ref, kc_ref, k_head, blk), nt_dims,
                                      preferred_element_type=F32)
                s_scr[e_scores][:, cols] = s_t
                m_t = jnp.maximum(s_t[:, 0:LANE], s_t[:, LANE:2 * LANE])
                if blk > 0:
                    m_t = jnp.maximum(m_t, mrun_scr[e_scores][...])
                mrun_scr[e_scores][...] = m_t
            if e_cur is not None:
                p_parts = []
                for hh in range(KEY_TILE // LANE):
                    c0 = blk * KEY_TILE + hh * LANE
                    s_h = s_scr[e_cur][:, c0:c0 + LANE]
                    p_parts.append(jnp.exp2((s_h - m_b) * exp_scale).astype(BF16))
                p_t = jnp.concatenate(p_parts, axis=1)
                part = _dot(p_t, kv_tile(v_ref, vc_ref, v_head, blk))
                acc = part if acc is None else acc + part
        if e_cur is not None:
            acc_scr[e_cur][...] = acc

    region(0, 0, None, None, None)
    region(0, 1, 0, 0, None)
    region(1, 0, 0, 1, 0)

    def body(i, carry):
        for t in (1 + 2 * i, 2 + 2 * i):
            region(t, 1, t, 0, t - 1)
            region(t + 1, 0, t, 1, t)
        return carry

    assert (n_units - 2) % 2 == 0
    lax.fori_loop(0, (n_units - 2) // 2, body, 0)
    region(n_units - 1, 1, n_units - 1, 0, n_units - 2)
    region(None, None, n_units - 1, 1, n_units - 1)
    finish_prev(1, n_units - 1)


def _attn_pipe_call(q, k, v, kc, vc, batch, seq_len, tq, scale, shared_kv, name, pairs=LAT_PAIRS):
    n = q.shape[1]
    past = kc.shape[1] // batch
    exp_scale = scale * math.log2(math.e)

    def spec(a, rows):
        heads = a.shape[0]
        if heads < 2 * N_PAIRS:
            return pl.BlockSpec((heads, rows, LANE), lambda b, p: (0, b, 0))
        return pl.BlockSpec((2 * pairs, rows, LANE), lambda b, p: (p, b, 0))

    total = seq_len + past
    return pl.pallas_call(
        functools.partial(_attn_pipe_body, shared_kv, exp_scale, tq, pairs),
        grid=(batch, N_PAIRS // pairs),
        in_specs=[spec(q, seq_len), spec(k, seq_len), spec(v, seq_len), spec(kc, past), spec(vc, past)],
        out_specs=pl.BlockSpec((pairs, seq_len, LANE), lambda b, p: (p, b, 0)),
        out_shape=jax.ShapeDtypeStruct((N_PAIRS, n, LANE), BF16),
        scratch_shapes=[pltpu.VMEM((tq, total), F32), pltpu.VMEM((tq, total), F32)]
        + [pltpu.VMEM((tq, LANE), F32) for _ in range(4)],
        compiler_params=pltpu.CompilerParams(
            dimension_semantics=("arbitrary", "arbitrary"), vmem_limit_bytes=VMEM_LIMIT),
        name=name,
    )(q, k, v, kc, vc)


def _seq_pos(tm, seq_len):
    i = pl.program_id(0)
    row = lax.broadcasted_iota(jnp.int32, (tm, 1), 0) + i * tm
    return jnp.bitwise_and(row, seq_len - 1)


def _merge_body(seq_len, x_ref, mod_ref, gmix_ref, wg_ref, a_ref, b_ref, pc_ref, pl_ref, pr_ref,
                wpool_ref, pscale_ref, wa_ref, wb_ref, wc_ref, wo_ref, o_ref, merged_scr, ab_scr):
    x = x_ref[...]
    tm = x.shape[0]
    mod = mod_ref[0]
    sh1 = mod[:, 0:D_MODEL]
    sc1 = mod[:, D_MODEL:2 * D_MODEL]
    g1 = mod[:, 2 * D_MODEL:3 * D_MODEL]
    h = (_rms(x, gmix_ref[...]) * (1.0 + sc1) + sh1).astype(BF16)

    pos = _seq_pos(tm, seq_len)
    pscale = pscale_ref[...]
    centre_rows = pc_ref[...]
    if seq_len >= tm:
        start = (pl.program_id(0) * tm) % seq_len
        left = jnp.where(start == 0, 0.0, pl_ref[...])
        right = jnp.where(start + tm == seq_len, 0.0, pr_ref[...])
        blocks = [jnp.concatenate([left, centre_rows, right], axis=0)]
    else:
        zeros = jnp.zeros((HALO, POOL_WIDTH), F32)
        blocks = [jnp.concatenate([zeros, centre_rows[s * seq_len:(s + 1) * seq_len], zeros], axis=0)
                  for s in range(tm // seq_len)]

    def pool_group(g):
        w = POOL_WINDOWS[g]
        gcols = slice(g * POOL_GROUP, (g + 1) * POOL_GROUP)
        sums = []
        for blk in blocks:
            rows = blk.shape[0] - 2 * HALO
            eg = blk[:, gcols]
            wsum = eg[HALO:HALO + rows]
            for dlt in range(-(w // 2), w - w // 2):
                if dlt != 0:
                    wsum = wsum + eg[HALO + dlt:HALO + dlt + rows]
            sums.append(wsum)
        wsum = sums[0] if len(sums) == 1 else jnp.concatenate(sums, axis=0)
        cnt = jnp.minimum(pos + (w - w // 2), seq_len) - jnp.maximum(pos - w // 2, 0)
        return (wsum / cnt.astype(F32) - centre_rows[:, gcols]).astype(BF16)

    def pool_pair_map(p, pooled_a, pooled_b):
        both = jnp.concatenate([pooled_a, pooled_b], axis=1)
        return _dot(both, wpool_ref[p]) * pscale[:, 2 * p * POOL_GROUP:2 * (p + 1) * POOL_GROUP]

    def gated(branch, src, w_ref, j):
        c0 = branch * D_MODEL + j * FF_CHUNK
        gate = jax.nn.sigmoid(_dot(h, wg_ref[:, c0:c0 + FF_CHUNK]))
        return gate * _dot(src, w_ref[:, j * FF_CHUNK:(j + 1) * FF_CHUNK])

    a_out = jnp.concatenate([a_ref[p] for p in range(N_PAIRS)], axis=1)
    b_out = jnp.concatenate([b_ref[p] for p in range(N_PAIRS)], axis=1)
    n_chunks = D_MODEL // FF_CHUNK
    pooled = []
    c_parts = []
    for j in range(n_chunks):
        cols = slice(j * FF_CHUNK, (j + 1) * FF_CHUNK)
        for g in range(j * len(POOL_WINDOWS) // n_chunks, (j + 1) * len(POOL_WINDOWS) // n_chunks):
            pooled.append(pool_group(g))
            if g % 2 == 1:
                c_parts.append(pool_pair_map(g // 2, pooled[g - 1], pooled[g]))
        ab_scr[:, cols] = gated(0, a_out, wa_ref, j) + gated(1, b_out, wb_ref, j)
    c_out = jnp.concatenate(c_parts, axis=-1).astype(BF16)
    for j in range(n_chunks):
        cols = slice(j * FF_CHUNK, (j + 1) * FF_CHUNK)
        merged_scr[:, cols] = (ab_scr[:, cols] + gated(2, c_out, wc_ref, j)).astype(BF16)
    o_ref[...] = x + g1 * _dot(merged_scr[...], wo_ref[...])


def _halo_specs(tm, width, n):
    blocks = tm // HALO
    last = n // HALO - 1
    left = pl.BlockSpec((HALO, width), lambda i: (jnp.maximum(i * blocks - 1, 0), 0))
    right = pl.BlockSpec((HALO, width), lambda i: (jnp.minimum((i + 1) * blocks, last), 0))
    return left, right


def _mod_spec(layer, mod_base, per_seq, latent):
    if latent:
        return pl.BlockSpec((None, 1, 1, 6 * D_MODEL), lambda i: (layer, mod_base + i // per_seq, 0, 0))
    return pl.BlockSpec((None, 1, 1, 6 * D_MODEL), lambda i: (layer, mod_base, 0, 0))


def _merge_call(x, mod4, layer, mod_base, seq_len, tm, latent, a_out, b_out, pool_in, wl):
    n = x.shape[0]
    per_seq = seq_len // tm if latent else 1

    def tile(i):
        return (i, 0)

    left, right = _halo_specs(tm, POOL_WIDTH, n)
    consts = [wl["wpool"], wl["pscale"], wl["wa"], wl["wb"], wl["wc"], wl["wo"]]
    in_specs = [pl.BlockSpec((tm, D_MODEL), tile), _mod_spec(layer, mod_base, per_seq, latent),
                _wspec(wl["gmix"], layer), _wspec(wl["wg"], layer),
                pl.BlockSpec((N_PAIRS, tm, LANE), lambda i: (0, i, 0)),
                pl.BlockSpec((N_PAIRS, tm, LANE), lambda i: (0, i, 0)),
                pl.BlockSpec((tm, POOL_WIDTH), tile), left, right]
    in_specs += [_wspec(a, layer) for a in consts]
    return pl.pallas_call(
        functools.partial(_merge_body, seq_len),
        grid=(n // tm,),
        in_specs=in_specs,
        out_specs=pl.BlockSpec((tm, D_MODEL), tile),
        out_shape=jax.ShapeDtypeStruct((n, D_MODEL), F32),
        scratch_shapes=[pltpu.VMEM((tm, D_MODEL), BF16), pltpu.VMEM((tm, D_MODEL), F32)],
        compiler_params=pltpu.CompilerParams(
            dimension_semantics=("arbitrary",), vmem_limit_bytes=VMEM_LIMIT),
        name="merge_lat" if latent else "merge_ctx",
    )(x, mod4, wl["gmix"], wl["wg"], a_out, b_out, pool_in, pool_in, pool_in, *consts)


def _ffn_body(seq_len, final, x_ref, xl_ref, xr_ref, mod_ref, gffn_ref, wug_ref, wuv_ref, cw_ref, cb_ref,
              wd_ref, gfin_ref, o_ref, act_scr):
    x = x_ref[...]
    tm = x.shape[0]
    mod = mod_ref[0]
    sh2 = mod[:, 3 * D_MODEL:4 * D_MODEL]
    sc2 = mod[:, 4 * D_MODEL:5 * D_MODEL]
    g2 = mod[:, 5 * D_MODEL:6 * D_MODEL]
    xe = jnp.concatenate([xl_ref[...], x, xr_ref[...]], axis=0)
    h2e = _rms(xe, gffn_ref[...]) * (1.0 + sc2) + sh2
    h2 = h2e[HALO:HALO + tm].astype(BF16)
    h2e = h2e.astype(BF16)

    pos = _seq_pos(tm, seq_len)
    has_prev = pos >= 1
    has_next = pos <= seq_len - 2
    cw = cw_ref[...]
    cb = cb_ref[...]
    for j in range(D_FF // FF_CHUNK):
        cols = slice(j * FF_CHUNK, (j + 1) * FF_CHUNK)
        ge = _dot(h2e, wug_ref[:, cols])
        g = (jnp.where(has_prev, ge[HALO - 1:HALO - 1 + tm], 0.0) * cw[0:1, cols]
             + ge[HALO:HALO + tm] * cw[1:2, cols]
             + jnp.where(has_next, ge[HALO + 1:HALO + 1 + tm], 0.0) * cw[2:3, cols]
             + cb[:, cols])
        val = _dot(h2, wuv_ref[:, cols])
        act_scr[:, cols] = (g * jax.nn.sigmoid(g) * val).astype(BF16)
    y = x + g2 * _dot(act_scr[...], wd_ref[...])
    if final:
        y = _rms(y, gfin_ref[...])
    o_ref[...] = y


def _ffn_call(x, mod4, layer, mod_base, seq_len, tm, latent, final, wl, g_final):
    n = x.shape[0]
    per_seq = seq_len // tm if latent else 1

    def tile(i):
        return (i, 0)

    left, right = _halo_specs(tm, D_MODEL, n)
    consts = [wl["gffn"], wl["wu"], wl["wu"], wl["cw"], wl["cb"], wl["wd"], g_final]
    in_specs = [pl.BlockSpec((tm, D_MODEL), tile), left, right, _mod_spec(layer, mod_base, per_seq, latent),
                _wspec(wl["gffn"], layer), _wspec(wl["wu"], layer, cols=D_FF, col_block=0),
                _wspec(wl["wu"], layer, cols=D_FF, col_block=1), _wspec(wl["cw"], layer),
                _wspec(wl["cb"], layer), _wspec(wl["wd"], layer), _const_spec(g_final.shape)]
    return pl.pallas_call(
        functools.partial(_ffn_body, seq_len, final),
        grid=(n // tm,),
        in_specs=in_specs,
        out_specs=pl.BlockSpec((tm, D_MODEL), tile),
        out_shape=jax.ShapeDtypeStruct((n, D_MODEL), F32),
        scratch_shapes=[pltpu.VMEM((tm, D_FF), BF16)],
        compiler_params=pltpu.CompilerParams(
            dimension_semantics=("arbitrary",), vmem_limit_bytes=VMEM_LIMIT),
        name="ffn_lat" if latent else "ffn_ctx",
    )(x, x, x, mod4, *consts)


def _rot_partner(a):
    d = a.shape[-1]
    return jnp.flip(a.reshape(a.shape[:-1] + (2, 2, d // 4)), axis=-2).reshape(a.shape)


def _prep_weights(g_norm_mix, w_in, g_q_a, w_q_b, g_kv_a, w_kv_b, g_q_gqa, g_k_gqa, w_pool, pool_scale,
                  w_br_a, w_br_b, w_br_c, w_out, g_norm_ffn, w_up, conv_w, conv_b, w_down):
    depth = w_in.shape[0]
    hd = GQA_HEAD_DIM
    grp = GQA_HEADS // GQA_KV_HEADS

    def lane_pad(a, before, width=LANE):
        cfg = [(0, 0)] * (a.ndim - 1) + [(before, width - before - a.shape[-1])]
        return jnp.pad(a, cfg)

    def pair_heads(a):
        lead = a.shape[:-1]
        return a.reshape(lead + (GQA_KV_HEADS, grp, hd)).swapaxes(-3, -2).reshape(lead + (GQA_HEADS * hd,))

    w1 = jnp.concatenate(
        [w_in[:, :, _OFF_QA:_OFF_KR], pair_heads(w_in[:, :, _OFF_GQ:_OFF_GK]), w_in[:, :, _OFF_GK:_OFF_GATE],
         lane_pad(w_in[:, :, _OFF_KR:_OFF_GQ], MLA_NOPE)], axis=-1).astype(BF16)
    assert w1.shape[-1] == _W1_COLS

    qb = w_q_b.reshape(depth, Q_LORA, MLA_HEADS, MLA_NOPE + MLA_ROPE)
    wqb = lane_pad(qb, 0).reshape(depth, Q_LORA, MLA_HEADS * LANE).astype(BF16)
    wqbsw = lane_pad(_rot_partner(qb[..., MLA_NOPE:]), MLA_NOPE).reshape(
        depth, Q_LORA, MLA_HEADS * LANE).astype(BF16)

    kvb = w_kv_b.reshape(depth, KV_LORA, MLA_HEADS, MLA_NOPE + MLA_V)
    wk = lane_pad(kvb[..., :MLA_NOPE], 0).reshape(depth, KV_LORA, MLA_HEADS * LANE).astype(BF16)
    wv = kvb[..., MLA_NOPE:].reshape(depth, KV_LORA, MLA_HEADS * MLA_V).astype(BF16)

    pk = np.zeros((LANE, MLA_HEADS * LANE), np.float32)
    for hh in range(MLA_HEADS):
        for t in range(MLA_ROPE):
            pk[MLA_NOPE + t, hh * LANE + MLA_NOPE + t] = 1.0
    wkp = jnp.concatenate([wk, jnp.broadcast_to(jnp.asarray(pk, BF16), (depth,) + pk.shape)], axis=1)

    wp = w_pool.reshape(depth, len(POOL_WINDOWS) // 2, 2, POOL_GROUP, POOL_GROUP)
    zero = jnp.zeros_like(wp[:, :, 0])
    wpool2 = jnp.concatenate([jnp.concatenate([wp[:, :, 0], zero], axis=-1),
                              jnp.concatenate([zero, wp[:, :, 1]], axis=-1)], axis=-2).astype(BF16)

    row = lambda a: a[:, None, :]
    two = lambda a: jnp.concatenate([a, a], axis=-1)
    wb = w_br_b.reshape(depth, GQA_KV_HEADS, grp, hd, D_MODEL).swapaxes(1, 2).reshape(depth, GQA_HEADS * hd, D_MODEL)
    return dict(
        gmix=row(g_norm_mix), w1=w1, gqa=row(g_q_a), wqb=wqb, wqbsw=wqbsw, gkva=row(g_kv_a), wkp=wkp,
        wv=wv, gq2=row(two(g_q_gqa)), gk2=row(two(g_k_gqa)),
        wg=w_in[:, :, _OFF_GATE:].astype(BF16), wpool=wpool2, pscale=row(pool_scale),
        wa=w_br_a.astype(BF16), wb=wb.astype(BF16), wc=w_br_c.astype(BF16), wo=w_out.astype(BF16),
        gffn=row(g_norm_ffn), wu=w_up.astype(BF16), cw=conv_w, cb=row(conv_b), wd=w_down.astype(BF16),
    )


def _layer(x, mod4, layer, mod_base, batch, seq_len, latent, final, wl, g_final, tabs, caches):
    tm = TOKEN_TILE
    tq = Q_TILE
    assert tm % seq_len == 0 or seq_len % tm == 0
    pj = _inproj_call(x, mod4, layer, mod_base, seq_len, tm, latent, wl, tabs)
    if caches is not None:
        kc_m, vc_m, kc_g, vc_g = _cachekv_call(caches, layer, wl)
    mla_scale = (MLA_NOPE + MLA_ROPE) ** -0.5
    n = x.shape[0]
    k_g = pj["k_gqa"].reshape(1, n, LANE)
    if latent:
        a_out = _attn_pipe_call(pj["q_mla"], pj["k_mla"], pj["v_mla"], kc_m, vc_m, batch, seq_len, tq,
                                mla_scale, False, "attn_mla_lat")
        b_out = _attn_pipe_call(pj["q_gqa"], k_g, pj["v_gqa"], kc_g, vc_g, batch, seq_len, tq, 1.0, True,
                                "attn_gqa_lat")
    else:
        a_out = _attn_call(pj["q_mla"], pj["k_mla"], pj["v_mla"], batch, seq_len, CTX_GROUP,
                           mla_scale, False, "attn_mla_ctx")
        b_out = _attn_call(pj["q_gqa"], k_g, pj["v_gqa"], batch, seq_len, CTX_GROUP, 1.0, True,
                           "attn_gqa_ctx")
    x1 = _merge_call(x, mod4, layer, mod_base, seq_len, tm, latent, a_out, b_out, pj["pool"], wl)
    x2 = _ffn_call(x1, mod4, layer, mod_base, seq_len, tm, latent, final, wl, g_final)
    return x2, pj


def kernel(x_prompt, x_sample, c, cache_mla_ckv, cache_mla_krope, cache_gqa_k, cache_gqa_v, c_ctx, w_ada, b_ada, g_norm_mix, w_in, g_q_a, w_q_b, g_kv_a, w_kv_b, g_q_gqa, g_k_gqa, w_pool, pool_scale, w_br_a, w_br_b, w_br_c, w_out, g_norm_ffn, w_up, conv_w, conv_b, w_down, g_final):
    depth = w_in.shape[0]
    bc, tc, _ = x_prompt.shape
    bl, tl, _ = x_sample.shape
    past = cache_mla_ckv.shape[2]
    mod_rows = 8
    assert 1 + bl <= mod_rows and tc & (tc - 1) == 0 and tl & (tl - 1) == 0

    cin = jnp.concatenate([c_ctx[None, :], c, jnp.zeros((mod_rows - 1 - bl, D_MODEL), F32)], axis=0)
    mod4 = _ada_call(cin, w_ada, b_ada).reshape(depth, mod_rows, 1, 6 * D_MODEL)
    tabs = _rope_tables(tl)
    g_fin = g_final.reshape(1, D_MODEL)
    wl = _prep_weights(g_norm_mix, w_in, g_q_a, w_q_b, g_kv_a, w_kv_b, g_q_gqa, g_k_gqa, w_pool,
                       pool_scale, w_br_a, w_br_b, w_br_c, w_out, g_norm_ffn, w_up, conv_w, conv_b, w_down)
    kr_pad = jnp.pad(cache_mla_krope, ((0, 0), (0, 0), (0, 0), (MLA_NOPE, LANE - MLA_NOPE - MLA_ROPE)))
    caches = (cache_mla_ckv, kr_pad, cache_gqa_k.reshape(bl, depth, past, LANE),
              cache_gqa_v.reshape(bl, depth, past, LANE))

    xc = x_prompt.reshape(bc * tc, D_MODEL)
    xl = x_sample.reshape(bl * tl, D_MODEL)
    st_ckv, st_kr, st_k, st_v = [], [], [], []
    for l in range(depth):
        final = l == depth - 1
        xc, pj = _layer(xc, mod4, l, 0, bc, tc, False, final, wl, g_fin, None, None)
        st_ckv.append(pj["ckv_out"].reshape(bc, tc, KV_LORA))
        st_kr.append(pj["kr_out"].reshape(bc, tc, MLA_ROPE))
        st_k.append(pj["kg_out"].reshape(bc, tc, GQA_KV_HEADS, GQA_HEAD_DIM))
        st_v.append(pj["vg_out"].reshape(bc, tc, GQA_KV_HEADS, GQA_HEAD_DIM))
        xl, _ = _layer(xl, mod4, l, 1, bl, tl, True, final, wl, g_fin, tabs, caches)

    return (xc.reshape(bc, tc, D_MODEL), xl.reshape(bl, tl, D_MODEL),
            jnp.stack(st_ckv, axis=1), jnp.stack(st_kr, axis=1),
            jnp.stack(st_k, axis=1), jnp.stack(st_v, axis=1))
```

```python
import functools
import math

import numpy as np
import jax
import jax.numpy as jnp
from jax import lax
from jax.experimental import pallas as pl
from jax.experimental.pallas import tpu as pltpu

D_MODEL = 1024
GRID_W = 64
RMS_EPS = 1e-6
ROPE_THETA = 10000.0
MLA_HEADS = 8
MLA_NOPE = 64
MLA_ROPE = 32
MLA_V = 64
Q_LORA = 256
KV_LORA = 128
GQA_HEADS = 8
GQA_KV_HEADS = 2
GQA_HEAD_DIM = 64
POOL_WINDOWS = (2, 4, 8, 16)
POOL_GROUP = 128
POOL_WIDTH = POOL_GROUP * len(POOL_WINDOWS)
D_FF = 2816

LANE = 128
HALO = 8
FF_CHUNK = 256
KEY_TILE = 256
CTX_GROUP = 8
CTX_Q_ROWS = 256
TOKEN_TILE = 512
Q_TILE = 512
LAT_PAIRS = 2
N_PAIRS = MLA_HEADS // 2
VMEM_LIMIT = 56 * 1024 * 1024

BF16 = jnp.bfloat16
F32 = jnp.float32

_OFF_QA = 0
_OFF_CKV = _OFF_QA + Q_LORA
_OFF_KR = _OFF_CKV + KV_LORA
_OFF_GQ = _OFF_KR + MLA_ROPE
_OFF_GK = _OFF_GQ + GQA_HEADS * GQA_HEAD_DIM
_OFF_GV = _OFF_GK + GQA_KV_HEADS * GQA_HEAD_DIM
_OFF_POOL = _OFF_GV + GQA_KV_HEADS * GQA_HEAD_DIM
_OFF_GATE = _OFF_POOL + POOL_WIDTH

_W1_QA = 0
_W1_CKV = 256
_W1_GQ = 384
_W1_GK = 896
_W1_GV = 1024
_W1_POOL = 1152
_W1_KR = 1664
_W1_COLS = 1792


def _const_spec(shape):
    nd = len(shape)
    return pl.BlockSpec(shape, lambda *_: (0,) * nd, pipeline_mode=pl.Buffered(1))


def _wspec(a, layer, cols=None, col_block=0):
    shape = a.shape[1:] if cols is None else a.shape[1:-1] + (cols,)
    idx = (layer,) + (0,) * (len(shape) - 1) + (col_block,)
    return pl.BlockSpec((None,) + shape, lambda *_: idx, pipeline_mode=pl.Buffered(1))


def _rms(x, g):
    ms = jnp.mean(x * x, axis=-1, keepdims=True)
    return x * lax.rsqrt(ms + RMS_EPS) * g


def _dot(a, b):
    return jnp.dot(a, b, preferred_element_type=F32)


def _rope_head_tables(n_tokens, d):
    dim_axis = d // 2
    t = np.arange(n_tokens)
    row = (t // GRID_W).astype(np.float64)
    col = (t % GRID_W).astype(np.float64)
    freqs = ROPE_THETA ** (-np.arange(0, dim_axis, 2, dtype=np.float64) / dim_axis)
    ar = row[:, None] * freqs[None, :]
    ac = col[:, None] * freqs[None, :]
    cos = np.concatenate([np.cos(ar), np.cos(ar), np.cos(ac), np.cos(ac)], axis=1)
    sin = np.concatenate([-np.sin(ar), np.sin(ar), -np.sin(ac), np.sin(ac)], axis=1)
    return cos, sin


def _rope_tables(n_tokens):
    cg, sg = _rope_head_tables(n_tokens, GQA_HEAD_DIM)
    cg = np.concatenate([cg, cg], axis=1)
    sg = np.concatenate([sg, sg], axis=1)
    cm32, sm32 = _rope_head_tables(n_tokens, MLA_ROPE)
    pad = LANE - MLA_NOPE - MLA_ROPE
    cm = np.concatenate([np.ones((n_tokens, MLA_NOPE)), cm32, np.zeros((n_tokens, pad))], axis=1)
    sm = np.concatenate([np.zeros((n_tokens, MLA_NOPE)), sm32, np.zeros((n_tokens, pad))], axis=1)
    return tuple(jnp.asarray(a, F32) for a in (cg, sg, cm, sm))


def _ada_body(c_ref, w_ref, b_ref, o_ref):
    c = c_ref[...]
    s = (c * jax.nn.sigmoid(c)).astype(BF16)
    o_ref[0] = _dot(s, w_ref[0].astype(BF16)) + b_ref[0]


def _ada_call(cin, w_ada, b_ada):
    depth, d, n = w_ada.shape
    cols = 2048
    rows = cin.shape[0]
    return pl.pallas_call(
        _ada_body,
        grid=(depth, n // cols),
        in_specs=[
            pl.BlockSpec((rows, d), lambda l, j: (0, 0)),
            pl.BlockSpec((1, d, cols), lambda l, j: (l, 0, j)),
            pl.BlockSpec((1, 1, cols), lambda l, j: (l, 0, j)),
        ],
        out_specs=pl.BlockSpec((1, rows, cols), lambda l, j: (l, 0, j)),
        out_shape=jax.ShapeDtypeStruct((depth, rows, n), F32),
        compiler_params=pltpu.CompilerParams(
            dimension_semantics=("arbitrary", "arbitrary"), vmem_limit_bytes=VMEM_LIMIT),
        name="ada_mod",
    )(cin, w_ada, b_ada.reshape(depth, 1, n))


def _two_head_rsqrt(xb, lo):
    x2 = xb * xb
    s_lo = jnp.sum(jnp.where(lo, x2, 0.0), axis=-1, keepdims=True)
    s_hi = jnp.sum(jnp.where(lo, 0.0, x2), axis=-1, keepdims=True)
    ms = jnp.where(lo, s_lo, s_hi) * (1.0 / GQA_HEAD_DIM)
    return lax.rsqrt(ms + RMS_EPS)


def _rot_partner_lanes(xb, lane, quarter):
    first = jnp.bitwise_and(lane, quarter) == 0
    return jnp.where(first, pltpu.roll(xb, LANE - quarter, axis=1), pltpu.roll(xb, quarter, axis=1))


def _store_value_heads(v_ref, first, blk, lo):
    v_ref[first] = jnp.where(lo, blk, 1.0).astype(BF16)
    v_ref[first + 1] = jnp.where(lo, 1.0, blk).astype(BF16)


def _inproj_body(latent, names, *refs):
    r = dict(zip(names, refs))
    x = r["x"][...]
    tm = x.shape[0]
    mod = r["mod"][0]
    sh1 = mod[:, 0:D_MODEL]
    sc1 = mod[:, D_MODEL:2 * D_MODEL]
    h = (_rms(x, r["gmix"][...]) * (1.0 + sc1) + sh1).astype(BF16)
    z = _dot(h, r["w1"][...])

    lane = lax.broadcasted_iota(jnp.int32, (tm, LANE), 1)
    lo = lane < GQA_HEAD_DIM

    if latent:
        cg = r["cg"][...]
        sg = r["sg"][...]
        cm = r["cm"][...]
        sm = r["sm"][...]

    qn = _rms(z[:, _W1_QA:_W1_QA + Q_LORA], r["gqa"][...]).astype(BF16)
    qm = _dot(qn, r["wqb"][...])
    if latent:
        qms = _dot(qn, r["wqbsw"][...])
    for hh in range(MLA_HEADS):
        blk = qm[:, hh * LANE:(hh + 1) * LANE]
        if latent:
            blk = blk * cm + qms[:, hh * LANE:(hh + 1) * LANE] * sm
        r["q_mla"][hh] = blk.astype(BF16)

    ckv = _rms(z[:, _W1_CKV:_W1_CKV + KV_LORA], r["gkva"][...])
    kr = z[:, _W1_KR:_W1_KR + LANE]
    if latent:
        kr = kr * cm + _rot_partner_lanes(kr, lane, MLA_ROPE // 4) * sm
    else:
        r["ckv_out"][...] = ckv
        r["kr_out"][...] = kr[:, MLA_NOPE:MLA_NOPE + MLA_ROPE]
    ckv_b = ckv.astype(BF16)
    kfull = _dot(jnp.concatenate([ckv_b, kr.astype(BF16)], axis=1), r["wkp"][...])
    for hh in range(MLA_HEADS):
        r["k_mla"][hh] = kfull[:, hh * LANE:(hh + 1) * LANE].astype(BF16)
    vfull = _dot(ckv_b, r["wv"][...])
    for pp in range(N_PAIRS):
        _store_value_heads(r["v_mla"], 2 * pp, vfull[:, pp * LANE:(pp + 1) * LANE], lo)

    gq2 = r["gq2"][...]
    q_scale = GQA_HEAD_DIM ** -0.5
    for j in range(N_PAIRS):
        xb = z[:, _W1_GQ + j * LANE:_W1_GQ + (j + 1) * LANE]
        rs = _two_head_rsqrt(xb, lo)
        y = xb * rs * gq2
        if latent:
            y = y * cg + _rot_partner_lanes(y, lane, GQA_HEAD_DIM // 4) * sg
        y = y * q_scale
        r["q_gqa"][2 * j] = jnp.where(lo, y, 0.0).astype(BF16)
        r["q_gqa"][2 * j + 1] = jnp.where(lo, 0.0, y).astype(BF16)
    kb = z[:, _W1_GK:_W1_GK + LANE]
    rs = _two_head_rsqrt(kb, lo)
    kg = kb * rs * r["gk2"][...]
    if latent:
        kg = kg * cg + _rot_partner_lanes(kg, lane, GQA_HEAD_DIM // 4) * sg
    vg = z[:, _W1_GV:_W1_GV + LANE]
    if not latent:
        r["kg_out"][...] = kg
        r["vg_out"][...] = vg
    r["k_gqa"][...] = kg.astype(BF16)
    _store_value_heads(r["v_gqa"], 0, vg, lo)

    r["pool"][...] = z[:, _W1_POOL:_W1_POOL + POOL_WIDTH]


def _inproj_call(x, mod4, layer, mod_base, seq_len, tm, latent, wl, tabs):
    n = x.shape[0]
    nt = n // tm
    per_seq = seq_len // tm if latent else 1

    def tile(i):
        return (i, 0)

    def mod_idx(i):
        if latent:
            return (layer, mod_base + i // per_seq, 0, 0)
        return (layer, mod_base, 0, 0)

    names = ["x", "mod", "w1", "gmix", "gqa", "wqb", "gkva", "wkp", "wv", "gq2", "gk2"]
    args = [x, mod4, wl["w1"]] + [wl[nm] for nm in names[3:]]
    specs = [pl.BlockSpec((tm, D_MODEL), tile), pl.BlockSpec((None, 1, 1, 6 * D_MODEL), mod_idx),
             _wspec(wl["w1"], layer)]
    specs += [_wspec(a, layer) for a in args[3:]]
    if latent:
        names.append("wqbsw")
        args.append(wl["wqbsw"])
        specs.append(_wspec(wl["wqbsw"], layer))
        for nm, a in zip(("cg", "sg", "cm", "sm"), tabs):
            names.append(nm)
            args.append(a)
            specs.append(pl.BlockSpec((tm, LANE), lambda i: (i % per_seq, 0)))

    head_spec = pl.BlockSpec((MLA_HEADS, tm, LANE), lambda i: (0, i, 0))
    kvh_spec = pl.BlockSpec((GQA_KV_HEADS, tm, LANE), lambda i: (0, i, 0))
    out_names = ["q_mla", "k_mla", "v_mla", "q_gqa", "k_gqa", "v_gqa", "pool"]
    out_shapes = [
        jax.ShapeDtypeStruct((MLA_HEADS, n, LANE), BF16),
        jax.ShapeDtypeStruct((MLA_HEADS, n, LANE), BF16),
        jax.ShapeDtypeStruct((MLA_HEADS, n, LANE), BF16),
        jax.ShapeDtypeStruct((GQA_HEADS, n, LANE), BF16),
        jax.ShapeDtypeStruct((n, LANE), BF16),
        jax.ShapeDtypeStruct((GQA_KV_HEADS, n, LANE), BF16),
        jax.ShapeDtypeStruct((n, POOL_WIDTH), F32),
    ]
    out_specs = [head_spec, head_spec, head_spec, head_spec,
                 pl.BlockSpec((tm, LANE), tile), kvh_spec,
                 pl.BlockSpec((tm, POOL_WIDTH), tile)]
    if not latent:
        out_names += ["ckv_out", "kr_out", "kg_out", "vg_out"]
        out_shapes += [jax.ShapeDtypeStruct((n, KV_LORA), F32), jax.ShapeDtypeStruct((n, MLA_ROPE), F32),
                       jax.ShapeDtypeStruct((n, LANE), F32), jax.ShapeDtypeStruct((n, LANE), F32)]
        out_specs += [pl.BlockSpec((tm, KV_LORA), tile), pl.BlockSpec((tm, MLA_ROPE), tile),
                      pl.BlockSpec((tm, LANE), tile), pl.BlockSpec((tm, LANE), tile)]

    outs = pl.pallas_call(
        functools.partial(_inproj_body, latent, names + out_names),
        grid=(nt,),
        in_specs=specs,
        out_specs=out_specs,
        out_shape=out_shapes,
        compiler_params=pltpu.CompilerParams(
            dimension_semantics=("arbitrary",), vmem_limit_bytes=VMEM_LIMIT),
        name="inproj_lat" if latent else "inproj_ctx",
    )(*args)
    return dict(zip(out_names, outs))


def _cachekv_body(ckv_ref, kr_ref, kg_ref, vg_ref, wkp_ref, wv_ref, k_ref, v_ref, kgo_ref, vgo_ref):
    rows = ckv_ref.shape[0] * ckv_ref.shape[1]
    ckv_b = ckv_ref[...].reshape(rows, LANE).astype(BF16)
    lo = lax.broadcasted_iota(jnp.int32, (rows, LANE), 1) < MLA_V
    kr_b = kr_ref[...].reshape(rows, LANE).astype(BF16)
    kfull = _dot(jnp.concatenate([ckv_b, kr_b], axis=1), wkp_ref[...])
    kgo_ref[0] = kg_ref[...].reshape(rows, LANE).astype(BF16)
    for hh in range(MLA_HEADS):
        k_ref[hh] = kfull[:, hh * LANE:(hh + 1) * LANE].astype(BF16)
    vfull = _dot(ckv_b, wv_ref[...])
    for pp in range(N_PAIRS):
        _store_value_heads(v_ref, 2 * pp, vfull[:, pp * LANE:(pp + 1) * LANE], lo)
    _store_value_heads(vgo_ref, 0, vg_ref[...].reshape(rows, LANE), lo)


def _cachekv_call(caches, layer, wl):
    bsz, _, past, _ = caches[0].shape
    n = bsz * past
    cache_spec = pl.BlockSpec((bsz, None, past, LANE), lambda i: (0, layer, 0, 0))
    return pl.pallas_call(
        _cachekv_body,
        grid=(1,),
        in_specs=[cache_spec] * 4 + [_wspec(wl["wkp"], layer), _wspec(wl["wv"], layer)],
        out_specs=[_const_spec((MLA_HEADS, n, LANE)), _const_spec((MLA_HEADS, n, LANE)),
                   _const_spec((1, n, LANE)), _const_spec((GQA_KV_HEADS, n, LANE))],
        out_shape=[jax.ShapeDtypeStruct((MLA_HEADS, n, LANE), BF16),
                   jax.ShapeDtypeStruct((MLA_HEADS, n, LANE), BF16),
                   jax.ShapeDtypeStruct((1, n, LANE), BF16),
                   jax.ShapeDtypeStruct((GQA_KV_HEADS, n, LANE), BF16)],
        compiler_params=pltpu.CompilerParams(
            dimension_semantics=("arbitrary",), vmem_limit_bytes=VMEM_LIMIT),
        name="cache_kv",
    )(*caches, wl["wkp"], wl["wv"])


def _normalise_pair(acc_a, acc_b, lo):
    num = jnp.where(lo, acc_a, acc_b)
    den = pltpu.roll(jnp.where(lo, acc_b, acc_a), MLA_V, axis=1)
    return num / den


def _attn_body(group, seq_len, shared_kv, exp_scale, q_ref, k_ref, v_ref, o_ref):
    lo = lax.broadcasted_iota(jnp.int32, (CTX_Q_ROWS, LANE), 1) < MLA_V
    nt_dims = (((1,), (1,)), ((), ()))
    units = [(pr, qb, e) for pr in range(N_PAIRS) for qb in range(seq_len // CTX_Q_ROWS) for e in range(2)]

    def one_batch_row(g, carry):
        base = pl.multiple_of(g * seq_len, seq_len)
        rows = pl.ds(base, seq_len)

        scores, row_max, accs = {}, {}, {}
        for i in range(len(units) + 2):
            if i < len(units):
                pr, qb, e = units[i]
                qrows = pl.ds(base + qb * CTX_Q_ROWS, CTX_Q_ROWS)
                q = q_ref[2 * pr + e, qrows, :]
                k = k_ref[0 if shared_kv else 2 * pr + e, rows, :]
                scores[i] = lax.dot_general(q, k, nt_dims, preferred_element_type=F32)
            if 0 <= i - 1 < len(units):
                row_max[i - 1] = jnp.max(scores[i - 1], axis=-1, keepdims=True)
            if 0 <= i - 2 < len(units):
                u = i - 2
                pr, qb, e = units[u]
                qrows = pl.ds(base + qb * CTX_Q_ROWS, CTX_Q_ROWS)
                v = v_ref[e if shared_kv else 2 * pr + e, rows, :]
                p = jnp.exp2((scores.pop(u) - row_max.pop(u)) * exp_scale)
                accs[u] = _dot(p.astype(BF16), v)
                if e == 1:
                    o_ref[pr, qrows, :] = _normalise_pair(accs.pop(u - 1), accs.pop(u), lo).astype(BF16)
        return carry

    lax.fori_loop(0, group, one_batch_row, 0)


def _attn_call(q, k, v, batch, seq_len, group, scale, shared_kv, name):
    n = q.shape[1]
    rows = group * seq_len
    exp_scale = scale * math.log2(math.e)

    def spec(heads):
        return pl.BlockSpec((heads, rows, LANE), lambda b: (0, b, 0))

    return pl.pallas_call(
        functools.partial(_attn_body, group, seq_len, shared_kv, exp_scale),
        grid=(batch // group,),
        in_specs=[spec(q.shape[0]), spec(k.shape[0]), spec(v.shape[0])],
        out_specs=pl.BlockSpec((N_PAIRS, rows, LANE), lambda b: (0, b, 0)),
        out_shape=jax.ShapeDtypeStruct((N_PAIRS, n, LANE), BF16),
        compiler_params=pltpu.CompilerParams(
            dimension_semantics=("arbitrary",), vmem_limit_bytes=VMEM_LIMIT),
        name=name,
    )(q, k, v)


def _attn_pipe_body(shared_kv, exp_scale, tq, pairs, q_ref, k_ref, v_ref, kc_ref, vc_ref, o_ref,
                    s0, s1, mr0, mr1, ac0, ac1):
    seq = k_ref.shape[1]
    past = kc_ref.shape[1]
    nq = seq // tq
    n_units = pairs * nq
    n_new = seq // KEY_TILE
    n_tiles = n_new + past // KEY_TILE
    s_scr = (s0, s1)
    mrun_scr = (mr0, mr1)
    acc_scr = (ac0, ac1)
    nt_dims = (((1,), (1,)), ((), ()))
    lo = lax.broadcasted_iota(jnp.int32, (tq, LANE), 1) < MLA_V

    def pair_rows(t):
        j = t % nq
        return t // nq, pl.ds(pl.multiple_of(j * tq, tq), tq)

    def kv_tile(new_ref, cache_ref, head, blk):
        if blk < n_new:
            return new_ref[head, blk * KEY_TILE:(blk + 1) * KEY_TILE, :]
        blk -= n_new
        return cache_ref[head, blk * KEY_TILE:(blk + 1) * KEY_TILE, :]

    def finish_prev(e_prev, t_prev):
        if e_prev == 1:
            pair, rows = pair_rows(t_prev)
            o_ref[pair, rows, :] = _normalise_pair(ac0[...], ac1[...], lo).astype(BF16)

    def region(t_scores, e_scores, t_cur, e_cur, t_prev):
        if t_prev is not None:
            finish_prev(1 - e_cur, t_prev)
        if e_scores is not None:
            pair, rows = pair_rows(t_scores)
            q = q_ref[2 * pair + e_scores, rows, :]
            k_head = 0 if shared_kv else 2 * pair + e_scores
        if e_cur is not None:
            v_head = e_cur if shared_kv else 2 * (t_cur // nq) + e_cur
            m = jnp.max(mrun_scr[e_cur][...], axis=-1, keepdims=True)
            m_b = jnp.broadcast_to(m, (tq, LANE))
        acc = None
        for blk in range(n_tiles):
            cols = slice(blk * KEY_TILE, (blk + 1) * KEY_TILE)
            if e_scores is not None:
                s_t = lax.dot_general(q, kv_tile(k_ref, kc_ref, k_head, blk), nt_dims,
                                      preferred_element_type=F32)
                s_scr[e_scores][:, cols] = s_t
                m_t = jnp.maximum(s_t[:, 0:LANE], s_t[:, LANE:2 * LANE])
                if blk > 0:
                    m_t = jnp.maximum(m_t, mrun_scr[e_scores][...])
                mrun_scr[e_scores][...] = m_t
            if e_cur is not None:
                p_parts = []
                for hh in range(KEY_TILE // LANE):
                    c0 = blk * KEY_TILE + hh * LANE
                    s_h = s_scr[e_cur][:, c0:c0 + LANE]
                    p_parts.append(jnp.exp2((s_h - m_b) * exp_scale).astype(BF16))
                p_t = jnp.concatenate(p_parts, axis=1)
                part = _dot(p_t, kv_tile(v_ref, vc_ref, v_head, blk))
                acc = part if acc is None else acc + part
        if e_cur is not None:
            acc_scr[e_cur][...] = acc

    region(0, 0, None, None, None)
    region(0, 1, 0, 0, None)
    region(1, 0, 0, 1, 0)

    def body(t, carry):
        region(t, 1, t, 0, t - 1)
        region(t + 1, 0, t, 1, t)
        return carry

    lax.fori_loop(1, n_units - 1, body, 0)
    region(n_units - 1, 1, n_units - 1, 0, n_units - 2)
    region(None, None, n_units - 1, 1, n_units - 1)
    finish_prev(1, n_units - 1)


def _attn_pipe_call(q, k, v, kc, vc, batch, seq_len, tq, scale, shared_kv, name, pairs=LAT_PAIRS):
    n = q.shape[1]
    past = kc.shape[1] // batch
    exp_scale = scale * math.log2(math.e)

    def spec(a, rows):
        heads = a.shape[0]
        if heads < 2 * N_PAIRS:
            return pl.BlockSpec((heads, rows, LANE), lambda b, p: (0, b, 0))
        return pl.BlockSpec((2 * pairs, rows, LANE), lambda b, p: (p, b, 0))

    total = seq_len + past
    return pl.pallas_call(
        functools.partial(_attn_pipe_body, shared_kv, exp_scale, tq, pairs),
        grid=(batch, N_PAIRS // pairs),
        in_specs=[spec(q, seq_len), spec(k, seq_len), spec(v, seq_len), spec(kc, past), spec(vc, past)],
        out_specs=pl.BlockSpec((pairs, seq_len, LANE), lambda b, p: (p, b, 0)),
        out_shape=jax.ShapeDtypeStruct((N_PAIRS, n, LANE), BF16),
        scratch_shapes=[pltpu.VMEM((tq, total), F32), pltpu.VMEM((tq, total), F32)]
        + [pltpu.VMEM((tq, LANE), F32) for _ in range(4)],
        compiler_params=pltpu.CompilerParams(
            dimension_semantics=("arbitrary", "arbitrary"), vmem_limit_bytes=VMEM_LIMIT),
        name=name,
    )(q, k, v, kc, vc)


def _seq_pos(tm, seq_len):
    i = pl.program_id(0)
    row = lax.broadcasted_iota(jnp.int32, (tm, 1), 0) + i * tm
    return jnp.bitwise_and(row, seq_len - 1)


def _merge_body(seq_len, x_ref, mod_ref, gmix_ref, wg_ref, a_ref, b_ref, pc_ref, pl_ref, pr_ref,
                wpool_ref, pscale_ref, wa_ref, wb_ref, wc_ref, wo_ref, o_ref, merged_scr, ab_scr):
    x = x_ref[...]
    tm = x.shape[0]
    mod = mod_ref[0]
    sh1 = mod[:, 0:D_MODEL]
    sc1 = mod[:, D_MODEL:2 * D_MODEL]
    g1 = mod[:, 2 * D_MODEL:3 * D_MODEL]
    h = (_rms(x, gmix_ref[...]) * (1.0 + sc1) + sh1).astype(BF16)

    pos = _seq_pos(tm, seq_len)
    pscale = pscale_ref[...]
    centre_rows = pc_ref[...]
    if seq_len >= tm:
        start = (pl.program_id(0) * tm) % seq_len
        left = jnp.where(start == 0, 0.0, pl_ref[...])
        right = jnp.where(start + tm == seq_len, 0.0, pr_ref[...])
        blocks = [jnp.concatenate([left, centre_rows, right], axis=0)]
    else:
        zeros = jnp.zeros((HALO, POOL_WIDTH), F32)
        blocks = [jnp.concatenate([zeros, centre_rows[s * seq_len:(s + 1) * seq_len], zeros], axis=0)
                  for s in range(tm // seq_len)]

    def pool_group(g):
        w = POOL_WINDOWS[g]
        gcols = slice(g * POOL_GROUP, (g + 1) * POOL_GROUP)
        sums = []
        for blk in blocks:
            rows = blk.shape[0] - 2 * HALO
            eg = blk[:, gcols]
            wsum = eg[HALO:HALO + rows]
            for dlt in range(-(w // 2), w - w // 2):
                if dlt != 0:
                    wsum = wsum + eg[HALO + dlt:HALO + dlt + rows]
            sums.append(wsum)
        wsum = sums[0] if len(sums) == 1 else jnp.concatenate(sums, axis=0)
        cnt = jnp.minimum(pos + (w - w // 2), seq_len) - jnp.maximum(pos - w // 2, 0)
        return (wsum / cnt.astype(F32) - centre_rows[:, gcols]).astype(BF16)

    def pool_pair_map(p, pooled_a, pooled_b):
        both = jnp.concatenate([pooled_a, pooled_b], axis=1)
        return _dot(both, wpool_ref[p]) * pscale[:, 2 * p * POOL_GROUP:2 * (p + 1) * POOL_GROUP]

    def gated(branch, src, w_ref, j):
        c0 = branch * D_MODEL + j * FF_CHUNK
        gate = jax.nn.sigmoid(_dot(h, wg_ref[:, c0:c0 + FF_CHUNK]))
        return gate * _dot(src, w_ref[:, j * FF_CHUNK:(j + 1) * FF_CHUNK])

    a_out = jnp.concatenate([a_ref[p] for p in range(N_PAIRS)], axis=1)
    b_out = jnp.concatenate([b_ref[p] for p in range(N_PAIRS)], axis=1)
    n_chunks = D_MODEL // FF_CHUNK
    pooled = []
    c_parts = []
    for j in range(n_chunks):
        cols = slice(j * FF_CHUNK, (j + 1) * FF_CHUNK)
        for g in range(j * len(POOL_WINDOWS) // n_chunks, (j + 1) * len(POOL_WINDOWS) // n_chunks):
            pooled.append(pool_group(g))
            if g % 2 == 1:
                c_parts.append(pool_pair_map(g // 2, pooled[g - 1], pooled[g]))
        ab_scr[:, cols] = gated(0, a_out, wa_ref, j) + gated(1, b_out, wb_ref, j)
    c_out = jnp.concatenate(c_parts, axis=-1).astype(BF16)
    for j in range(n_chunks):
        cols = slice(j * FF_CHUNK, (j + 1) * FF_CHUNK)
        merged_scr[:, cols] = (ab_scr[:, cols] + gated(2, c_out, wc_ref, j)).astype(BF16)
    o_ref[...] = x + g1 * _dot(merged_scr[...], wo_ref[...])


def _halo_specs(tm, width, n):
    blocks = tm // HALO
    last = n // HALO - 1
    left = pl.BlockSpec((HALO, width), lambda i: (jnp.maximum(i * blocks - 1, 0), 0))
    right = pl.BlockSpec((HALO, width), lambda i: (jnp.minimum((i + 1) * blocks, last), 0))
    return left, right


def _mod_spec(layer, mod_base, per_seq, latent):
    if latent:
        return pl.BlockSpec((None, 1, 1, 6 * D_MODEL), lambda i: (layer, mod_base + i // per_seq, 0, 0))
    return pl.BlockSpec((None, 1, 1, 6 * D_MODEL), lambda i: (layer, mod_base, 0, 0))


def _merge_call(x, mod4, layer, mod_base, seq_len, tm, latent, a_out, b_out, pool_in, wl):
    n = x.shape[0]
    per_seq = seq_len // tm if latent else 1

    def tile(i):
        return (i, 0)

    left, right = _halo_specs(tm, POOL_WIDTH, n)
    consts = [wl["wpool"], wl["pscale"], wl["wa"], wl["wb"], wl["wc"], wl["wo"]]
    in_specs = [pl.BlockSpec((tm, D_MODEL), tile), _mod_spec(layer, mod_base, per_seq, latent),
                _wspec(wl["gmix"], layer), _wspec(wl["wg"], layer),
                pl.BlockSpec((N_PAIRS, tm, LANE), lambda i: (0, i, 0)),
                pl.BlockSpec((N_PAIRS, tm, LANE), lambda i: (0, i, 0)),
                pl.BlockSpec((tm, POOL_WIDTH), tile), left, right]
    in_specs += [_wspec(a, layer) for a in consts]
    return pl.pallas_call(
        functools.partial(_merge_body, seq_len),
        grid=(n // tm,),
        in_specs=in_specs,
        out_specs=pl.BlockSpec((tm, D_MODEL), tile),
        out_shape=jax.ShapeDtypeStruct((n, D_MODEL), F32),
        scratch_shapes=[pltpu.VMEM((tm, D_MODEL), BF16), pltpu.VMEM((tm, D_MODEL), F32)],
        compiler_params=pltpu.CompilerParams(
            dimension_semantics=("arbitrary",), vmem_limit_bytes=VMEM_LIMIT),
        name="merge_lat" if latent else "merge_ctx",
    )(x, mod4, wl["gmix"], wl["wg"], a_out, b_out, pool_in, pool_in, pool_in, *consts)


def _ffn_body(seq_len, final, x_ref, xl_ref, xr_ref, mod_ref, gffn_ref, wug_ref, wuv_ref, cw_ref, cb_ref,
              wd_ref, gfin_ref, o_ref, act_scr):
    x = x_ref[...]
    tm = x.shape[0]
    mod = mod_ref[0]
    sh2 = mod[:, 3 * D_MODEL:4 * D_MODEL]
    sc2 = mod[:, 4 * D_MODEL:5 * D_MODEL]
    g2 = mod[:, 5 * D_MODEL:6 * D_MODEL]
    xe = jnp.concatenate([xl_ref[...], x, xr_ref[...]], axis=0)
    h2e = _rms(xe, gffn_ref[...]) * (1.0 + sc2) + sh2
    h2 = h2e[HALO:HALO + tm].astype(BF16)
    h2e = h2e.astype(BF16)

    cw = cw_ref[...]
    cb = cb_ref[...]
    if seq_len >= tm:
        start = (pl.program_id(0) * tm) % seq_len
        at_start = start == 0
        at_end = start + tm == seq_len

    def neighbours(ge):
        if seq_len >= tm:
            blocks = [jnp.concatenate([jnp.where(at_start, 0.0, ge[0:HALO]), ge[HALO:HALO + tm],
                                       jnp.where(at_end, 0.0, ge[HALO + tm:])], axis=0)]
        else:
            zeros = jnp.zeros((HALO, ge.shape[1]), F32)
            blocks = [jnp.concatenate([zeros, ge[HALO + s * seq_len:HALO + (s + 1) * seq_len], zeros], axis=0)
                      for s in range(tm // seq_len)]
        prev = [b[HALO - 1:b.shape[0] - HALO - 1] for b in blocks]
        nxt = [b[HALO + 1:b.shape[0] - HALO + 1] for b in blocks]
        cat = lambda parts: parts[0] if len(parts) == 1 else jnp.concatenate(parts, axis=0)
        return cat(prev), cat(nxt)

    for j in range(D_FF // FF_CHUNK):
        cols = slice(j * FF_CHUNK, (j + 1) * FF_CHUNK)
        ge = _dot(h2e, wug_ref[:, cols])
        g_prev, g_next = neighbours(ge)
        g = (g_prev * cw[0:1, cols] + ge[HALO:HALO + tm] * cw[1:2, cols] + g_next * cw[2:3, cols]
             + cb[:, cols])
        val = _dot(h2, wuv_ref[:, cols])
        act_scr[:, cols] = (g * jax.nn.sigmoid(g) * val).astype(BF16)
    y = x + g2 * _dot(act_scr[...], wd_ref[...])
    if final:
        y = _rms(y, gfin_ref[...])
    o_ref[...] = y


def _ffn_call(x, mod4, layer, mod_base, seq_len, tm, latent, final, wl, g_final):
    n = x.shape[0]
    per_seq = seq_len // tm if latent else 1

    def tile(i):
        return (i, 0)

    left, right = _halo_specs(tm, D_MODEL, n)
    consts = [wl["gffn"], wl["wu"], wl["wu"], wl["cw"], wl["cb"], wl["wd"], g_final]
    in_specs = [pl.BlockSpec((tm, D_MODEL), tile), left, right, _mod_spec(layer, mod_base, per_seq, latent),
                _wspec(wl["gffn"], layer), _wspec(wl["wu"], layer, cols=D_FF, col_block=0),
                _wspec(wl["wu"], layer, cols=D_FF, col_block=1), _wspec(wl["cw"], layer),
                _wspec(wl["cb"], layer), _wspec(wl["wd"], layer), _const_spec(g_final.shape)]
    return pl.pallas_call(
        functools.partial(_ffn_body, seq_len, final),
        grid=(n // tm,),
        in_specs=in_specs,
        out_specs=pl.BlockSpec((tm, D_MODEL), tile),
        out_shape=jax.ShapeDtypeStruct((n, D_MODEL), F32),
        scratch_shapes=[pltpu.VMEM((tm, D_FF), BF16)],
        compiler_params=pltpu.CompilerParams(
            dimension_semantics=("arbitrary",), vmem_limit_bytes=VMEM_LIMIT),
        name="ffn_lat" if latent else "ffn_ctx",
    )(x, x, x, mod4, *consts)


def _rot_partner(a):
    d = a.shape[-1]
    return jnp.flip(a.reshape(a.shape[:-1] + (2, 2, d // 4)), axis=-2).reshape(a.shape)


def _prep_weights(g_norm_mix, w_in, g_q_a, w_q_b, g_kv_a, w_kv_b, g_q_gqa, g_k_gqa, w_pool, pool_scale,
                  w_br_a, w_br_b, w_br_c, w_out, g_norm_ffn, w_up, conv_w, conv_b, w_down):
    depth = w_in.shape[0]
    hd = GQA_HEAD_DIM
    grp = GQA_HEADS // GQA_KV_HEADS

    def lane_pad(a, before, width=LANE):
        cfg = [(0, 0)] * (a.ndim - 1) + [(before, width - before - a.shape[-1])]
        return jnp.pad(a, cfg)

    def pair_heads(a):
        lead = a.shape[:-1]
        return a.reshape(lead + (GQA_KV_HEADS, grp, hd)).swapaxes(-3, -2).reshape(lead + (GQA_HEADS * hd,))

    w1 = jnp.concatenate(
        [w_in[:, :, _OFF_QA:_OFF_KR], pair_heads(w_in[:, :, _OFF_GQ:_OFF_GK]), w_in[:, :, _OFF_GK:_OFF_GATE],
         lane_pad(w_in[:, :, _OFF_KR:_OFF_GQ], MLA_NOPE)], axis=-1).astype(BF16)
    assert w1.shape[-1] == _W1_COLS

    qb = w_q_b.reshape(depth, Q_LORA, MLA_HEADS, MLA_NOPE + MLA_ROPE)
    wqb = lane_pad(qb, 0).reshape(depth, Q_LORA, MLA_HEADS * LANE).astype(BF16)
    wqbsw = lane_pad(_rot_partner(qb[..., MLA_NOPE:]), MLA_NOPE).reshape(
        depth, Q_LORA, MLA_HEADS * LANE).astype(BF16)

    kvb = w_kv_b.reshape(depth, KV_LORA, MLA_HEADS, MLA_NOPE + MLA_V)
    wk = lane_pad(kvb[..., :MLA_NOPE], 0).reshape(depth, KV_LORA, MLA_HEADS * LANE).astype(BF16)
    wv = kvb[..., MLA_NOPE:].reshape(depth, KV_LORA, MLA_HEADS * MLA_V).astype(BF16)

    pk = np.zeros((LANE, MLA_HEADS * LANE), np.float32)
    for hh in range(MLA_HEADS):
        for t in range(MLA_ROPE):
            pk[MLA_NOPE + t, hh * LANE + MLA_NOPE + t] = 1.0
    wkp = jnp.concatenate([wk, jnp.broadcast_to(jnp.asarray(pk, BF16), (depth,) + pk.shape)], axis=1)

    wp = w_pool.reshape(depth, len(POOL_WINDOWS) // 2, 2, POOL_GROUP, POOL_GROUP)
    zero = jnp.zeros_like(wp[:, :, 0])
    wpool2 = jnp.concatenate([jnp.concatenate([wp[:, :, 0], zero], axis=-1),
                              jnp.concatenate([zero, wp[:, :, 1]], axis=-1)], axis=-2).astype(BF16)

    row = lambda a: a[:, None, :]
    two = lambda a: jnp.concatenate([a, a], axis=-1)
    wb = w_br_b.reshape(depth, GQA_KV_HEADS, grp, hd, D_MODEL).swapaxes(1, 2).reshape(depth, GQA_HEADS * hd, D_MODEL)
    return dict(
        gmix=row(g_norm_mix), w1=w1, gqa=row(g_q_a), wqb=wqb, wqbsw=wqbsw, gkva=row(g_kv_a), wkp=wkp,
        wv=wv, gq2=row(two(g_q_gqa)), gk2=row(two(g_k_gqa)),
        wg=w_in[:, :, _OFF_GATE:].astype(BF16), wpool=wpool2, pscale=row(pool_scale),
        wa=w_br_a.astype(BF16), wb=wb.astype(BF16), wc=w_br_c.astype(BF16), wo=w_out.astype(BF16),
        gffn=row(g_norm_ffn), wu=w_up.astype(BF16), cw=conv_w, cb=row(conv_b), wd=w_down.astype(BF16),
    )


def _layer(x, mod4, layer, mod_base, batch, seq_len, latent, final, wl, g_final, tabs, caches):
    tm = TOKEN_TILE
    tq = Q_TILE
    assert tm % seq_len == 0 or seq_len % tm == 0
    pj = _inproj_call(x, mod4, layer, mod_base, seq_len, tm, latent, wl, tabs)
    if caches is not None:
        kc_m, vc_m, kc_g, vc_g = _cachekv_call(caches, layer, wl)
    mla_scale = (MLA_NOPE + MLA_ROPE) ** -0.5
    n = x.shape[0]
    k_g = pj["k_gqa"].reshape(1, n, LANE)
    if latent:
        a_out = _attn_pipe_call(pj["q_mla"], pj["k_mla"], pj["v_mla"], kc_m, vc_m, batch, seq_len, tq,
                                mla_scale, False, "attn_mla_lat")
        b_out = _attn_pipe_call(pj["q_gqa"], k_g, pj["v_gqa"], kc_g, vc_g, batch, seq_len, tq, 1.0, True,
                                "attn_gqa_lat")
    else:
        a_out = _attn_call(pj["q_mla"], pj["k_mla"], pj["v_mla"], batch, seq_len, CTX_GROUP,
                           mla_scale, False, "attn_mla_ctx")
        b_out = _attn_call(pj["q_gqa"], k_g, pj["v_gqa"], batch, seq_len, CTX_GROUP, 1.0, True,
                           "attn_gqa_ctx")
    x1 = _merge_call(x, mod4, layer, mod_base, seq_len, tm, latent, a_out, b_out, pj["pool"], wl)
    x2 = _ffn_call(x1, mod4, layer, mod_base, seq_len, tm, latent, final, wl, g_final)
    return x2, pj


def kernel(x_prompt, x_sample, c, cache_mla_ckv, cache_mla_krope, cache_gqa_k, cache_gqa_v, c_ctx, w_ada, b_ada, g_norm_mix, w_in, g_q_a, w_q_b, g_kv_a, w_kv_b, g_q_gqa, g_k_gqa, w_pool, pool_scale, w_br_a, w_br_b, w_br_c, w_out, g_norm_ffn, w_up, conv_w, conv_b, w_down, g_final):
    depth = w_in.shape[0]
    bc, tc, _ = x_prompt.shape
    bl, tl, _ = x_sample.shape
    past = cache_mla_ckv.shape[2]
    mod_rows = 8
    assert 1 + bl <= mod_rows and tc & (tc - 1) == 0 and tl & (tl - 1) == 0

    cin = jnp.concatenate([c_ctx[None, :], c, jnp.zeros((mod_rows - 1 - bl, D_MODEL), F32)], axis=0)
    mod4 = _ada_call(cin, w_ada, b_ada).reshape(depth, mod_rows, 1, 6 * D_MODEL)
    tabs = _rope_tables(tl)
    g_fin = g_final.reshape(1, D_MODEL)
    wl = _prep_weights(g_norm_mix, w_in, g_q_a, w_q_b, g_kv_a, w_kv_b, g_q_gqa, g_k_gqa, w_pool,
                       pool_scale, w_br_a, w_br_b, w_br_c, w_out, g_norm_ffn, w_up, conv_w, conv_b, w_down)
    kr_pad = jnp.pad(cache_mla_krope, ((0, 0), (0, 0), (0, 0), (MLA_NOPE, LANE - MLA_NOPE - MLA_ROPE)))
    caches = (cache_mla_ckv, kr_pad, cache_gqa_k.reshape(bl, depth, past, LANE),
              cache_gqa_v.reshape(bl, depth, past, LANE))

    xc = x_prompt.reshape(bc * tc, D_MODEL)
    xl = x_sample.reshape(bl * tl, D_MODEL)
    st_ckv, st_kr, st_k, st_v = [], [], [], []
    for l in range(depth):
        final = l == depth - 1
        xc, pj = _layer(xc, mod4, l, 0, bc, tc, False, final, wl, g_fin, None, None)
        st_ckv.append(pj["ckv_out"].reshape(bc, tc, KV_LORA))
        st_kr.append(pj["kr_out"].reshape(bc, tc, MLA_ROPE))
        st_k.append(pj["kg_out"].reshape(bc, tc, GQA_KV_HEADS, GQA_HEAD_DIM))
        st_v.append(pj["vg_out"].reshape(bc, tc, GQA_KV_HEADS, GQA_HEAD_DIM))
        xl, _ = _layer(xl, mod4, l, 1, bl, tl, True, final, wl, g_fin, tabs, caches)

    return (xc.reshape(bc, tc, D_MODEL), xl.reshape(bl, tl, D_MODEL),
            jnp.stack(st_ckv, axis=1), jnp.stack(st_kr, axis=1),
            jnp.stack(st_k, axis=1), jnp.stack(st_v, axis=1))
```

```python
import functools
import math

import numpy as np
import jax
import jax.numpy as jnp
from jax import lax
from jax.experimental import pallas as pl
from jax.experimental.pallas import tpu as pltpu

D_MODEL = 1024
GRID_W = 64
RMS_EPS = 1e-6
ROPE_THETA = 10000.0
MLA_HEADS = 8
MLA_NOPE = 64
MLA_ROPE = 32
MLA_V = 64
Q_LORA = 256
KV_LORA = 128
GQA_HEADS = 8
GQA_KV_HEADS = 2
GQA_HEAD_DIM = 64
POOL_WINDOWS = (2, 4, 8, 16)
POOL_GROUP = 128
POOL_WIDTH = POOL_GROUP * len(POOL_WINDOWS)
D_FF = 2816

LANE = 128
HALO = 8
FF_CHUNK = 256
KEY_TILE = 256
CTX_GROUP = 8
CTX_Q_ROWS = 256
TOKEN_TILE = 512
Q_TILE = 512
LAT_PAIRS = 2
N_PAIRS = MLA_HEADS // 2
VMEM_LIMIT = 56 * 1024 * 1024

BF16 = jnp.bfloat16
F32 = jnp.float32

_OFF_QA = 0
_OFF_CKV = _OFF_QA + Q_LORA
_OFF_KR = _OFF_CKV + KV_LORA
_OFF_GQ = _OFF_KR + MLA_ROPE
_OFF_GK = _OFF_GQ + GQA_HEADS * GQA_HEAD_DIM
_OFF_GV = _OFF_GK + GQA_KV_HEADS * GQA_HEAD_DIM
_OFF_POOL = _OFF_GV + GQA_KV_HEADS * GQA_HEAD_DIM
_OFF_GATE = _OFF_POOL + POOL_WIDTH

_W1_QA = 0
_W1_CKV = 256
_W1_GQ = 384
_W1_GK = 896
_W1_GV = 1024
_W1_POOL = 1152
_W1_KR = 1664
_W1_COLS = 1792


def _const_spec(shape):
    nd = len(shape)
    return pl.BlockSpec(shape, lambda *_: (0,) * nd, pipeline_mode=pl.Buffered(1))


def _wspec(a, layer, cols=None, col_block=0):
    shape = a.shape[1:] if cols is None else a.shape[1:-1] + (cols,)
    idx = (layer,) + (0,) * (len(shape) - 1) + (col_block,)
    return pl.BlockSpec((None,) + shape, lambda *_: idx, pipeline_mode=pl.Buffered(1))


def _rms(x, g):
    ms = jnp.mean(x * x, axis=-1, keepdims=True)
    return x * lax.rsqrt(ms + RMS_EPS) * g


def _dot(a, b):
    return jnp.dot(a, b, preferred_element_type=F32)


def _rope_head_tables(n_tokens, d):
    dim_axis = d // 2
    t = np.arange(n_tokens)
    row = (t // GRID_W).astype(np.float64)
    col = (t % GRID_W).astype(np.float64)
    freqs = ROPE_THETA ** (-np.arange(0, dim_axis, 2, dtype=np.float64) / dim_axis)
    ar = row[:, None] * freqs[None, :]
    ac = col[:, None] * freqs[None, :]
    cos = np.concatenate([np.cos(ar), np.cos(ar), np.cos(ac), np.cos(ac)], axis=1)
    sin = np.concatenate([-np.sin(ar), np.sin(ar), -np.sin(ac), np.sin(ac)], axis=1)
    return cos, sin


def _rope_tables(n_tokens):
    cg, sg = _rope_head_tables(n_tokens, GQA_HEAD_DIM)
    cg = np.concatenate([cg, cg], axis=1)
    sg = np.concatenate([sg, sg], axis=1)
    cm32, sm32 = _rope_head_tables(n_tokens, MLA_ROPE)
    pad = LANE - MLA_NOPE - MLA_ROPE
    cm = np.concatenate([np.ones((n_tokens, MLA_NOPE)), cm32, np.zeros((n_tokens, pad))], axis=1)
    sm = np.concatenate([np.zeros((n_tokens, MLA_NOPE)), sm32, np.zeros((n_tokens, pad))], axis=1)
    return tuple(jnp.asarray(a, F32) for a in (cg, sg, cm, sm))


def _ada_body(c_ref, w_ref, b_ref, o_ref):
    c = c_ref[...]
    s = (c * jax.nn.sigmoid(c)).astype(BF16)
    o_ref[0] = _dot(s, w_ref[0].astype(BF16)) + b_ref[0]


def _ada_call(cin, w_ada, b_ada):
    depth, d, n = w_ada.shape
    cols = 2048
    rows = cin.shape[0]
    return pl.pallas_call(
        _ada_body,
        grid=(depth, n // cols),
        in_specs=[
            pl.BlockSpec((rows, d), lambda l, j: (0, 0)),
            pl.BlockSpec((1, d, cols), lambda l, j: (l, 0, j)),
            pl.BlockSpec((1, 1, cols), lambda l, j: (l, 0, j)),
        ],
        out_specs=pl.BlockSpec((1, rows, cols), lambda l, j: (l, 0, j)),
        out_shape=jax.ShapeDtypeStruct((depth, rows, n), F32),
        compiler_params=pltpu.CompilerParams(
            dimension_semantics=("arbitrary", "arbitrary"), vmem_limit_bytes=VMEM_LIMIT),
        name="ada_mod",
    )(cin, w_ada, b_ada.reshape(depth, 1, n))


def _two_head_rsqrt(xb, lo):
    x2 = xb * xb
    s_lo = jnp.sum(jnp.where(lo, x2, 0.0), axis=-1, keepdims=True)
    s_hi = jnp.sum(jnp.where(lo, 0.0, x2), axis=-1, keepdims=True)
    ms = jnp.where(lo, s_lo, s_hi) * (1.0 / GQA_HEAD_DIM)
    return lax.rsqrt(ms + RMS_EPS)


def _rot_partner_lanes(xb, lane, quarter):
    first = jnp.bitwise_and(lane, quarter) == 0
    return jnp.where(first, pltpu.roll(xb, LANE - quarter, axis=1), pltpu.roll(xb, quarter, axis=1))


def _store_value_heads(v_ref, first, blk, lo):
    v_ref[first] = jnp.where(lo, blk, 1.0).astype(BF16)
    v_ref[first + 1] = jnp.where(lo, 1.0, blk).astype(BF16)


def _inproj_body(latent, names, *refs):
    r = dict(zip(names, refs))
    x = r["x"][...]
    tm = x.shape[0]
    mod = r["mod"][0]
    sh1 = mod[:, 0:D_MODEL]
    sc1 = mod[:, D_MODEL:2 * D_MODEL]
    h = (_rms(x, r["gmix"][...]) * (1.0 + sc1) + sh1).astype(BF16)
    z = _dot(h, r["w1"][...])

    lane = lax.broadcasted_iota(jnp.int32, (tm, LANE), 1)
    lo = lane < GQA_HEAD_DIM

    if latent:
        cg = r["cg"][...]
        sg = r["sg"][...]
        cm = r["cm"][...]
        sm = r["sm"][...]

    qn = _rms(z[:, _W1_QA:_W1_QA + Q_LORA], r["gqa"][...]).astype(BF16)
    qm = _dot(qn, r["wqb"][...])
    if latent:
        qms = _dot(qn, r["wqbsw"][...])
    for hh in range(MLA_HEADS):
        blk = qm[:, hh * LANE:(hh + 1) * LANE]
        if latent:
            blk = blk * cm + qms[:, hh * LANE:(hh + 1) * LANE] * sm
        r["q_mla"][hh] = blk.astype(BF16)

    ckv = _rms(z[:, _W1_CKV:_W1_CKV + KV_LORA], r["gkva"][...])
    kr = z[:, _W1_KR:_W1_KR + LANE]
    if latent:
        kr = kr * cm + _rot_partner_lanes(kr, lane, MLA_ROPE // 4) * sm
    else:
        r["ckv_out"][...] = ckv
        r["kr_out"][...] = kr[:, MLA_NOPE:MLA_NOPE + MLA_ROPE]
    ckv_b = ckv.astype(BF16)
    kfull = _dot(jnp.concatenate([ckv_b, kr.astype(BF16)], axis=1), r["wkp"][...])
    for hh in range(MLA_HEADS):
        r["k_mla"][hh] = kfull[:, hh * LANE:(hh + 1) * LANE].astype(BF16)
    vfull = _dot(ckv_b, r["wv"][...])
    for pp in range(N_PAIRS):
        _store_value_heads(r["v_mla"], 2 * pp, vfull[:, pp * LANE:(pp + 1) * LANE], lo)

    gq2 = r["gq2"][...]
    q_scale = GQA_HEAD_DIM ** -0.5
    for j in range(N_PAIRS):
        xb = z[:, _W1_GQ + j * LANE:_W1_GQ + (j + 1) * LANE]
        rs = _two_head_rsqrt(xb, lo)
        y = xb * rs * gq2
        if latent:
            y = y * cg + _rot_partner_lanes(y, lane, GQA_HEAD_DIM // 4) * sg
        y = y * q_scale
        r["q_gqa"][2 * j] = jnp.where(lo, y, 0.0).astype(BF16)
        r["q_gqa"][2 * j + 1] = jnp.where(lo, 0.0, y).astype(BF16)
    kb = z[:, _W1_GK:_W1_GK + LANE]
    rs = _two_head_rsqrt(kb, lo)
    kg = kb * rs * r["gk2"][...]
    if latent:
        kg = kg * cg + _rot_partner_lanes(kg, lane, GQA_HEAD_DIM // 4) * sg
    vg = z[:, _W1_GV:_W1_GV + LANE]
    if not latent:
        r["kg_out"][...] = kg
        r["vg_out"][...] = vg
    r["k_gqa"][...] = kg.astype(BF16)
    _store_value_heads(r["v_gqa"], 0, vg, lo)

    r["pool"][...] = z[:, _W1_POOL:_W1_POOL + POOL_WIDTH]


def _inproj_call(x, mod4, layer, mod_base, seq_len, tm, latent, wl, tabs):
    n = x.shape[0]
    nt = n // tm
    per_seq = seq_len // tm if latent else 1

    def tile(i):
        return (i, 0)

    def mod_idx(i):
        if latent:
            return (layer, mod_base + i // per_seq, 0, 0)
        return (layer, mod_base, 0, 0)

    names = ["x", "mod", "w1", "gmix", "gqa", "wqb", "gkva", "wkp", "wv", "gq2", "gk2"]
    args = [x, mod4, wl["w1"]] + [wl[nm] for nm in names[3:]]
    specs = [pl.BlockSpec((tm, D_MODEL), tile), pl.BlockSpec((None, 1, 1, 6 * D_MODEL), mod_idx),
             _wspec(wl["w1"], layer)]
    specs += [_wspec(a, layer) for a in args[3:]]
    if latent:
        names.append("wqbsw")
        args.append(wl["wqbsw"])
        specs.append(_wspec(wl["wqbsw"], layer))
        for nm, a in zip(("cg", "sg", "cm", "sm"), tabs):
            names.append(nm)
            args.append(a)
            specs.append(pl.BlockSpec((tm, LANE), lambda i: (i % per_seq, 0)))

    head_spec = pl.BlockSpec((MLA_HEADS, tm, LANE), lambda i: (0, i, 0))
    kvh_spec = pl.BlockSpec((GQA_KV_HEADS, tm, LANE), lambda i: (0, i, 0))
    out_names = ["q_mla", "k_mla", "v_mla", "q_gqa", "k_gqa", "v_gqa", "pool"]
    out_shapes = [
        jax.ShapeDtypeStruct((MLA_HEADS, n, LANE), BF16),
        jax.ShapeDtypeStruct((MLA_HEADS, n, LANE), BF16),
        jax.ShapeDtypeStruct((MLA_HEADS, n, LANE), BF16),
        jax.ShapeDtypeStruct((GQA_HEADS, n, LANE), BF16),
        jax.ShapeDtypeStruct((n, LANE), BF16),
        jax.ShapeDtypeStruct((GQA_KV_HEADS, n, LANE), BF16),
        jax.ShapeDtypeStruct((n, POOL_WIDTH), F32),
    ]
    out_specs = [head_spec, head_spec, head_spec, head_spec,
                 pl.BlockSpec((tm, LANE), tile), kvh_spec,
                 pl.BlockSpec((tm, POOL_WIDTH), tile)]
    if not latent:
        out_names += ["ckv_out", "kr_out", "kg_out", "vg_out"]
        out_shapes += [jax.ShapeDtypeStruct((n, KV_LORA), F32), jax.ShapeDtypeStruct((n, MLA_ROPE), F32),
                       jax.ShapeDtypeStruct((n, LANE), F32), jax.ShapeDtypeStruct((n, LANE), F32)]
        out_specs += [pl.BlockSpec((tm, KV_LORA), tile), pl.BlockSpec((tm, MLA_ROPE), tile),
                      pl.BlockSpec((tm, LANE), tile), pl.BlockSpec((tm, LANE), tile)]

    outs = pl.pallas_call(
        functools.partial(_inproj_body, latent, names + out_names),
        grid=(nt,),
        in_specs=specs,
        out_specs=out_specs,
        out_shape=out_shapes,
        compiler_params=pltpu.CompilerParams(
            dimension_semantics=("arbitrary",), vmem_limit_bytes=VMEM_LIMIT),
        name="inproj_lat" if latent else "inproj_ctx",
    )(*args)
    return dict(zip(out_names, outs))


def _cachekv_body(ckv_ref, kr_ref, kg_ref, vg_ref, wkp_ref, wv_ref, k_ref, v_ref, kgo_ref, vgo_ref):
    rows = ckv_ref.shape[0] * ckv_ref.shape[1]
    ckv_b = ckv_ref[...].reshape(rows, LANE).astype(BF16)
    lo = lax.broadcasted_iota(jnp.int32, (rows, LANE), 1) < MLA_V
    kr_b = kr_ref[...].reshape(rows, LANE).astype(BF16)
    kfull = _dot(jnp.concatenate([ckv_b, kr_b], axis=1), wkp_ref[...])
    kgo_ref[0] = kg_ref[...].reshape(rows, LANE).astype(BF16)
    for hh in range(MLA_HEADS):
        k_ref[hh] = kfull[:, hh * LANE:(hh + 1) * LANE].astype(BF16)
    vfull = _dot(ckv_b, wv_ref[...])
    for pp in range(N_PAIRS):
        _store_value_heads(v_ref, 2 * pp, vfull[:, pp * LANE:(pp + 1) * LANE], lo)
    _store_value_heads(vgo_ref, 0, vg_ref[...].reshape(rows, LANE), lo)


def _cachekv_call(caches, layer, wl):
    bsz, _, past, _ = caches[0].shape
    n = bsz * past
    cache_spec = pl.BlockSpec((bsz, None, past, LANE), lambda i: (0, layer, 0, 0))
    return pl.pallas_call(
        _cachekv_body,
        grid=(1,),
        in_specs=[cache_spec] * 4 + [_wspec(wl["wkp"], layer), _wspec(wl["wv"], layer)],
        out_specs=[_const_spec((MLA_HEADS, n, LANE)), _const_spec((MLA_HEADS, n, LANE)),
                   _const_spec((1, n, LANE)), _const_spec((GQA_KV_HEADS, n, LANE))],
        out_shape=[jax.ShapeDtypeStruct((MLA_HEADS, n, LANE), BF16),
                   jax.ShapeDtypeStruct((MLA_HEADS, n, LANE), BF16),
                   jax.ShapeDtypeStruct((1, n, LANE), BF16),
                   jax.ShapeDtypeStruct((GQA_KV_HEADS, n, LANE), BF16)],
        compiler_params=pltpu.CompilerParams(
            dimension_semantics=("arbitrary",), vmem_limit_bytes=VMEM_LIMIT),
        name="cache_kv",
    )(*caches, wl["wkp"], wl["wv"])


def _normalise_pair(acc_a, acc_b, lo):
    num = jnp.where(lo, acc_a, acc_b)
    den = pltpu.roll(jnp.where(lo, acc_b, acc_a), MLA_V, axis=1)
    return num / den


def _attn_body(group, seq_len, shared_kv, exp_scale, q_ref, k_ref, v_ref, o_ref):
    lo = lax.broadcasted_iota(jnp.int32, (CTX_Q_ROWS, LANE), 1) < MLA_V
    nt_dims = (((1,), (1,)), ((), ()))
    units = [(pr, qb, e) for pr in range(N_PAIRS) for qb in range(seq_len // CTX_Q_ROWS) for e in range(2)]

    def one_batch_row(g, carry):
        base = pl.multiple_of(g * seq_len, seq_len)
        rows = pl.ds(base, seq_len)

        scores, row_max, accs = {}, {}, {}
        for i in range(len(units) + 2):
            if i < len(units):
                pr, qb, e = units[i]
                qrows = pl.ds(base + qb * CTX_Q_ROWS, CTX_Q_ROWS)
                q = q_ref[2 * pr + e, qrows, :]
                k = k_ref[0 if shared_kv else 2 * pr + e, rows, :]
                scores[i] = lax.dot_general(q, k, nt_dims, preferred_element_type=F32)
            if 0 <= i - 1 < len(units):
                row_max[i - 1] = jnp.max(scores[i - 1], axis=-1, keepdims=True)
            if 0 <= i - 2 < len(units):
                u = i - 2
                pr, qb, e = units[u]
                qrows = pl.ds(base + qb * CTX_Q_ROWS, CTX_Q_ROWS)
                v = v_ref[e if shared_kv else 2 * pr + e, rows, :]
                p = jnp.exp2((scores.pop(u) - row_max.pop(u)) * exp_scale)
                accs[u] = _dot(p.astype(BF16), v)
                if e == 1:
                    o_ref[pr, qrows, :] = _normalise_pair(accs.pop(u - 1), accs.pop(u), lo).astype(BF16)
        return carry

    lax.fori_loop(0, group, one_batch_row, 0)


def _attn_call(q, k, v, batch, seq_len, group, scale, shared_kv, name):
    n = q.shape[1]
    rows = group * seq_len
    exp_scale = scale * math.log2(math.e)

    def spec(heads):
        return pl.BlockSpec((heads, rows, LANE), lambda b: (0, b, 0))

    return pl.pallas_call(
        functools.partial(_attn_body, group, seq_len, shared_kv, exp_scale),
        grid=(batch // group,),
        in_specs=[spec(q.shape[0]), spec(k.shape[0]), spec(v.shape[0])],
        out_specs=pl.BlockSpec((N_PAIRS, rows, LANE), lambda b: (0, b, 0)),
        out_shape=jax.ShapeDtypeStruct((N_PAIRS, n, LANE), BF16),
        compiler_params=pltpu.CompilerParams(
            dimension_semantics=("arbitrary",), vmem_limit_bytes=VMEM_LIMIT),
        name=name,
    )(q, k, v)


def _attn_pipe_body(shared_kv, exp_scale, tq, pairs, q_ref, k_ref, v_ref, kc_ref, vc_ref, o_ref,
                    s0, s1, mr0, mr1, ac0, ac1):
    seq = k_ref.shape[1]
    past = kc_ref.shape[1]
    nq = seq // tq
    n_units = pairs * nq
    n_new = seq // KEY_TILE
    n_tiles = n_new + past // KEY_TILE
    s_scr = (s0, s1)
    mrun_scr = (mr0, mr1)
    acc_scr = (ac0, ac1)
    nt_dims = (((1,), (1,)), ((), ()))
    lo = lax.broadcasted_iota(jnp.int32, (tq, LANE), 1) < MLA_V

    def pair_rows(t):
        j = t % nq
        return t // nq, pl.ds(pl.multiple_of(j * tq, tq), tq)

    def kv_tile(new_ref, cache_ref, head, blk):
        if blk < n_new:
            return new_ref[head, blk * KEY_TILE:(blk + 1) * KEY_TILE, :]
        blk -= n_new
        return cache_ref[head, blk * KEY_TILE:(blk + 1) * KEY_TILE, :]

    def finish_prev(e_prev, t_prev):
        if e_prev == 1:
            pair, rows = pair_rows(t_prev)
            o_ref[pair, rows, :] = _normalise_pair(ac0[...], ac1[...], lo).astype(BF16)

    def region(t_scores, e_scores, t_cur, e_cur, t_prev):
        if t_prev is not None:
            finish_prev(1 - e_cur, t_prev)
        if e_scores is not None:
            pair, rows = pair_rows(t_scores)
            q = q_ref[2 * pair + e_scores, rows, :]
            k_head = 0 if shared_kv else 2 * pair + e_scores
        if e_cur is not None:
            v_head = e_cur if shared_kv else 2 * (t_cur // nq) + e_cur
            m = jnp.max(mrun_scr[e_cur][...], axis=-1, keepdims=True)
            m_b = jnp.broadcast_to(m, (tq, LANE))
        acc = None
        for blk in range(n_tiles):
            cols = slice(blk * KEY_TILE, (blk + 1) * KEY_TILE)
            if e_scores is not None:
                s_t = lax.dot_general(q, kv_tile(k_ref, kc_ref, k_head, blk), nt_dims,
                                      preferred_element_type=F32)
                s_scr[e_scores][:, cols] = s_t
                m_t = jnp.maximum(s_t[:, 0:LANE], s_t[:, LANE:2 * LANE])
                if blk > 0:
                    m_t = jnp.maximum(m_t, mrun_scr[e_scores][...])
                mrun_scr[e_scores][...] = m_t
            if e_cur is not None:
                p_parts = []
                for hh in range(KEY_TILE // LANE):
                    c0 = blk * KEY_TILE + hh * LANE
                    s_h = s_scr[e_cur][:, c0:c0 + LANE]
                    p_parts.append(jnp.exp2((s_h - m_b) * exp_scale).astype(BF16))
                p_t = jnp.concatenate(p_parts, axis=1)
                part = _dot(p_t, kv_tile(v_ref, vc_ref, v_head, blk))
                acc = part if acc is None else acc + part
        if e_cur is not None:
            acc_scr[e_cur][...] = acc

    region(0, 0, None, None, None)
    region(0, 1, 0, 0, None)
    region(1, 0, 0, 1, 0)

    def body(t, carry):
        region(t, 1, t, 0, t - 1)
        region(t + 1, 0, t, 1, t)
        return carry

    lax.fori_loop(1, n_units - 1, body, 0)
    region(n_units - 1, 1, n_units - 1, 0, n_units - 2)
    region(None, None, n_units - 1, 1, n_units - 1)
    finish_prev(1, n_units - 1)


def _attn_pipe_call(q, k, v, kc, vc, batch, seq_len, tq, scale, shared_kv, name, pairs=LAT_PAIRS):
    n = q.shape[1]
    past = kc.shape[1] // batch
    exp_scale = scale * math.log2(math.e)

    def spec(a, rows):
        heads = a.shape[0]
        if heads < 2 * N_PAIRS:
            return pl.BlockSpec((heads, rows, LANE), lambda b, p: (0, b, 0))
        return pl.BlockSpec((2 * pairs, rows, LANE), lambda b, p: (p, b, 0))

    total = seq_len + past
    return pl.pallas_call(
        functools.partial(_attn_pipe_body, shared_kv, exp_scale, tq, pairs),
        grid=(batch, N_PAIRS // pairs),
        in_specs=[spec(q, seq_len), spec(k, seq_len), spec(v, seq_len), spec(kc, past), spec(vc, past)],
        out_specs=pl.BlockSpec((pairs, seq_len, LANE), lambda b, p: (p, b, 0)),
        out_shape=jax.ShapeDtypeStruct((N_PAIRS, n, LANE), BF16),
        scratch_shapes=[pltpu.VMEM((tq, total), F32), pltpu.VMEM((tq, total), F32)]
        + [pltpu.VMEM((tq, LANE), F32) for _ in range(4)],
        compiler_params=pltpu.CompilerParams(
            dimension_semantics=("arbitrary", "arbitrary"), vmem_limit_bytes=VMEM_LIMIT),
        name=name,
    )(q, k, v, kc, vc)


def _seq_pos(tm, seq_len):
    i = pl.program_id(0)
    row = lax.broadcasted_iota(jnp.int32, (tm, 1), 0) + i * tm
    return jnp.bitwise_and(row, seq_len - 1)


def _merge_body(seq_len, x_ref, mod_ref, gmix_ref, wg_ref, a_ref, b_ref, pc_ref, pl_ref, pr_ref,
                wpool_ref, pscale_ref, wa_ref, wb_ref, wc_ref, wo_ref, o_ref, merged_scr, ab_scr):
    x = x_ref[...]
    tm = x.shape[0]
    mod = mod_ref[0]
    sh1 = mod[:, 0:D_MODEL]
    sc1 = mod[:, D_MODEL:2 * D_MODEL]
    g1 = mod[:, 2 * D_MODEL:3 * D_MODEL]
    h = (_rms(x, gmix_ref[...]) * (1.0 + sc1) + sh1).astype(BF16)

    pos = _seq_pos(tm, seq_len)
    pscale = pscale_ref[...]
    centre_rows = pc_ref[...]
    if seq_len >= tm:
        start = (pl.program_id(0) * tm) % seq_len
        left = jnp.where(start == 0, 0.0, pl_ref[...])
        right = jnp.where(start + tm == seq_len, 0.0, pr_ref[...])
        blocks = [jnp.concatenate([left, centre_rows, right], axis=0)]
    else:
        zeros = jnp.zeros((HALO, POOL_WIDTH), F32)
        blocks = [jnp.concatenate([zeros, centre_rows[s * seq_len:(s + 1) * seq_len], zeros], axis=0)
                  for s in range(tm // seq_len)]

    def pool_group(g):
        w = POOL_WINDOWS[g]
        gcols = slice(g * POOL_GROUP, (g + 1) * POOL_GROUP)
        sums = []
        for blk in blocks:
            rows = blk.shape[0] - 2 * HALO
            eg = blk[:, gcols]
            wsum = eg[HALO:HALO + rows]
            for dlt in range(-(w // 2), w - w // 2):
                if dlt != 0:
                    wsum = wsum + eg[HALO + dlt:HALO + dlt + rows]
            sums.append(wsum)
        wsum = sums[0] if len(sums) == 1 else jnp.concatenate(sums, axis=0)
        cnt = jnp.minimum(pos + (w - w // 2), seq_len) - jnp.maximum(pos - w // 2, 0)
        return (wsum / cnt.astype(F32) - centre_rows[:, gcols]).astype(BF16)

    def pool_pair_map(p, pooled_a, pooled_b):
        both = jnp.concatenate([pooled_a, pooled_b], axis=1)
        return _dot(both, wpool_ref[p]) * pscale[:, 2 * p * POOL_GROUP:2 * (p + 1) * POOL_GROUP]

    def gated(branch, src, w_ref, j):
        c0 = branch * D_MODEL + j * FF_CHUNK
        gate = jax.nn.sigmoid(_dot(h, wg_ref[:, c0:c0 + FF_CHUNK]))
        return gate * _dot(src, w_ref[:, j * FF_CHUNK:(j + 1) * FF_CHUNK])

    a_out = jnp.concatenate([a_ref[p] for p in range(N_PAIRS)], axis=1)
    b_out = jnp.concatenate([b_ref[p] for p in range(N_PAIRS)], axis=1)
    n_chunks = D_MODEL // FF_CHUNK
    pooled = []
    c_parts = []
    for j in range(n_chunks):
        cols = slice(j * FF_CHUNK, (j + 1) * FF_CHUNK)
        for g in range(j * len(POOL_WINDOWS) // n_chunks, (j + 1) * len(POOL_WINDOWS) // n_chunks):
            pooled.append(pool_group(g))
            if g % 2 == 1:
                c_parts.append(pool_pair_map(g // 2, pooled[g - 1], pooled[g]))
        ab_scr[:, cols] = gated(0, a_out, wa_ref, j) + gated(1, b_out, wb_ref, j)
    c_out = jnp.concatenate(c_parts, axis=-1).astype(BF16)
    for j in range(n_chunks):
        cols = slice(j * FF_CHUNK, (j + 1) * FF_CHUNK)
        merged_scr[:, cols] = (ab_scr[:, cols] + gated(2, c_out, wc_ref, j)).astype(BF16)
    o_ref[...] = x + g1 * _dot(merged_scr[...], wo_ref[...])


def _halo_specs(tm, width, n):
    blocks = tm // HALO
    last = n // HALO - 1
    left = pl.BlockSpec((HALO, width), lambda i: (jnp.maximum(i * blocks - 1, 0), 0))
    right = pl.BlockSpec((HALO, width), lambda i: (jnp.minimum((i + 1) * blocks, last), 0))
    return left, right


def _mod_spec(layer, mod_base, per_seq, latent):
    if latent:
        return pl.BlockSpec((None, 1, 1, 6 * D_MODEL), lambda i: (layer, mod_base + i // per_seq, 0, 0))
    return pl.BlockSpec((None, 1, 1, 6 * D_MODEL), lambda i: (layer, mod_base, 0, 0))


def _merge_call(x, mod4, layer, mod_base, seq_len, tm, latent, a_out, b_out, pool_in, wl):
    n = x.shape[0]
    per_seq = seq_len // tm if latent else 1

    def tile(i):
        return (i, 0)

    left, right = _halo_specs(tm, POOL_WIDTH, n)
    consts = [wl["wpool"], wl["pscale"], wl["wa"], wl["wb"], wl["wc"], wl["wo"]]
    in_specs = [pl.BlockSpec((tm, D_MODEL), tile), _mod_spec(layer, mod_base, per_seq, latent),
                _wspec(wl["gmix"], layer), _wspec(wl["wg"], layer),
                pl.BlockSpec((N_PAIRS, tm, LANE), lambda i: (0, i, 0)),
                pl.BlockSpec((N_PAIRS, tm, LANE), lambda i: (0, i, 0)),
                pl.BlockSpec((tm, POOL_WIDTH), tile), left, right]
    in_specs += [_wspec(a, layer) for a in consts]
    return pl.pallas_call(
        functools.partial(_merge_body, seq_len),
        grid=(n // tm,),
        in_specs=in_specs,
        out_specs=pl.BlockSpec((tm, D_MODEL), tile),
        out_shape=jax.ShapeDtypeStruct((n, D_MODEL), F32),
        scratch_shapes=[pltpu.VMEM((tm, D_MODEL), BF16), pltpu.VMEM((tm, D_MODEL), F32)],
        compiler_params=pltpu.CompilerParams(
            dimension_semantics=("arbitrary",), vmem_limit_bytes=VMEM_LIMIT),
        name="merge_lat" if latent else "merge_ctx",
    )(x, mod4, wl["gmix"], wl["wg"], a_out, b_out, pool_in, pool_in, pool_in, *consts)


def _ffn_body(seq_len, final, x_ref, xl_ref, xr_ref, mod_ref, gffn_ref, wug_ref, wuv_ref, cw_ref, cb_ref,
              wd_ref, gfin_ref, o_ref, act_scr):
    x = x_ref[...]
    tm = x.shape[0]
    mod = mod_ref[0]
    sh2 = mod[:, 3 * D_MODEL:4 * D_MODEL]
    sc2 = mod[:, 4 * D_MODEL:5 * D_MODEL]
    g2 = mod[:, 5 * D_MODEL:6 * D_MODEL]
    xe = jnp.concatenate([xl_ref[...], x, xr_ref[...]], axis=0)
    h2e = _rms(xe, gffn_ref[...]) * (1.0 + sc2) + sh2
    h2 = h2e[HALO:HALO + tm].astype(BF16)
    h2e = h2e.astype(BF16)

    cw = cw_ref[...]
    cb = cb_ref[...]
    if seq_len >= tm:
        pos = _seq_pos(tm, seq_len)
        has_prev = pos >= 1
        has_next = pos <= seq_len - 2

    def neighbours(ge):
        if seq_len >= tm:
            return (jnp.where(has_prev, ge[HALO - 1:HALO - 1 + tm], 0.0),
                    jnp.where(has_next, ge[HALO + 1:HALO + 1 + tm], 0.0))
        zeros = jnp.zeros((HALO, ge.shape[1]), F32)
        blocks = [jnp.concatenate([zeros, ge[HALO + s * seq_len:HALO + (s + 1) * seq_len], zeros], axis=0)
                  for s in range(tm // seq_len)]
        prev = jnp.concatenate([b[HALO - 1:HALO - 1 + seq_len] for b in blocks], axis=0)
        nxt = jnp.concatenate([b[HALO + 1:HALO + 1 + seq_len] for b in blocks], axis=0)
        return prev, nxt

    for j in range(D_FF // FF_CHUNK):
        cols = slice(j * FF_CHUNK, (j + 1) * FF_CHUNK)
        ge = _dot(h2e, wug_ref[:, cols])
        g_prev, g_next = neighbours(ge)
        g = (g_prev * cw[0:1, cols] + ge[HALO:HALO + tm] * cw[1:2, cols] + g_next * cw[2:3, cols]
             + cb[:, cols])
        val = _dot(h2, wuv_ref[:, cols])
        act_scr[:, cols] = (g * jax.nn.sigmoid(g) * val).astype(BF16)
    y = x + g2 * _dot(act_scr[...], wd_ref[...])
    if final:
        y = _rms(y, gfin_ref[...])
    o_ref[...] = y


def _ffn_call(x, mod4, layer, mod_base, seq_len, tm, latent, final, wl, g_final):
    n = x.shape[0]
    per_seq = seq_len // tm if latent else 1

    def tile(i):
        return (i, 0)

    left, right = _halo_specs(tm, D_MODEL, n)
    consts = [wl["gffn"], wl["wu"], wl["wu"], wl["cw"], wl["cb"], wl["wd"], g_final]
    in_specs = [pl.BlockSpec((tm, D_MODEL), tile), left, right, _mod_spec(layer, mod_base, per_seq, latent),
                _wspec(wl["gffn"], layer), _wspec(wl["wu"], layer, cols=D_FF, col_block=0),
                _wspec(wl["wu"], layer, cols=D_FF, col_block=1), _wspec(wl["cw"], layer),
                _wspec(wl["cb"], layer), _wspec(wl["wd"], layer), _const_spec(g_final.shape)]
    return pl.pallas_call(
        functools.partial(_ffn_body, seq_len, final),
        grid=(n // tm,),
        in_specs=in_specs,
        out_specs=pl.BlockSpec((tm, D_MODEL), tile),
        out_shape=jax.ShapeDtypeStruct((n, D_MODEL), F32),
        scratch_shapes=[pltpu.VMEM((tm, D_FF), BF16)],
        compiler_params=pltpu.CompilerParams(
            dimension_semantics=("arbitrary",), vmem_limit_bytes=VMEM_LIMIT),
        name="ffn_lat" if latent else "ffn_ctx",
    )(x, x, x, mod4, *consts)


def _rot_partner(a):
    d = a.shape[-1]
    return jnp.flip(a.reshape(a.shape[:-1] + (2, 2, d // 4)), axis=-2).reshape(a.shape)


def _prep_weights(g_norm_mix, w_in, g_q_a, w_q_b, g_kv_a, w_kv_b, g_q_gqa, g_k_gqa, w_pool, pool_scale,
                  w_br_a, w_br_b, w_br_c, w_out, g_norm_ffn, w_up, conv_w, conv_b, w_down):
    depth = w_in.shape[0]
    hd = GQA_HEAD_DIM
    grp = GQA_HEADS // GQA_KV_HEADS

    def lane_pad(a, before, width=LANE):
        cfg = [(0, 0)] * (a.ndim - 1) + [(before, width - before - a.shape[-1])]
        return jnp.pad(a, cfg)

    def pair_heads(a):
        lead = a.shape[:-1]
        return a.reshape(lead + (GQA_KV_HEADS, grp, hd)).swapaxes(-3, -2).reshape(lead + (GQA_HEADS * hd,))

    w1 = jnp.concatenate(
        [w_in[:, :, _OFF_QA:_OFF_KR], pair_heads(w_in[:, :, _OFF_GQ:_OFF_GK]), w_in[:, :, _OFF_GK:_OFF_GATE],
         lane_pad(w_in[:, :, _OFF_KR:_OFF_GQ], MLA_NOPE)], axis=-1).astype(BF16)
    assert w1.shape[-1] == _W1_COLS

    qb = w_q_b.reshape(depth, Q_LORA, MLA_HEADS, MLA_NOPE + MLA_ROPE)
    wqb = lane_pad(qb, 0).reshape(depth, Q_LORA, MLA_HEADS * LANE).astype(BF16)
    wqbsw = lane_pad(_rot_partner(qb[..., MLA_NOPE:]), MLA_NOPE).reshape(
        depth, Q_LORA, MLA_HEADS * LANE).astype(BF16)

    kvb = w_kv_b.reshape(depth, KV_LORA, MLA_HEADS, MLA_NOPE + MLA_V)
    wk = lane_pad(kvb[..., :MLA_NOPE], 0).reshape(depth, KV_LORA, MLA_HEADS * LANE).astype(BF16)
    wv = kvb[..., MLA_NOPE:].reshape(depth, KV_LORA, MLA_HEADS * MLA_V).astype(BF16)

    pk = np.zeros((LANE, MLA_HEADS * LANE), np.float32)
    for hh in range(MLA_HEADS):
        for t in range(MLA_ROPE):
            pk[MLA_NOPE + t, hh * LANE + MLA_NOPE + t] = 1.0
    wkp = jnp.concatenate([wk, jnp.broadcast_to(jnp.asarray(pk, BF16), (depth,) + pk.shape)], axis=1)

    wp = w_pool.reshape(depth, len(POOL_WINDOWS) // 2, 2, POOL_GROUP, POOL_GROUP)
    zero = jnp.zeros_like(wp[:, :, 0])
    wpool2 = jnp.concatenate([jnp.concatenate([wp[:, :, 0], zero], axis=-1),
                              jnp.concatenate([zero, wp[:, :, 1]], axis=-1)], axis=-2).astype(BF16)

    row = lambda a: a[:, None, :]
    two = lambda a: jnp.concatenate([a, a], axis=-1)
    wb = w_br_b.reshape(depth, GQA_KV_HEADS, grp, hd, D_MODEL).swapaxes(1, 2).reshape(depth, GQA_HEADS * hd, D_MODEL)
    return dict(
        gmix=row(g_norm_mix), w1=w1, gqa=row(g_q_a), wqb=wqb, wqbsw=wqbsw, gkva=row(g_kv_a), wkp=wkp,
        wv=wv, gq2=row(two(g_q_gqa)), gk2=row(two(g_k_gqa)),
        wg=w_in[:, :, _OFF_GATE:].astype(BF16), wpool=wpool2, pscale=row(pool_scale),
        wa=w_br_a.astype(BF16), wb=wb.astype(BF16), wc=w_br_c.astype(BF16), wo=w_out.astype(BF16),
        gffn=row(g_norm_ffn), wu=w_up.astype(BF16), cw=conv_w, cb=row(conv_b), wd=w_down.astype(BF16),
    )


def _layer(x, mod4, layer, mod_base, batch, seq_len, latent, final, wl, g_final, tabs, caches):
    tm = TOKEN_TILE
    tq = Q_TILE
    assert tm % seq_len == 0 or seq_len % tm == 0
    pj = _inproj_call(x, mod4, layer, mod_base, seq_len, tm, latent, wl, tabs)
    if caches is not None:
        kc_m, vc_m, kc_g, vc_g = _cachekv_call(caches, layer, wl)
    mla_scale = (MLA_NOPE + MLA_ROPE) ** -0.5
    n = x.shape[0]
    k_g = pj["k_gqa"].reshape(1, n, LANE)
    if latent:
        a_out = _attn_pipe_call(pj["q_mla"], pj["k_mla"], pj["v_mla"], kc_m, vc_m, batch, seq_len, tq,
                                mla_scale, False, "attn_mla_lat")
        b_out = _attn_pipe_call(pj["q_gqa"], k_g, pj["v_gqa"], kc_g, vc_g, batch, seq_len, tq, 1.0, True,
                                "attn_gqa_lat")
    else:
        a_out = _attn_call(pj["q_mla"], pj["k_mla"], pj["v_mla"], batch, seq_len, CTX_GROUP,
                           mla_scale, False, "attn_mla_ctx")
        b_out = _attn_call(pj["q_gqa"], k_g, pj["v_gqa"], batch, seq_len, CTX_GROUP, 1.0, True,
                           "attn_gqa_ctx")
    x1 = _merge_call(x, mod4, layer, mod_base, seq_len, tm, latent, a_out, b_out, pj["pool"], wl)
    x2 = _ffn_call(x1, mod4, layer, mod_base, seq_len, tm, latent, final, wl, g_final)
    return x2, pj


def kernel(x_prompt, x_sample, c, cache_mla_ckv, cache_mla_krope, cache_gqa_k, cache_gqa_v, c_ctx, w_ada, b_ada, g_norm_mix, w_in, g_q_a, w_q_b, g_kv_a, w_kv_b, g_q_gqa, g_k_gqa, w_pool, pool_scale, w_br_a, w_br_b, w_br_c, w_out, g_norm_ffn, w_up, conv_w, conv_b, w_down, g_final):
    depth = w_in.shape[0]
    bc, tc, _ = x_prompt.shape
    bl, tl, _ = x_sample.shape
    past = cache_mla_ckv.shape[2]
    mod_rows = 8
    assert 1 + bl <= mod_rows and tc & (tc - 1) == 0 and tl & (tl - 1) == 0

    cin = jnp.concatenate([c_ctx[None, :], c, jnp.zeros((mod_rows - 1 - bl, D_MODEL), F32)], axis=0)
    mod4 = _ada_call(cin, w_ada, b_ada).reshape(depth, mod_rows, 1, 6 * D_MODEL)
    tabs = _rope_tables(tl)
    g_fin = g_final.reshape(1, D_MODEL)
    wl = _prep_weights(g_norm_mix, w_in, g_q_a, w_q_b, g_kv_a, w_kv_b, g_q_gqa, g_k_gqa, w_pool,
                       pool_scale, w_br_a, w_br_b, w_br_c, w_out, g_norm_ffn, w_up, conv_w, conv_b, w_down)
    kr_pad = jnp.pad(cache_mla_krope, ((0, 0), (0, 0), (0, 0), (MLA_NOPE, LANE - MLA_NOPE - MLA_ROPE)))
    caches = (cache_mla_ckv, kr_pad, cache_gqa_k.reshape(bl, depth, past, LANE),
              cache_gqa_v.reshape(bl, depth, past, LANE))

    xc = x_prompt.reshape(bc * tc, D_MODEL)
    xl = x_sample.reshape(bl * tl, D_MODEL)
    st_ckv, st_kr, st_k, st_v = [], [], [], []
    for l in range(depth):
        final = l == depth - 1
        xc, pj = _layer(xc, mod4, l, 0, bc, tc, False, final, wl, g_fin, None, None)
        st_ckv.append(pj["ckv_out"].reshape(bc, tc, KV_LORA))
        st_kr.append(pj["kr_out"].reshape(bc, tc, MLA_ROPE))
        st_k.append(pj["kg_out"].reshape(bc, tc, GQA_KV_HEADS, GQA_HEAD_DIM))
        st_v.append(pj["vg_out"].reshape(bc, tc, GQA_KV_HEADS, GQA_HEAD_DIM))
        xl, _ = _layer(xl, mod4, l, 1, bl, tl, True, final, wl, g_fin, tabs, caches)

    return (xc.reshape(bc, tc, D_MODEL), xl.reshape(bl, tl, D_MODEL),
            jnp.stack(st_ckv, axis=1), jnp.stack(st_kr, axis=1),
            jnp.stack(st_k, axis=1), jnp.stack(st_v, axis=1))
```

```python
import functools
import math

import numpy as np
import jax
import jax.numpy as jnp
from jax import lax
from jax.experimental import pallas as pl
from jax.experimental.pallas import tpu as pltpu

D_MODEL = 1024
GRID_W = 64
RMS_EPS = 1e-6
ROPE_THETA = 10000.0
MLA_HEADS = 8
MLA_NOPE = 64
MLA_ROPE = 32
MLA_V = 64
Q_LORA = 256
KV_LORA = 128
GQA_HEADS = 8
GQA_KV_HEADS = 2
GQA_HEAD_DIM = 64
POOL_WINDOWS = (2, 4, 8, 16)
POOL_GROUP = 128
POOL_WIDTH = POOL_GROUP * len(POOL_WINDOWS)
D_FF = 2816

LANE = 128
HALO = 8
FF_CHUNK = 256
KEY_TILE = 256
CTX_GROUP = 8
CTX_Q_ROWS = 256
TOKEN_TILE = 512
Q_TILE = 512
LAT_PAIRS = 2
N_PAIRS = MLA_HEADS // 2
VMEM_LIMIT = 56 * 1024 * 1024

BF16 = jnp.bfloat16
F32 = jnp.float32

_OFF_QA = 0
_OFF_CKV = _OFF_QA + Q_LORA
_OFF_KR = _OFF_CKV + KV_LORA
_OFF_GQ = _OFF_KR + MLA_ROPE
_OFF_GK = _OFF_GQ + GQA_HEADS * GQA_HEAD_DIM
_OFF_GV = _OFF_GK + GQA_KV_HEADS * GQA_HEAD_DIM
_OFF_POOL = _OFF_GV + GQA_KV_HEADS * GQA_HEAD_DIM
_OFF_GATE = _OFF_POOL + POOL_WIDTH

_W1_QA = 0
_W1_CKV = 256
_W1_GQ = 384
_W1_GK = 896
_W1_GV = 1024
_W1_POOL = 1152
_W1_KR = 1664
_W1_COLS = 1792


def _const_spec(shape):
    nd = len(shape)
    return pl.BlockSpec(shape, lambda *_: (0,) * nd, pipeline_mode=pl.Buffered(1))


def _wspec(a, layer, cols=None, col_block=0):
    shape = a.shape[1:] if cols is None else a.shape[1:-1] + (cols,)
    idx = (layer,) + (0,) * (len(shape) - 1) + (col_block,)
    return pl.BlockSpec((None,) + shape, lambda *_: idx, pipeline_mode=pl.Buffered(1))


def _rms(x, g):
    ms = jnp.mean(x * x, axis=-1, keepdims=True)
    return x * lax.rsqrt(ms + RMS_EPS) * g


def _dot(a, b):
    return jnp.dot(a, b, preferred_element_type=F32)


def _rope_head_tables(n_tokens, d):
    dim_axis = d // 2
    t = np.arange(n_tokens)
    row = (t // GRID_W).astype(np.float64)
    col = (t % GRID_W).astype(np.float64)
    freqs = ROPE_THETA ** (-np.arange(0, dim_axis, 2, dtype=np.float64) / dim_axis)
    ar = row[:, None] * freqs[None, :]
    ac = col[:, None] * freqs[None, :]
    cos = np.concatenate([np.cos(ar), np.cos(ar), np.cos(ac), np.cos(ac)], axis=1)
    sin = np.concatenate([-np.sin(ar), np.sin(ar), -np.sin(ac), np.sin(ac)], axis=1)
    return cos, sin


def _rope_tables(n_tokens):
    cg, sg = _rope_head_tables(n_tokens, GQA_HEAD_DIM)
    cg = np.concatenate([cg, cg], axis=1)
    sg = np.concatenate([sg, sg], axis=1)
    cm32, sm32 = _rope_head_tables(n_tokens, MLA_ROPE)
    pad = LANE - MLA_NOPE - MLA_ROPE
    cm = np.concatenate([np.ones((n_tokens, MLA_NOPE)), cm32, np.zeros((n_tokens, pad))], axis=1)
    sm = np.concatenate([np.zeros((n_tokens, MLA_NOPE)), sm32, np.zeros((n_tokens, pad))], axis=1)
    return tuple(jnp.asarray(a, F32) for a in (cg, sg, cm, sm))


def _ada_body(c_ref, w_ref, b_ref, o_ref):
    c = c_ref[...]
    s = (c * jax.nn.sigmoid(c)).astype(BF16)
    o_ref[0] = _dot(s, w_ref[0].astype(BF16)) + b_ref[0]


def _ada_call(cin, w_ada, b_ada):
    depth, d, n = w_ada.shape
    cols = 2048
    rows = cin.shape[0]
    return pl.pallas_call(
        _ada_body,
        grid=(depth, n // cols),
        in_specs=[
            pl.BlockSpec((rows, d), lambda l, j: (0, 0)),
            pl.BlockSpec((1, d, cols), lambda l, j: (l, 0, j)),
            pl.BlockSpec((1, 1, cols), lambda l, j: (l, 0, j)),
        ],
        out_specs=pl.BlockSpec((1, rows, cols), lambda l, j: (l, 0, j)),
        out_shape=jax.ShapeDtypeStruct((depth, rows, n), F32),
        compiler_params=pltpu.CompilerParams(
            dimension_semantics=("arbitrary", "arbitrary"), vmem_limit_bytes=VMEM_LIMIT),
        name="ada_mod",
    )(cin, w_ada, b_ada.reshape(depth, 1, n))


def _two_head_rsqrt(xb, lo):
    x2 = xb * xb
    s_lo = jnp.sum(jnp.where(lo, x2, 0.0), axis=-1, keepdims=True)
    s_hi = jnp.sum(jnp.where(lo, 0.0, x2), axis=-1, keepdims=True)
    ms = jnp.where(lo, s_lo, s_hi) * (1.0 / GQA_HEAD_DIM)
    return lax.rsqrt(ms + RMS_EPS)


def _rot_partner_lanes(xb, lane, quarter):
    first = jnp.bitwise_and(lane, quarter) == 0
    return jnp.where(first, pltpu.roll(xb, LANE - quarter, axis=1), pltpu.roll(xb, quarter, axis=1))


def _store_value_heads(v_ref, first, blk, lo):
    v_ref[first] = jnp.where(lo, blk, 1.0).astype(BF16)
    v_ref[first + 1] = jnp.where(lo, 1.0, blk).astype(BF16)


def _inproj_body(latent, names, *refs):
    r = dict(zip(names, refs))
    x = r["x"][...]
    tm = x.shape[0]
    mod = r["mod"][0]
    sh1 = mod[:, 0:D_MODEL]
    sc1 = mod[:, D_MODEL:2 * D_MODEL]
    h = (_rms(x, r["gmix"][...]) * (1.0 + sc1) + sh1).astype(BF16)
    z = _dot(h, r["w1"][...])

    lane = lax.broadcasted_iota(jnp.int32, (tm, LANE), 1)
    lo = lane < GQA_HEAD_DIM

    if latent:
        cg = r["cg"][...]
        sg = r["sg"][...]
        cm = r["cm"][...]
        sm = r["sm"][...]

    qn = _rms(z[:, _W1_QA:_W1_QA + Q_LORA], r["gqa"][...]).astype(BF16)
    qm = _dot(qn, r["wqb"][...])
    if latent:
        qms = _dot(qn, r["wqbsw"][...])
    for hh in range(MLA_HEADS):
        blk = qm[:, hh * LANE:(hh + 1) * LANE]
        if latent:
            blk = blk * cm + qms[:, hh * LANE:(hh + 1) * LANE] * sm
        r["q_mla"][hh] = blk.astype(BF16)

    ckv = _rms(z[:, _W1_CKV:_W1_CKV + KV_LORA], r["gkva"][...])
    kr = z[:, _W1_KR:_W1_KR + LANE]
    if latent:
        kr = kr * cm + _rot_partner_lanes(kr, lane, MLA_ROPE // 4) * sm
    else:
        r["ckv_out"][...] = ckv
        r["kr_out"][...] = kr[:, MLA_NOPE:MLA_NOPE + MLA_ROPE]
    ckv_b = ckv.astype(BF16)
    kfull = _dot(jnp.concatenate([ckv_b, kr.astype(BF16)], axis=1), r["wkp"][...])
    for hh in range(MLA_HEADS):
        r["k_mla"][hh] = kfull[:, hh * LANE:(hh + 1) * LANE].astype(BF16)
    vfull = _dot(ckv_b, r["wv"][...])
    for pp in range(N_PAIRS):
        _store_value_heads(r["v_mla"], 2 * pp, vfull[:, pp * LANE:(pp + 1) * LANE], lo)

    gq2 = r["gq2"][...]
    q_scale = GQA_HEAD_DIM ** -0.5
    for j in range(N_PAIRS):
        xb = z[:, _W1_GQ + j * LANE:_W1_GQ + (j + 1) * LANE]
        rs = _two_head_rsqrt(xb, lo)
        y = xb * rs * gq2
        if latent:
            y = y * cg + _rot_partner_lanes(y, lane, GQA_HEAD_DIM // 4) * sg
        y = y * q_scale
        r["q_gqa"][2 * j] = jnp.where(lo, y, 0.0).astype(BF16)
        r["q_gqa"][2 * j + 1] = jnp.where(lo, 0.0, y).astype(BF16)
    kb = z[:, _W1_GK:_W1_GK + LANE]
    rs = _two_head_rsqrt(kb, lo)
    kg = kb * rs * r["gk2"][...]
    if latent:
        kg = kg * cg + _rot_partner_lanes(kg, lane, GQA_HEAD_DIM // 4) * sg
    vg = z[:, _W1_GV:_W1_GV + LANE]
    if not latent:
        r["kg_out"][...] = kg
        r["vg_out"][...] = vg
    r["k_gqa"][...] = kg.astype(BF16)
    _store_value_heads(r["v_gqa"], 0, vg, lo)

    r["pool"][...] = z[:, _W1_POOL:_W1_POOL + POOL_WIDTH]


def _inproj_call(x, mod4, layer, mod_base, seq_len, tm, latent, wl, tabs):
    n = x.shape[0]
    nt = n // tm
    per_seq = seq_len // tm if latent else 1

    def tile(i):
        return (i, 0)

    def mod_idx(i):
        if latent:
            return (layer, mod_base + i // per_seq, 0, 0)
        return (layer, mod_base, 0, 0)

    names = ["x", "mod", "w1", "gmix", "gqa", "wqb", "gkva", "wkp", "wv", "gq2", "gk2"]
    args = [x, mod4, wl["w1"]] + [wl[nm] for nm in names[3:]]
    specs = [pl.BlockSpec((tm, D_MODEL), tile), pl.BlockSpec((None, 1, 1, 6 * D_MODEL), mod_idx),
             _wspec(wl["w1"], layer)]
    specs += [_wspec(a, layer) for a in args[3:]]
    if latent:
        names.append("wqbsw")
        args.append(wl["wqbsw"])
        specs.append(_wspec(wl["wqbsw"], layer))
        for nm, a in zip(("cg", "sg", "cm", "sm"), tabs):
            names.append(nm)
            args.append(a)
            specs.append(pl.BlockSpec((tm, LANE), lambda i: (i % per_seq, 0)))

    head_spec = pl.BlockSpec((MLA_HEADS, tm, LANE), lambda i: (0, i, 0))
    kvh_spec = pl.BlockSpec((GQA_KV_HEADS, tm, LANE), lambda i: (0, i, 0))
    out_names = ["q_mla", "k_mla", "v_mla", "q_gqa", "k_gqa", "v_gqa", "pool"]
    out_shapes = [
        jax.ShapeDtypeStruct((MLA_HEADS, n, LANE), BF16),
        jax.ShapeDtypeStruct((MLA_HEADS, n, LANE), BF16),
        jax.ShapeDtypeStruct((MLA_HEADS, n, LANE), BF16),
        jax.ShapeDtypeStruct((GQA_HEADS, n, LANE), BF16),
        jax.ShapeDtypeStruct((n, LANE), BF16),
        jax.ShapeDtypeStruct((GQA_KV_HEADS, n, LANE), BF16),
        jax.ShapeDtypeStruct((n, POOL_WIDTH), F32),
    ]
    out_specs = [head_spec, head_spec, head_spec, head_spec,
                 pl.BlockSpec((tm, LANE), tile), kvh_spec,
                 pl.BlockSpec((tm, POOL_WIDTH), tile)]
    if not latent:
        out_names += ["ckv_out", "kr_out", "kg_out", "vg_out"]
        out_shapes += [jax.ShapeDtypeStruct((n, KV_LORA), F32), jax.ShapeDtypeStruct((n, MLA_ROPE), F32),
                       jax.ShapeDtypeStruct((n, LANE), F32), jax.ShapeDtypeStruct((n, LANE), F32)]
        out_specs += [pl.BlockSpec((tm, KV_LORA), tile), pl.BlockSpec((tm, MLA_ROPE), tile),
                      pl.BlockSpec((tm, LANE), tile), pl.BlockSpec((tm, LANE), tile)]

    outs = pl.pallas_call(
        functools.partial(_inproj_body, latent, names + out_names),
        grid=(nt,),
        in_specs=specs,
        out_specs=out_specs,
        out_shape=out_shapes,
        compiler_params=pltpu.CompilerParams(
            dimension_semantics=("arbitrary",), vmem_limit_bytes=VMEM_LIMIT),
        name="inproj_lat" if latent else "inproj_ctx",
    )(*args)
    return dict(zip(out_names, outs))


def _cachekv_body(ckv_ref, kr_ref, kg_ref, vg_ref, wkp_ref, wv_ref, k_ref, v_ref, kgo_ref, vgo_ref):
    rows = ckv_ref.shape[0] * ckv_ref.shape[1]
    ckv_b = ckv_ref[...].reshape(rows, LANE).astype(BF16)
    lo = lax.broadcasted_iota(jnp.int32, (rows, LANE), 1) < MLA_V
    kr_b = kr_ref[...].reshape(rows, LANE).astype(BF16)
    kfull = _dot(jnp.concatenate([ckv_b, kr_b], axis=1), wkp_ref[...])
    kgo_ref[0] = kg_ref[...].reshape(rows, LANE).astype(BF16)
    for hh in range(MLA_HEADS):
        k_ref[hh] = kfull[:, hh * LANE:(hh + 1) * LANE].astype(BF16)
    vfull = _dot(ckv_b, wv_ref[...])
    for pp in range(N_PAIRS):
        _store_value_heads(v_ref, 2 * pp, vfull[:, pp * LANE:(pp + 1) * LANE], lo)
    _store_value_heads(vgo_ref, 0, vg_ref[...].reshape(rows, LANE), lo)


def _cachekv_call(caches, layer, wl):
    bsz, _, past, _ = caches[0].shape
    n = bsz * past
    cache_spec = pl.BlockSpec((bsz, None, past, LANE), lambda i: (0, layer, 0, 0))
    return pl.pallas_call(
        _cachekv_body,
        grid=(1,),
        in_specs=[cache_spec] * 4 + [_wspec(wl["wkp"], layer), _wspec(wl["wv"], layer)],
        out_specs=[_const_spec((MLA_HEADS, n, LANE)), _const_spec((MLA_HEADS, n, LANE)),
                   _const_spec((1, n, LANE)), _const_spec((GQA_KV_HEADS, n, LANE))],
        out_shape=[jax.ShapeDtypeStruct((MLA_HEADS, n, LANE), BF16),
                   jax.ShapeDtypeStruct((MLA_HEADS, n, LANE), BF16),
                   jax.ShapeDtypeStruct((1, n, LANE), BF16),
                   jax.ShapeDtypeStruct((GQA_KV_HEADS, n, LANE), BF16)],
        compiler_params=pltpu.CompilerParams(
            dimension_semantics=("arbitrary",), vmem_limit_bytes=VMEM_LIMIT),
        name="cache_kv",
    )(*caches, wl["wkp"], wl["wv"])


def _normalise_pair(acc_a, acc_b, lo):
    num = jnp.where(lo, acc_a, acc_b)
    den = pltpu.roll(jnp.where(lo, acc_b, acc_a), MLA_V, axis=1)
    return num / den


def _attn_body(group, seq_len, shared_kv, exp_scale, q_ref, k_ref, v_ref, o_ref):
    lo = lax.broadcasted_iota(jnp.int32, (CTX_Q_ROWS, LANE), 1) < MLA_V
    nt_dims = (((1,), (1,)), ((), ()))
    units = [(pr, qb, e) for pr in range(N_PAIRS) for qb in range(seq_len // CTX_Q_ROWS) for e in range(2)]

    def one_batch_row(g, carry):
        base = pl.multiple_of(g * seq_len, seq_len)
        rows = pl.ds(base, seq_len)

        scores, row_max, accs = {}, {}, {}
        for i in range(len(units) + 2):
            if i < len(units):
                pr, qb, e = units[i]
                qrows = pl.ds(base + qb * CTX_Q_ROWS, CTX_Q_ROWS)
                q = q_ref[2 * pr + e, qrows, :]
                k = k_ref[0 if shared_kv else 2 * pr + e, rows, :]
                scores[i] = lax.dot_general(q, k, nt_dims, preferred_element_type=F32)
            if 0 <= i - 1 < len(units):
                row_max[i - 1] = jnp.max(scores[i - 1], axis=-1, keepdims=True)
            if 0 <= i - 2 < len(units):
                u = i - 2
                pr, qb, e = units[u]
                qrows = pl.ds(base + qb * CTX_Q_ROWS, CTX_Q_ROWS)
                v = v_ref[e if shared_kv else 2 * pr + e, rows, :]
                p = jnp.exp2((scores.pop(u) - row_max.pop(u)) * exp_scale)
                accs[u] = _dot(p.astype(BF16), v)
                if e == 1:
                    o_ref[pr, qrows, :] = _normalise_pair(accs.pop(u - 1), accs.pop(u), lo).astype(BF16)
        return carry

    lax.fori_loop(0, group, one_batch_row, 0)


def _attn_call(q, k, v, batch, seq_len, group, scale, shared_kv, name):
    n = q.shape[1]
    rows = group * seq_len
    exp_scale = scale * math.log2(math.e)

    def spec(heads):
        return pl.BlockSpec((heads, rows, LANE), lambda b: (0, b, 0))

    return pl.pallas_call(
        functools.partial(_attn_body, group, seq_len, shared_kv, exp_scale),
        grid=(batch // group,),
        in_specs=[spec(q.shape[0]), spec(k.shape[0]), spec(v.shape[0])],
        out_specs=pl.BlockSpec((N_PAIRS, rows, LANE), lambda b: (0, b, 0)),
        out_shape=jax.ShapeDtypeStruct((N_PAIRS, n, LANE), BF16),
        compiler_params=pltpu.CompilerParams(
            dimension_semantics=("arbitrary",), vmem_limit_bytes=VMEM_LIMIT),
        name=name,
    )(q, k, v)


def _attn_pipe_body(shared_kv, exp_scale, tq, pairs, q_ref, k_ref, v_ref, kc_ref, vc_ref, o_ref,
                    s0, s1, mr0, mr1, ac0, ac1):
    seq = k_ref.shape[1]
    past = kc_ref.shape[1]
    nq = seq // tq
    n_units = pairs * nq
    n_new = seq // KEY_TILE
    n_tiles = n_new + past // KEY_TILE
    s_scr = (s0, s1)
    mrun_scr = (mr0, mr1)
    acc_scr = (ac0, ac1)
    nt_dims = (((1,), (1,)), ((), ()))
    lo = lax.broadcasted_iota(jnp.int32, (tq, LANE), 1) < MLA_V

    def pair_rows(t):
        j = t % nq
        return t // nq, pl.ds(pl.multiple_of(j * tq, tq), tq)

    def kv_tile(new_ref, cache_ref, head, blk):
        if blk < n_new:
            return new_ref[head, blk * KEY_TILE:(blk + 1) * KEY_TILE, :]
        blk -= n_new
        return cache_ref[head, blk * KEY_TILE:(blk + 1) * KEY_TILE, :]

    def finish_prev(e_prev, t_prev):
        if e_prev == 1:
            pair, rows = pair_rows(t_prev)
            o_ref[pair, rows, :] = _normalise_pair(ac0[...], ac1[...], lo).astype(BF16)

    def region(t_scores, e_scores, t_cur, e_cur, t_prev):
        if t_prev is not None:
            finish_prev(1 - e_cur, t_prev)
        if e_scores is not None:
            pair, rows = pair_rows(t_scores)
            q = q_ref[2 * pair + e_scores, rows, :]
            k_head = 0 if shared_kv else 2 * pair + e_scores
        if e_cur is not None:
            v_head = e_cur if shared_kv else 2 * (t_cur // nq) + e_cur
            m = jnp.max(mrun_scr[e_cur][...], axis=-1, keepdims=True)
            m_b = jnp.broadcast_to(m, (tq, LANE))
        acc = None
        for blk in range(n_tiles):
            cols = slice(blk * KEY_TILE, (blk + 1) * KEY_TILE)
            if e_scores is not None:
                s_t = lax.dot_general(q, kv_tile(k_ref, kc_ref, k_head, blk), nt_dims,
                                      preferred_element_type=F32)
                s_scr[e_scores][:, cols] = s_t
                m_t = jnp.maximum(s_t[:, 0:LANE], s_t[:, LANE:2 * LANE])
                if blk > 0:
                    m_t = jnp.maximum(m_t, mrun_scr[e_scores][...])
                mrun_scr[e_scores][...] = m_t
            if e_cur is not None:
                p_parts = []
                for hh in range(KEY_TILE // LANE):
                    c0 = blk * KEY_TILE + hh * LANE
                    s_h = s_scr[e_cur][:, c0:c0 + LANE]
                    p_parts.append(jnp.exp2((s_h - m_b) * exp_scale).astype(BF16))
                p_t = jnp.concatenate(p_parts, axis=1)
                part = _dot(p_t, kv_tile(v_ref, vc_ref, v_head, blk))
                acc = part if acc is None else acc + part
        if e_cur is not None:
            acc_scr[e_cur][...] = acc

    region(0, 0, None, None, None)
    region(0, 1, 0, 0, None)
    region(1, 0, 0, 1, 0)

    per_trip = 1 if shared_kv else 2
    assert (n_units - 2) % per_trip == 0

    def body(i, carry):
        for t in range(per_trip):
            t = 1 + per_trip * i + t
            region(t, 1, t, 0, t - 1)
            region(t + 1, 0, t, 1, t)
        return carry

    lax.fori_loop(0, (n_units - 2) // per_trip, body, 0)
    region(n_units - 1, 1, n_units - 1, 0, n_units - 2)
    region(None, None, n_units - 1, 1, n_units - 1)
    finish_prev(1, n_units - 1)


def _attn_pipe_call(q, k, v, kc, vc, batch, seq_len, tq, scale, shared_kv, name, pairs=LAT_PAIRS):
    n = q.shape[1]
    past = kc.shape[1] // batch
    exp_scale = scale * math.log2(math.e)

    def spec(a, rows):
        heads = a.shape[0]
        if heads < 2 * N_PAIRS:
            return pl.BlockSpec((heads, rows, LANE), lambda b, p: (0, b, 0))
        return pl.BlockSpec((2 * pairs, rows, LANE), lambda b, p: (p, b, 0))

    total = seq_len + past
    return pl.pallas_call(
        functools.partial(_attn_pipe_body, shared_kv, exp_scale, tq, pairs),
        grid=(batch, N_PAIRS // pairs),
        in_specs=[spec(q, seq_len), spec(k, seq_len), spec(v, seq_len), spec(kc, past), spec(vc, past)],
        out_specs=pl.BlockSpec((pairs, seq_len, LANE), lambda b, p: (p, b, 0)),
        out_shape=jax.ShapeDtypeStruct((N_PAIRS, n, LANE), BF16),
        scratch_shapes=[pltpu.VMEM((tq, total), F32), pltpu.VMEM((tq, total), F32)]
        + [pltpu.VMEM((tq, LANE), F32) for _ in range(4)],
        compiler_params=pltpu.CompilerParams(
            dimension_semantics=("arbitrary", "arbitrary"), vmem_limit_bytes=VMEM_LIMIT),
        name=name,
    )(q, k, v, kc, vc)


def _seq_pos(tm, seq_len):
    i = pl.program_id(0)
    row = lax.broadcasted_iota(jnp.int32, (tm, 1), 0) + i * tm
    return jnp.bitwise_and(row, seq_len - 1)


def _merge_body(seq_len, x_ref, mod_ref, gmix_ref, wg_ref, a_ref, b_ref, pc_ref, pl_ref, pr_ref,
                wpool_ref, pscale_ref, wa_ref, wb_ref, wc_ref, wo_ref, o_ref, merged_scr, ab_scr):
    x = x_ref[...]
    tm = x.shape[0]
    mod = mod_ref[0]
    sh1 = mod[:, 0:D_MODEL]
    sc1 = mod[:, D_MODEL:2 * D_MODEL]
    g1 = mod[:, 2 * D_MODEL:3 * D_MODEL]
    h = (_rms(x, gmix_ref[...]) * (1.0 + sc1) + sh1).astype(BF16)

    pos = _seq_pos(tm, seq_len)
    pscale = pscale_ref[...]
    centre_rows = pc_ref[...]
    if seq_len >= tm:
        start = (pl.program_id(0) * tm) % seq_len
        left = jnp.where(start == 0, 0.0, pl_ref[...])
        right = jnp.where(start + tm == seq_len, 0.0, pr_ref[...])
        blocks = [jnp.concatenate([left, centre_rows, right], axis=0)]
    else:
        zeros = jnp.zeros((HALO, POOL_WIDTH), F32)
        blocks = [jnp.concatenate([zeros, centre_rows[s * seq_len:(s + 1) * seq_len], zeros], axis=0)
                  for s in range(tm // seq_len)]

    def pool_group(g):
        w = POOL_WINDOWS[g]
        gcols = slice(g * POOL_GROUP, (g + 1) * POOL_GROUP)
        sums = []
        for blk in blocks:
            rows = blk.shape[0] - 2 * HALO
            eg = blk[:, gcols]
            wsum = eg[HALO:HALO + rows]
            for dlt in range(-(w // 2), w - w // 2):
                if dlt != 0:
                    wsum = wsum + eg[HALO + dlt:HALO + dlt + rows]
            sums.append(wsum)
        wsum = sums[0] if len(sums) == 1 else jnp.concatenate(sums, axis=0)
        cnt = jnp.minimum(pos + (w - w // 2), seq_len) - jnp.maximum(pos - w // 2, 0)
        return (wsum / cnt.astype(F32) - centre_rows[:, gcols]).astype(BF16)

    def pool_pair_map(p, pooled_a, pooled_b):
        both = jnp.concatenate([pooled_a, pooled_b], axis=1)
        return _dot(both, wpool_ref[p]) * pscale[:, 2 * p * POOL_GROUP:2 * (p + 1) * POOL_GROUP]

    def gated(branch, src, w_ref, j):
        c0 = branch * D_MODEL + j * FF_CHUNK
        gate = jax.nn.sigmoid(_dot(h, wg_ref[:, c0:c0 + FF_CHUNK]))
        return gate * _dot(src, w_ref[:, j * FF_CHUNK:(j + 1) * FF_CHUNK])

    a_out = jnp.concatenate([a_ref[p] for p in range(N_PAIRS)], axis=1)
    b_out = jnp.concatenate([b_ref[p] for p in range(N_PAIRS)], axis=1)
    n_chunks = D_MODEL // FF_CHUNK
    pooled = []
    c_parts = []
    for j in range(n_chunks):
        cols = slice(j * FF_CHUNK, (j + 1) * FF_CHUNK)
        for g in range(j * len(POOL_WINDOWS) // n_chunks, (j + 1) * len(POOL_WINDOWS) // n_chunks):
            pooled.append(pool_group(g))
            if g % 2 == 1:
                c_parts.append(pool_pair_map(g // 2, pooled[g - 1], pooled[g]))
        ab_scr[:, cols] = gated(0, a_out, wa_ref, j) + gated(1, b_out, wb_ref, j)
    c_out = jnp.concatenate(c_parts, axis=-1).astype(BF16)
    for j in range(n_chunks):
        cols = slice(j * FF_CHUNK, (j + 1) * FF_CHUNK)
        merged_scr[:, cols] = (ab_scr[:, cols] + gated(2, c_out, wc_ref, j)).astype(BF16)
    o_ref[...] = x + g1 * _dot(merged_scr[...], wo_ref[...])


def _halo_specs(tm, width, n):
    blocks = tm // HALO
    last = n // HALO - 1
    left = pl.BlockSpec((HALO, width), lambda i: (jnp.maximum(i * blocks - 1, 0), 0))
    right = pl.BlockSpec((HALO, width), lambda i: (jnp.minimum((i + 1) * blocks, last), 0))
    return left, right


def _mod_spec(layer, mod_base, per_seq, latent):
    if latent:
        return pl.BlockSpec((None, 1, 1, 6 * D_MODEL), lambda i: (layer, mod_base + i // per_seq, 0, 0))
    return pl.BlockSpec((None, 1, 1, 6 * D_MODEL), lambda i: (layer, mod_base, 0, 0))


def _merge_call(x, mod4, layer, mod_base, seq_len, tm, latent, a_out, b_out, pool_in, wl):
    n = x.shape[0]
    per_seq = seq_len // tm if latent else 1

    def tile(i):
        return (i, 0)

    left, right = _halo_specs(tm, POOL_WIDTH, n)
    consts = [wl["wpool"], wl["pscale"], wl["wa"], wl["wb"], wl["wc"], wl["wo"]]
    in_specs = [pl.BlockSpec((tm, D_MODEL), tile), _mod_spec(layer, mod_base, per_seq, latent),
                _wspec(wl["gmix"], layer), _wspec(wl["wg"], layer),
                pl.BlockSpec((N_PAIRS, tm, LANE), lambda i: (0, i, 0)),
                pl.BlockSpec((N_PAIRS, tm, LANE), lambda i: (0, i, 0)),
                pl.BlockSpec((tm, POOL_WIDTH), tile), left, right]
    in_specs += [_wspec(a, layer) for a in consts]
    return pl.pallas_call(
        functools.partial(_merge_body, seq_len),
        grid=(n // tm,),
        in_specs=in_specs,
        out_specs=pl.BlockSpec((tm, D_MODEL), tile),
        out_shape=jax.ShapeDtypeStruct((n, D_MODEL), F32),
        scratch_shapes=[pltpu.VMEM((tm, D_MODEL), BF16), pltpu.VMEM((tm, D_MODEL), F32)],
        compiler_params=pltpu.CompilerParams(
            dimension_semantics=("arbitrary",), vmem_limit_bytes=VMEM_LIMIT),
        name="merge_lat" if latent else "merge_ctx",
    )(x, mod4, wl["gmix"], wl["wg"], a_out, b_out, pool_in, pool_in, pool_in, *consts)


def _ffn_body(seq_len, final, x_ref, xl_ref, xr_ref, mod_ref, gffn_ref, wug_ref, wuv_ref, cw_ref, cb_ref,
              wd_ref, gfin_ref, o_ref, act_scr):
    x = x_ref[...]
    tm = x.shape[0]
    mod = mod_ref[0]
    sh2 = mod[:, 3 * D_MODEL:4 * D_MODEL]
    sc2 = mod[:, 4 * D_MODEL:5 * D_MODEL]
    g2 = mod[:, 5 * D_MODEL:6 * D_MODEL]
    xe = jnp.concatenate([xl_ref[...], x, xr_ref[...]], axis=0)
    h2e = _rms(xe, gffn_ref[...]) * (1.0 + sc2) + sh2
    h2 = h2e[HALO:HALO + tm].astype(BF16)
    h2e = h2e.astype(BF16)

    cw = cw_ref[...]
    cb = cb_ref[...]
    if seq_len >= tm:
        pos = _seq_pos(tm, seq_len)
        has_prev = pos >= 1
        has_next = pos <= seq_len - 2

    def neighbours(ge):
        if seq_len >= tm:
            return (jnp.where(has_prev, ge[HALO - 1:HALO - 1 + tm], 0.0),
                    jnp.where(has_next, ge[HALO + 1:HALO + 1 + tm], 0.0))
        zeros = jnp.zeros((HALO, ge.shape[1]), F32)
        blocks = [jnp.concatenate([zeros, ge[HALO + s * seq_len:HALO + (s + 1) * seq_len], zeros], axis=0)
                  for s in range(tm // seq_len)]
        prev = jnp.concatenate([b[HALO - 1:HALO - 1 + seq_len] for b in blocks], axis=0)
        nxt = jnp.concatenate([b[HALO + 1:HALO + 1 + seq_len] for b in blocks], axis=0)
        return prev, nxt

    for j in range(D_FF // FF_CHUNK):
        cols = slice(j * FF_CHUNK, (j + 1) * FF_CHUNK)
        ge = _dot(h2e, wug_ref[:, cols])
        g_prev, g_next = neighbours(ge)
        g = (g_prev * cw[0:1, cols] + ge[HALO:HALO + tm] * cw[1:2, cols] + g_next * cw[2:3, cols]
             + cb[:, cols])
        val = _dot(h2, wuv_ref[:, cols])
        act_scr[:, cols] = (g * jax.nn.sigmoid(g) * val).astype(BF16)
    y = x + g2 * _dot(act_scr[...], wd_ref[...])
    if final:
        y = _rms(y, gfin_ref[...])
    o_ref[...] = y


def _ffn_call(x, mod4, layer, mod_base, seq_len, tm, latent, final, wl, g_final):
    n = x.shape[0]
    per_seq = seq_len // tm if latent else 1

    def tile(i):
        return (i, 0)

    left, right = _halo_specs(tm, D_MODEL, n)
    consts = [wl["gffn"], wl["wu"], wl["wu"], wl["cw"], wl["cb"], wl["wd"], g_final]
    in_specs = [pl.BlockSpec((tm, D_MODEL), tile), left, right, _mod_spec(layer, mod_base, per_seq, latent),
                _wspec(wl["gffn"], layer), _wspec(wl["wu"], layer, cols=D_FF, col_block=0),
                _wspec(wl["wu"], layer, cols=D_FF, col_block=1), _wspec(wl["cw"], layer),
                _wspec(wl["cb"], layer), _wspec(wl["wd"], layer), _const_spec(g_final.shape)]
    return pl.pallas_call(
        functools.partial(_ffn_body, seq_len, final),
        grid=(n // tm,),
        in_specs=in_specs,
        out_specs=pl.BlockSpec((tm, D_MODEL), tile),
        out_shape=jax.ShapeDtypeStruct((n, D_MODEL), F32),
        scratch_shapes=[pltpu.VMEM((tm, D_FF), BF16)],
        compiler_params=pltpu.CompilerParams(
            dimension_semantics=("arbitrary",), vmem_limit_bytes=VMEM_LIMIT),
        name="ffn_lat" if latent else "ffn_ctx",
    )(x, x, x, mod4, *consts)


def _rot_partner(a):
    d = a.shape[-1]
    return jnp.flip(a.reshape(a.shape[:-1] + (2, 2, d // 4)), axis=-2).reshape(a.shape)


def _prep_weights(g_norm_mix, w_in, g_q_a, w_q_b, g_kv_a, w_kv_b, g_q_gqa, g_k_gqa, w_pool, pool_scale,
                  w_br_a, w_br_b, w_br_c, w_out, g_norm_ffn, w_up, conv_w, conv_b, w_down):
    depth = w_in.shape[0]
    hd = GQA_HEAD_DIM
    grp = GQA_HEADS // GQA_KV_HEADS

    def lane_pad(a, before, width=LANE):
        cfg = [(0, 0)] * (a.ndim - 1) + [(before, width - before - a.shape[-1])]
        return jnp.pad(a, cfg)

    def pair_heads(a):
        lead = a.shape[:-1]
        return a.reshape(lead + (GQA_KV_HEADS, grp, hd)).swapaxes(-3, -2).reshape(lead + (GQA_HEADS * hd,))

    w1 = jnp.concatenate(
        [w_in[:, :, _OFF_QA:_OFF_KR], pair_heads(w_in[:, :, _OFF_GQ:_OFF_GK]), w_in[:, :, _OFF_GK:_OFF_GATE],
         lane_pad(w_in[:, :, _OFF_KR:_OFF_GQ], MLA_NOPE)], axis=-1).astype(BF16)
    assert w1.shape[-1] == _W1_COLS

    qb = w_q_b.reshape(depth, Q_LORA, MLA_HEADS, MLA_NOPE + MLA_ROPE)
    wqb = lane_pad(qb, 0).reshape(depth, Q_LORA, MLA_HEADS * LANE).astype(BF16)
    wqbsw = lane_pad(_rot_partner(qb[..., MLA_NOPE:]), MLA_NOPE).reshape(
        depth, Q_LORA, MLA_HEADS * LANE).astype(BF16)

    kvb = w_kv_b.reshape(depth, KV_LORA, MLA_HEADS, MLA_NOPE + MLA_V)
    wk = lane_pad(kvb[..., :MLA_NOPE], 0).reshape(depth, KV_LORA, MLA_HEADS * LANE).astype(BF16)
    wv = kvb[..., MLA_NOPE:].reshape(depth, KV_LORA, MLA_HEADS * MLA_V).astype(BF16)

    pk = np.zeros((LANE, MLA_HEADS * LANE), np.float32)
    for hh in range(MLA_HEADS):
        for t in range(MLA_ROPE):
            pk[MLA_NOPE + t, hh * LANE + MLA_NOPE + t] = 1.0
    wkp = jnp.concatenate([wk, jnp.broadcast_to(jnp.asarray(pk, BF16), (depth,) + pk.shape)], axis=1)

    wp = w_pool.reshape(depth, len(POOL_WINDOWS) // 2, 2, POOL_GROUP, POOL_GROUP)
    zero = jnp.zeros_like(wp[:, :, 0])
    wpool2 = jnp.concatenate([jnp.concatenate([wp[:, :, 0], zero], axis=-1),
                              jnp.concatenate([zero, wp[:, :, 1]], axis=-1)], axis=-2).astype(BF16)

    row = lambda a: a[:, None, :]
    two = lambda a: jnp.concatenate([a, a], axis=-1)
    wb = w_br_b.reshape(depth, GQA_KV_HEADS, grp, hd, D_MODEL).swapaxes(1, 2).reshape(depth, GQA_HEADS * hd, D_MODEL)
    return dict(
        gmix=row(g_norm_mix), w1=w1, gqa=row(g_q_a), wqb=wqb, wqbsw=wqbsw, gkva=row(g_kv_a), wkp=wkp,
        wv=wv, gq2=row(two(g_q_gqa)), gk2=row(two(g_k_gqa)),
        wg=w_in[:, :, _OFF_GATE:].astype(BF16), wpool=wpool2, pscale=row(pool_scale),
        wa=w_br_a.astype(BF16), wb=wb.astype(BF16), wc=w_br_c.astype(BF16), wo=w_out.astype(BF16),
        gffn=row(g_norm_ffn), wu=w_up.astype(BF16), cw=conv_w, cb=row(conv_b), wd=w_down.astype(BF16),
    )


def _layer(x, mod4, layer, mod_base, batch, seq_len, latent, final, wl, g_final, tabs, caches):
    tm = TOKEN_TILE
    tq = Q_TILE
    assert tm % seq_len == 0 or seq_len % tm == 0
    pj = _inproj_call(x, mod4, layer, mod_base, seq_len, tm, latent, wl, tabs)
    if caches is not None:
        kc_m, vc_m, kc_g, vc_g = _cachekv_call(caches, layer, wl)
    mla_scale = (MLA_NOPE + MLA_ROPE) ** -0.5
    n = x.shape[0]
    k_g = pj["k_gqa"].reshape(1, n, LANE)
    if latent:
        a_out = _attn_pipe_call(pj["q_mla"], pj["k_mla"], pj["v_mla"], kc_m, vc_m, batch, seq_len, tq,
                                mla_scale, False, "attn_mla_lat")
        b_out = _attn_pipe_call(pj["q_gqa"], k_g, pj["v_gqa"], kc_g, vc_g, batch, seq_len, tq, 1.0, True,
                                "attn_gqa_lat")
    else:
        a_out = _attn_call(pj["q_mla"], pj["k_mla"], pj["v_mla"], batch, seq_len, CTX_GROUP,
                           mla_scale, False, "attn_mla_ctx")
        b_out = _attn_call(pj["q_gqa"], k_g, pj["v_gqa"], batch, seq_len, CTX_GROUP, 1.0, True,
                           "attn_gqa_ctx")
    x1 = _merge_call(x, mod4, layer, mod_base, seq_len, tm, latent, a_out, b_out, pj["pool"], wl)
    x2 = _ffn_call(x1, mod4, layer, mod_base, seq_len, tm, latent, final, wl, g_final)
    return x2, pj


def kernel(x_prompt, x_sample, c, cache_mla_ckv, cache_mla_krope, cache_gqa_k, cache_gqa_v, c_ctx, w_ada, b_ada, g_norm_mix, w_in, g_q_a, w_q_b, g_kv_a, w_kv_b, g_q_gqa, g_k_gqa, w_pool, pool_scale, w_br_a, w_br_b, w_br_c, w_out, g_norm_ffn, w_up, conv_w, conv_b, w_down, g_final):
    depth = w_in.shape[0]
    bc, tc, _ = x_prompt.shape
    bl, tl, _ = x_sample.shape
    past = cache_mla_ckv.shape[2]
    mod_rows = 8
    assert 1 + bl <= mod_rows and tc & (tc - 1) == 0 and tl & (tl - 1) == 0

    cin = jnp.concatenate([c_ctx[None, :], c, jnp.zeros((mod_rows - 1 - bl, D_MODEL), F32)], axis=0)
    mod4 = _ada_call(cin, w_ada, b_ada).reshape(depth, mod_rows, 1, 6 * D_MODEL)
    tabs = _rope_tables(tl)
    g_fin = g_final.reshape(1, D_MODEL)
    wl = _prep_weights(g_norm_mix, w_in, g_q_a, w_q_b, g_kv_a, w_kv_b, g_q_gqa, g_k_gqa, w_pool,
                       pool_scale, w_br_a, w_br_b, w_br_c, w_out, g_norm_ffn, w_up, conv_w, conv_b, w_down)
    kr_pad = jnp.pad(cache_mla_krope, ((0, 0), (0, 0), (0, 0), (MLA_NOPE, LANE - MLA_NOPE - MLA_ROPE)))
    caches = (cache_mla_ckv, kr_pad, cache_gqa_k.reshape(bl, depth, past, LANE),
              cache_gqa_v.reshape(bl, depth, past, LANE))

    xc = x_prompt.reshape(bc * tc, D_MODEL)
    xl = x_sample.reshape(bl * tl, D_MODEL)
    st_ckv, st_kr, st_k, st_v = [], [], [], []
    for l in range(depth):
        final = l == depth - 1
        xc, pj = _layer(xc, mod4, l, 0, bc, tc, False, final, wl, g_fin, None, None)
        st_ckv.append(pj["ckv_out"].reshape(bc, tc, KV_LORA))
        st_kr.append(pj["kr_out"].reshape(bc, tc, MLA_ROPE))
        st_k.append(pj["kg_out"].reshape(bc, tc, GQA_KV_HEADS, GQA_HEAD_DIM))
        st_v.append(pj["vg_out"].reshape(bc, tc, GQA_KV_HEADS, GQA_HEAD_DIM))
        xl, _ = _layer(xl, mod4, l, 1, bl, tl, True, final, wl, g_fin, tabs, caches)

    return (xc.reshape(bc, tc, D_MODEL), xl.reshape(bl, tl, D_MODEL),
            jnp.stack(st_ckv, axis=1), jnp.stack(st_kr, axis=1),
            jnp.stack(st_k, axis=1), jnp.stack(st_v, axis=1))
```
